```python
import math
import jax
import jax.numpy as jnp
from jax import lax
import numpy as np

D_MODEL = 2048
BATCH = 8
SEQ = 2048
DEPTH = 2
DEC_BATCH = 128
DEC_SEQ = 1
PAST_LEN = 8192
PAGE_SIZE = 128

MIX_WIDTH = D_MODEL
N_MEM = 256
MEM_HEADS = 4
MEM_HD = 128
MEM_Q_DIM = MEM_HEADS * MEM_HD
WINDOW = 128
SWA_HD = 64
SWA_Q_HEADS = (MIX_WIDTH - MEM_Q_DIM) // SWA_HD
SWA_KV_HEADS = 4
SWA_GROUP = SWA_Q_HEADS // SWA_KV_HEADS
SWA_Q_DIM = SWA_Q_HEADS * SWA_HD
SWA_KV_DIM = SWA_KV_HEADS * SWA_HD
N_BUCKETS = 32
MAX_DISTANCE = WINDOW
HG_DK = 128
HG_DV = 128
HG_HEADS = (MIX_WIDTH - MEM_Q_DIM) // HG_DV
HG_K_DIM = HG_HEADS * HG_DK
HG_V_DIM = HG_HEADS * HG_DV
HG_CHUNK = 64
N_SWA_LAYERS = (DEPTH + 1) // 2
N_HG_LAYERS = DEPTH // 2
SWA_SPLITS = (SWA_Q_DIM, SWA_KV_DIM, SWA_KV_DIM, MEM_Q_DIM, MIX_WIDTH)
HG_SPLITS = (HG_K_DIM, HG_K_DIM, HG_V_DIM, MEM_Q_DIM, MIX_WIDTH)
DEEPNORM_ALPHA = (2.0 * DEPTH) ** 0.25
DEEPNORM_BETA = (8.0 * DEPTH) ** -0.25
LN_EPS = 1e-5
RMS_EPS = 1e-6

kernel_name = 'swa_sink_hgrn2_memxattn_hybrid_step'


def _split(u, sizes):
    offs = [int(o) for o in np.cumsum(sizes)[:-1]]
    return jnp.split(u, offs, axis=-1)


def _layer_norm(x, w, b):
    xf = x.astype(jnp.float32)
    mu = jnp.mean(xf, axis=-1, keepdims=True)
    var = jnp.mean(jnp.square(xf - mu), axis=-1, keepdims=True)
    y = (xf - mu) * lax.rsqrt(var + LN_EPS) * w.astype(jnp.float32) + b.astype(jnp.float32)
    return y.astype(x.dtype)


def _t5_bucket(d):
    n = jnp.maximum(d, 0)
    max_exact = N_BUCKETS // 2
    nf = jnp.maximum(n, 1).astype(jnp.float32)
    large = max_exact + (jnp.log(nf / max_exact) / math.log(MAX_DISTANCE / max_exact)
                         * (N_BUCKETS - max_exact)).astype(jnp.int32)
    large = jnp.minimum(large, N_BUCKETS - 1)
    return jnp.where(n < max_exact, n, large)


def _banded_attend(qb, kb, vb, d, valid, sinks, rel_bias):
    n, nq, nk = d.shape
    bias = rel_bias.astype(jnp.float32)[_t5_bucket(d)]
    bias = jnp.transpose(bias, (0, 3, 1, 2)).reshape(n, SWA_KV_HEADS, SWA_GROUP, nq, nk)
    mask = valid & (d >= 0) & (d < WINDOW)
    s = jnp.einsum('bnqhgd,bnkhd->bnhgqk', qb, kb).astype(jnp.float32) * (SWA_HD ** -0.5) + bias[None]
    s = jnp.where(mask[None, :, None, None], s, -jnp.inf)
    sink = sinks.astype(jnp.float32).reshape(1, 1, SWA_KV_HEADS, SWA_GROUP, 1, 1)
    m = jnp.maximum(jnp.max(s, axis=-1, keepdims=True), sink)
    p = jnp.exp(s - m)
    p = p / (jnp.sum(p, axis=-1, keepdims=True) + jnp.exp(sink - m))
    return jnp.einsum('bnhgqk,bnkhd->bnqhgd', p.astype(vb.dtype), vb)


def _swa_prompt(q, k, v, sinks, rel_bias):
    b, t = q.shape[:2]
    nb = t // WINDOW
    qb = q.reshape(b, nb, WINDOW, SWA_KV_HEADS, SWA_GROUP, SWA_HD)
    kc = k.reshape(b, nb, WINDOW, SWA_KV_HEADS, SWA_HD)
    vc = v.reshape(b, nb, WINDOW, SWA_KV_HEADS, SWA_HD)

    def prev(a):
        return jnp.pad(a, ((0, 0), (1, 0), (0, 0), (0, 0), (0, 0)))[:, :-1]

    kb = jnp.concatenate([prev(kc), kc], axis=2)
    vb = jnp.concatenate([prev(vc), vc], axis=2)
    blk = jnp.arange(nb)[:, None, None] * WINDOW
    qpos = blk + jnp.arange(WINDOW)[None, :, None]
    kpos = blk - WINDOW + jnp.arange(2 * WINDOW)[None, None, :]
    o = _banded_attend(qb, kb, vb, qpos - kpos, kpos >= 0, sinks, rel_bias)
    keep = min(WINDOW, t)
    return o.reshape(b, t, SWA_Q_DIM), k[:, t - keep:], v[:, t - keep:]


def _swa_sample(q, k, v, buf_k, buf_v, sinks, rel_bias):
    b, t = q.shape[:2]
    nbuf = buf_k.shape[1]
    kk = jnp.concatenate([buf_k.astype(k.dtype), k], axis=1)
    vv = jnp.concatenate([buf_v.astype(v.dtype), v], axis=1)
    qpos = PAST_LEN + jnp.arange(t)
    kpos = PAST_LEN - nbuf + jnp.arange(nbuf + t)
    d = (qpos[:, None] - kpos[None, :])[None]
    valid = (kpos >= 0)[None, None, :]
    qb = q.reshape(b, 1, t, SWA_KV_HEADS, SWA_GROUP, SWA_HD)
    o = _banded_attend(qb, kk[:, None], vv[:, None], d, valid, sinks, rel_bias)
    return o.reshape(b, t, SWA_Q_DIM), kk[:, -nbuf:], vv[:, -nbuf:]


def _hgrn2_chunked(q, k, v, logf, s0):
    b, t, h, dk = q.shape
    dv = v.shape[-1]
    c = HG_CHUNK if t % HG_CHUNK == 0 else t
    n = t // c

    def blocks(a):
        return jnp.moveaxis(a.reshape(b, n, c, h, a.shape[-1]), 1, 0)

    causal = jnp.tril(jnp.ones((c, c), dtype=bool))[None, :, :, None, None]

    def step(state, xs):
        qc, kc, vc, gc = xs
        g_cum = jnp.cumsum(gc, axis=1)
        diff = g_cum[:, :, None] - g_cum[:, None, :]
        decay = jnp.exp(jnp.where(causal, diff, -jnp.inf))
        scores = jnp.einsum('bthk,btshk,bshk->bhts', qc, decay, kc)
        o = (jnp.einsum('bhts,bshv->bthv', scores, vc)
             + jnp.einsum('bthk,bhkv->bthv', qc * jnp.exp(g_cum), state))
        g_last = g_cum[:, -1]
        k_dec = kc * jnp.exp(g_last[:, None] - g_cum)
        state = jnp.exp(g_last)[..., None] * state + jnp.einsum('bshk,bshv->bhkv', k_dec, vc)
        return state, o

    s_fin, o = lax.scan(step, s0, (blocks(q), blocks(k), blocks(v), blocks(logf)))
    return jnp.moveaxis(o, 0, 1).reshape(b, t, h, dv), s_fin


def _swa_proj(h, w_in):
    b, t = h.shape[:2]
    q, k, v, mq, g = _split(jnp.einsum('btd,de->bte', h, w_in), SWA_SPLITS)
    return (q.reshape(b, t, SWA_Q_HEADS, SWA_HD), k.reshape(b, t, SWA_KV_HEADS, SWA_HD),
            v.reshape(b, t, SWA_KV_HEADS, SWA_HD), mq, g)


def _hgrn_branch(h, w_in, lb, norm_w, s0):
    b, t = h.shape[:2]
    q, f, iv, mq, g = _split(jnp.einsum('btd,de->bte', h, w_in), HG_SPLITS)
    fg = lb[None, None, :] + (1.0 - lb[None, None, :]) * jax.nn.sigmoid(f.astype(jnp.float32))
    qh = jax.nn.silu(q.astype(jnp.float32)).reshape(b, t, HG_HEADS, HG_DK)
    kh = (1.0 - fg).reshape(b, t, HG_HEADS, HG_DK)
    gh = jnp.log(fg).reshape(b, t, HG_HEADS, HG_DK)
    vh = iv.astype(jnp.float32).reshape(b, t, HG_HEADS, HG_DV)
    o, s_new = _hgrn2_chunked(qh, kh, vh, gh, s0.astype(jnp.float32))
    o = o * lax.rsqrt(jnp.mean(jnp.square(o), axis=-1, keepdims=True) + RMS_EPS) \
        * norm_w.astype(jnp.float32).reshape(HG_HEADS, HG_DV)
    return o.reshape(b, t, HG_V_DIM).astype(h.dtype), s_new, mq, g


def _mem_attend(mq, mk, mv):
    s = jnp.einsum('bthd,bmhd->bhtm', mq, mk).astype(jnp.float32) * (MEM_HD ** -0.5)
    p = jax.nn.softmax(s, axis=-1)
    return jnp.einsum('bhtm,bmhd->bthd', p.astype(mv.dtype), mv)


def _finish(h, mix, mq, g, mk, mv, w_out_i, ln_w_i, ln_b_i):
    b, t = h.shape[:2]
    mem_o = _mem_attend(mq.reshape(b, t, MEM_HEADS, MEM_HD), mk.astype(mq.dtype), mv.astype(mq.dtype))
    branch = jnp.concatenate([mix.astype(h.dtype), mem_o.reshape(b, t, MEM_Q_DIM).astype(h.dtype)], axis=-1) \
        * jax.nn.silu(g)
    y = jnp.einsum('bte,ed->btd', branch, w_out_i)
    return _layer_norm(DEEPNORM_ALPHA * h + y, ln_w_i, ln_b_i)


def setup_inputs(seed: int = 0) -> dict:
    key = jax.random.key(seed)
    ks = jax.random.split(key, 20)

    def nrm(k, shape, scale=1.0):
        return jax.random.normal(k, shape, jnp.float32) * scale

    win_buf = min(WINDOW, PAST_LEN)
    d_in = D_MODEL ** -0.5
    return {
        'x_prompt': nrm(ks[0], (BATCH, SEQ, D_MODEL)),
        'x_sample': nrm(ks[1], (DEC_BATCH, DEC_SEQ, D_MODEL)),
        'cache_mem_k': nrm(ks[2], (DEPTH, DEC_BATCH, N_MEM, MEM_HEADS, MEM_HD)),
        'cache_mem_v': nrm(ks[3], (DEPTH, DEC_BATCH, N_MEM, MEM_HEADS, MEM_HD)),
        'cache_swa_k': nrm(ks[4], (N_SWA_LAYERS, DEC_BATCH, win_buf, SWA_KV_HEADS, SWA_HD)),
        'cache_swa_v': nrm(ks[5], (N_SWA_LAYERS, DEC_BATCH, win_buf, SWA_KV_HEADS, SWA_HD)),
        'state_hgrn': nrm(ks[6], (N_HG_LAYERS, DEC_BATCH, HG_HEADS, HG_DK, HG_DV), 0.5),
        'mem_prompt': nrm(ks[7], (BATCH, N_MEM, D_MODEL)),
        'rel_bias': nrm(ks[8], (N_BUCKETS, SWA_Q_HEADS), 0.5),
        'swa_w_in': nrm(ks[9], (N_SWA_LAYERS, D_MODEL, sum(SWA_SPLITS)), d_in),
        'swa_sinks': nrm(ks[10], (N_SWA_LAYERS, SWA_Q_HEADS), 0.5),
        'hg_w_in': nrm(ks[11], (N_HG_LAYERS, D_MODEL, sum(HG_SPLITS)), d_in),
        'hg_lb_logits': nrm(ks[12], (DEPTH, HG_K_DIM)),
        'hg_norm_w': 1.0 + nrm(ks[13], (N_HG_LAYERS, HG_V_DIM), 0.02),
        'w_mem_k': nrm(ks[14], (DEPTH, D_MODEL, MEM_Q_DIM), d_in),
        'w_mem_v': nrm(ks[15], (DEPTH, D_MODEL, MEM_Q_DIM), d_in),
        'w_out': nrm(ks[16], (DEPTH, MIX_WIDTH, D_MODEL), DEEPNORM_BETA * MIX_WIDTH ** -0.5),
        'ln_w': 1.0 + nrm(ks[17], (DEPTH, D_MODEL), 0.02),
        'ln_b': nrm(ks[18], (DEPTH, D_MODEL), 0.02),
    }


def reference(x_prompt, x_sample, cache_mem_k, cache_mem_v, cache_swa_k, cache_swa_v, state_hgrn, mem_prompt,
              rel_bias, swa_w_in, swa_sinks, hg_w_in, hg_lb_logits, hg_norm_w, w_mem_k, w_mem_v, w_out, ln_w, ln_b):
    lb_all = jnp.cumsum(jax.nn.softmax(hg_lb_logits.astype(jnp.float32), axis=0), axis=0)
    lb_all = lb_all - lb_all[0:1]
    bp = x_prompt.shape[0]
    hp, hs = x_prompt, x_sample
    mk_list, mv_list = [], []
    swa_kp, swa_vp, swa_ks, swa_vs = [], [], [], []
    hg_sp, hg_ss = [], []
    for i in range(DEPTH):
        mk_p = jnp.einsum('bmd,de->bme', mem_prompt, w_mem_k[i]).reshape(bp, N_MEM, MEM_HEADS, MEM_HD)
        mv_p = jnp.einsum('bmd,de->bme', mem_prompt, w_mem_v[i]).reshape(bp, N_MEM, MEM_HEADS, MEM_HD)
        mk_list.append(mk_p)
        mv_list.append(mv_p)
        j = i // 2
        if i % 2 == 0:
            q, k, v, mq, g = _swa_proj(hp, swa_w_in[j])
            mix, kw, vw = _swa_prompt(q, k, v, swa_sinks[j], rel_bias)
            hp = _finish(hp, mix, mq, g, mk_p, mv_p, w_out[i], ln_w[i], ln_b[i])
            swa_kp.append(kw)
            swa_vp.append(vw)
            q, k, v, mq, g = _swa_proj(hs, swa_w_in[j])
            mix, kw, vw = _swa_sample(q, k, v, cache_swa_k[j], cache_swa_v[j], swa_sinks[j], rel_bias)
            hs = _finish(hs, mix, mq, g, cache_mem_k[i], cache_mem_v[i], w_out[i], ln_w[i], ln_b[i])
            swa_ks.append(kw.astype(cache_swa_k.dtype))
            swa_vs.append(vw.astype(cache_swa_v.dtype))
        else:
            s0 = jnp.zeros((bp, HG_HEADS, HG_DK, HG_DV), jnp.float32)
            mix, s_new, mq, g = _hgrn_branch(hp, hg_w_in[j], lb_all[i], hg_norm_w[j], s0)
            hp = _finish(hp, mix, mq, g, mk_p, mv_p, w_out[i], ln_w[i], ln_b[i])
            hg_sp.append(s_new.astype(x_prompt.dtype))
            mix, s_new, mq, g = _hgrn_branch(hs, hg_w_in[j], lb_all[i], hg_norm_w[j], state_hgrn[j])
            hs = _finish(hs, mix, mq, g, cache_mem_k[i], cache_mem_v[i], w_out[i], ln_w[i], ln_b[i])
            hg_ss.append(s_new.astype(state_hgrn.dtype))
    mem_k_prompt = jnp.stack(mk_list)
    mem_v_prompt = jnp.stack(mv_list)
    swa_k_prompt = jnp.stack(swa_kp)
    swa_v_prompt = jnp.stack(swa_vp)
    hgrn_state_prompt = jnp.stack(hg_sp)
    swa_k_sample = jnp.stack(swa_ks)
    swa_v_sample = jnp.stack(swa_vs)
    hgrn_state_sample = jnp.stack(hg_ss)
    return (hp, hs, mem_k_prompt, mem_v_prompt, swa_k_prompt, swa_v_prompt, hgrn_state_prompt,
            swa_k_sample, swa_v_sample, hgrn_state_sample)
```

```python
import functools
import math

import numpy as np
import jax
import jax.numpy as jnp
from jax import lax
from jax.experimental import pallas as pl
from jax.experimental.pallas import tpu as pltpu

F32 = jnp.float32
BF16 = jnp.bfloat16
HIGHEST = lax.Precision.HIGHEST

D_MODEL = 2048
DEPTH = 2
PAST_LEN = 8192
N_MEM = 256
MEM_HEADS = 4
MEM_HD = 128
MEM_Q = MEM_HEADS * MEM_HD
WINDOW = 128
SWA_HD = 64
SWA_KVH = 4
SWA_GROUP = 6
SWA_QH = SWA_KVH * SWA_GROUP
SWA_Q = SWA_QH * SWA_HD
SWA_KV = SWA_KVH * SWA_HD
N_BUCKETS = 32
HG_HEADS = 12
HG_D = 128
HG_K = HG_HEADS * HG_D
SWA_WIDTH = SWA_Q + 2 * SWA_KV + MEM_Q + D_MODEL
HG_WIDTH = 3 * HG_K + MEM_Q + D_MODEL
DEEPNORM_ALPHA = (2.0 * DEPTH) ** 0.25
LN_EPS = 1e-5
RMS_EPS = 1e-6
MASKED = -1e30

V7X_SUBLANES = 8
V7X_LANES = 128
V7X_VMEM_LIMIT = 60000 * 1024


def _t5_bucket_ranges():
    n = np.arange(WINDOW)
    max_exact = N_BUCKETS // 2
    nf = np.maximum(n, 1).astype(np.float64)
    large = max_exact + (np.log(nf / max_exact) / math.log(WINDOW / max_exact)
                         * (N_BUCKETS - max_exact)).astype(np.int64)
    bucket = np.where(n < max_exact, n, np.minimum(large, N_BUCKETS - 1))
    ranges = []
    for b in range(N_BUCKETS):
        idx = np.nonzero(bucket == b)[0]
        ranges.append((int(idx[0]), int(idx[-1])) if idx.size else None)
    return ranges


_BUCKET_RANGES = _t5_bucket_ranges()


def _dot(a, b):
    return jnp.dot(a.astype(BF16), b.astype(BF16), preferred_element_type=F32)


def _dot_nt(a, b):
    return lax.dot_general(a.astype(BF16), b.astype(BF16), (((1,), (1,)), ((), ())),
                           preferred_element_type=F32)


def _layer_norm(z, w, b):
    mu = jnp.mean(z, axis=-1, keepdims=True)
    zc = z - mu
    var = jnp.mean(zc * zc, axis=-1, keepdims=True)
    return zc * lax.rsqrt(var + LN_EPS) * w + b


def _silu(x):
    return x * jax.nn.sigmoid(x)


def _cparams(n_axes):
    return pltpu.CompilerParams(dimension_semantics=("arbitrary",) * n_axes,
                                vmem_limit_bytes=V7X_VMEM_LIMIT)


def _resident(shape):
    nd = len(shape)
    return pl.BlockSpec(shape, lambda *_: (0,) * nd, pipeline_mode=pl.Buffered(1))


_SMEM = pl.BlockSpec(memory_space=pltpu.SMEM)


def _proj_kernel(x_ref, w_ref, *o_refs):
    acc = jnp.dot(x_ref[...].astype(BF16), w_ref[...], preferred_element_type=F32)
    o_refs[0][...] = acc
    if len(o_refs) > 1:
        o_refs[1][...] = acc.astype(BF16)


def _proj(x, w, bm, bn, with_bf16=False, name="proj"):
    m, k = x.shape
    n = w.shape[1]
    out_shape = [jax.ShapeDtypeStruct((m, n), F32)]
    out_specs = [pl.BlockSpec((bm, bn), lambda i, j: (i, j))]
    if with_bf16:
        out_shape.append(jax.ShapeDtypeStruct((m, n), BF16))
        out_specs.append(pl.BlockSpec((bm, bn), lambda i, j: (i, j)))
    return pl.pallas_call(
        _proj_kernel,
        grid=(m // bm, n // bn),
        in_specs=[pl.BlockSpec((bm, k), lambda i, j: (i, 0)),
                  pl.BlockSpec((k, bn), lambda i, j: (0, j))],
        out_specs=out_specs,
        out_shape=out_shape,
        compiler_params=_cparams(2),
        name=name,
    )(x, w)


def _mem_attend_rows(u_scr, col0, mk_ref, mv_ref, br_scr, out_col0):
    for hm in range(MEM_HEADS):
        lo = hm * MEM_HD
        mq = u_scr[:, col0 + lo:col0 + lo + MEM_HD]
        s = _dot_nt(mq, mk_ref[:, lo:lo + MEM_HD]) * (MEM_HD ** -0.5)
        m = jnp.max(s, axis=-1, keepdims=True)
        p = jnp.exp(s - m)
        l = jnp.sum(p, axis=-1, keepdims=True)
        o = _dot(p, mv_ref[:, lo:lo + MEM_HD]) / l
        br_scr[:, out_col0 + lo:out_col0 + lo + MEM_HD] = o


def _gate_project_norm(x, u_scr, gate_col0, br_scr, wout_ref, lnw_ref, lnb_ref):
    g = u_scr[:, gate_col0:gate_col0 + D_MODEL]
    branch = br_scr[...] * _silu(g)
    y = jnp.dot(branch.astype(BF16), wout_ref[...], preferred_element_type=F32)
    return _layer_norm(DEEPNORM_ALPHA * x + y, lnw_ref[...], lnb_ref[...])


def _swa_prompt_kernel(relb_ref, sink_ref, h_ref, win_ref, wout_ref, mk_ref, mv_ref, lnw_ref, lnb_ref,
                       out_ref, kout_ref, vout_ref,
                       bias_scr, k_scr, v_scr, u_scr, br_scr, *, rows):
    b = pl.program_id(0)
    t = pl.program_id(1)
    w2 = 2 * WINDOW

    @pl.when(jnp.logical_and(b == 0, t == 0))
    def _build_bias():
        qi = lax.broadcasted_iota(jnp.int32, (WINDOW, w2), 0)
        kj = lax.broadcasted_iota(jnp.int32, (WINDOW, w2), 1)
        dist = qi + WINDOW - kj

        def per_head(hq, carry):
            acc = jnp.full((WINDOW, w2), MASKED, F32)
            for bk, rng in enumerate(_BUCKET_RANGES):
                if rng is not None:
                    hit = jnp.logical_and(dist >= rng[0], dist <= rng[1])
                    acc = jnp.where(hit, relb_ref[bk, hq], acc)
            bias_scr[0, hq] = acc
            bias_scr[1, hq] = jnp.where(kj < WINDOW, MASKED, acc)
            return carry

        lax.fori_loop(0, SWA_QH, per_head, 0)

    @pl.when(t == 0)
    def _reset_window():
        k_scr[0:WINDOW, :] = jnp.zeros((WINDOW, SWA_KV), BF16)
        v_scr[0:WINDOW, :] = jnp.zeros((WINDOW, SWA_KV), BF16)

    x = h_ref[0]
    u_scr[...] = jnp.dot(x.astype(BF16), win_ref[...], preferred_element_type=F32)
    k = u_scr[:, SWA_Q:SWA_Q + SWA_KV]
    v = u_scr[:, SWA_Q + SWA_KV:SWA_Q + 2 * SWA_KV]
    kout_ref[0] = k[rows - WINDOW:, :]
    vout_ref[0] = v[rows - WINDOW:, :]
    k_scr[WINDOW:WINDOW + rows, :] = k.astype(BF16)
    v_scr[WINDOW:WINDOW + rows, :] = v.astype(BF16)

    for qb in range(rows // WINDOW):
        r0 = qb * WINDOW
        table = jnp.where(t == 0, 1, 0) if qb == 0 else 0
        for g in range(SWA_KVH):
            kwin = k_scr[r0:r0 + w2, g * SWA_HD:(g + 1) * SWA_HD]
            vwin = v_scr[r0:r0 + w2, g * SWA_HD:(g + 1) * SWA_HD]
            for j in range(SWA_GROUP):
                hq = g * SWA_GROUP + j
                q = u_scr[r0:r0 + WINDOW, hq * SWA_HD:(hq + 1) * SWA_HD] * (SWA_HD ** -0.5)
                s = _dot_nt(q, kwin) + bias_scr[table, hq]
                sink = sink_ref[hq]
                m = jnp.maximum(jnp.max(s, axis=-1, keepdims=True), sink)
                p = jnp.exp(s - m)
                l = jnp.sum(p, axis=-1, keepdims=True) + jnp.exp(sink - m)
                o = _dot(p, vwin) / l
                br_scr[r0:r0 + WINDOW, hq * SWA_HD:(hq + 1) * SWA_HD] = o

    k_scr[0:WINDOW, :] = k_scr[rows:rows + WINDOW, :]
    v_scr[0:WINDOW, :] = v_scr[rows:rows + WINDOW, :]

    _mem_attend_rows(u_scr, SWA_Q + 2 * SWA_KV, mk_ref, mv_ref, br_scr, SWA_Q)
    out_ref[0] = _gate_project_norm(x, u_scr, SWA_Q + 2 * SWA_KV + MEM_Q, br_scr, wout_ref, lnw_ref, lnb_ref)


def _swa_prompt_layer(h, w_in, w_out, kv_bf, layer, rel_bias, sinks, ln_w, ln_b, rows):
    bsz, t, d = h.shape
    kernel = functools.partial(_swa_prompt_kernel, rows=rows)
    return pl.pallas_call(
        kernel,
        grid=(bsz, t // rows),
        in_specs=[_SMEM, _SMEM,
                  pl.BlockSpec((1, rows, d), lambda b, i: (b, i, 0)),
                  _resident((d, SWA_WIDTH)),
                  _resident((d, d)),
                  pl.BlockSpec((N_MEM, MEM_Q), lambda b, i: (b, layer)),
                  pl.BlockSpec((N_MEM, MEM_Q), lambda b, i: (b, DEPTH + layer)),
                  _resident((1, d)), _resident((1, d))],
        out_specs=[pl.BlockSpec((1, rows, d), lambda b, i: (b, i, 0)),
                   pl.BlockSpec((1, WINDOW, SWA_KV), lambda b, i: (b, 0, 0)),
                   pl.BlockSpec((1, WINDOW, SWA_KV), lambda b, i: (b, 0, 0))],
        out_shape=[jax.ShapeDtypeStruct((bsz, t, d), F32),
                   jax.ShapeDtypeStruct((bsz, WINDOW, SWA_KV), F32),
                   jax.ShapeDtypeStruct((bsz, WINDOW, SWA_KV), F32)],
        scratch_shapes=[pltpu.VMEM((2, SWA_QH, WINDOW, 2 * WINDOW), F32),
                        pltpu.VMEM((WINDOW + rows, SWA_KV), BF16),
                        pltpu.VMEM((WINDOW + rows, SWA_KV), BF16),
                        pltpu.VMEM((rows, SWA_WIDTH), F32),
                        pltpu.VMEM((rows, d), F32)],
        compiler_params=_cparams(2),
        name="swa_prompt_layer",
    )(rel_bias, sinks, h, w_in, w_out, kv_bf, kv_bf, ln_w.reshape(1, d), ln_b.reshape(1, d))


HG_CHUNK = 128
HG_DIRECT = 8


def _lower_bound(logits, layer):
    e = jnp.exp(logits - jnp.max(logits, axis=0, keepdims=True))
    sm = e / jnp.sum(e, axis=0, keepdims=True)
    return jnp.sum(sm[1:layer + 1, :], axis=0, keepdims=True)


def _hgrn_chunk_scores_t(qs, kk, gc):
    c = HG_CHUNK
    nb = c // HG_DIRECT
    q3 = qs.reshape(nb, HG_DIRECT, HG_D)
    k3 = kk.reshape(nb, HG_DIRECT, HG_D)
    g3 = gc.reshape(nb, HG_DIRECT, HG_D)
    lane = lax.broadcasted_iota(jnp.int32, (nb, HG_DIRECT, c), 2)
    blk = lax.broadcasted_iota(jnp.int32, (nb, HG_DIRECT, c), 0)
    srow = lax.broadcasted_iota(jnp.int32, (nb, HG_DIRECT, c), 1)
    at3 = jnp.zeros((nb, HG_DIRECT, c), F32)
    for tl in range(HG_DIRECT):
        e = jnp.exp(jnp.minimum(g3[:, tl:tl + 1, :] - g3, 0.0))
        col = jnp.sum(e * k3 * q3[:, tl:tl + 1, :], axis=-1, keepdims=True)
        hit = jnp.logical_and(lane == blk * HG_DIRECT + tl, srow <= tl)
        at3 = jnp.where(hit, col, at3)
    at = at3.reshape(c, c)

    si = lax.broadcasted_iota(jnp.int32, (c, c), 0)
    ti = lax.broadcasted_iota(jnp.int32, (c, c), 1)
    ri = lax.broadcasted_iota(jnp.int32, (c, 1), 0)
    half = HG_DIRECT
    while half < c:
        span = 2 * half
        gsp = gc.reshape(c // span, span, HG_D)
        gref = jnp.broadcast_to(gsp[:, half - 1:half, :], gsp.shape).reshape(c, HG_D)
        upper = (ri % span) >= half
        eq = jnp.where(upper, jnp.exp(jnp.minimum(gc - gref, 0.0)), 0.0)
        ek = jnp.where(upper, 0.0, jnp.exp(jnp.minimum(gref - gc, 0.0)))
        st = _dot_nt(kk * ek, qs * eq)
        at = at + jnp.where((si // span) == (ti // span), st, 0.0)
        half = span
    return at


def _hgrn_prompt_kernel(h_ref, win_ref, wout_ref, lbl_ref, nw_ref, mk_ref, mv_ref, lnw_ref, lnb_ref,
                        out_ref, st_ref,
                        u_scr, br_scr, lb_scr, *, layer):
    t = pl.program_id(1)
    c = HG_CHUNK

    @pl.when(t == 0)
    def _reset_state():
        st_ref[...] = jnp.zeros(st_ref.shape, F32)

    lb_scr[...] = jnp.broadcast_to(_lower_bound(lbl_ref[...], layer), lb_scr.shape)
    x = h_ref[0]
    u_scr[...] = jnp.dot(x.astype(BF16), win_ref[...], preferred_element_type=F32)

    ri = lax.broadcasted_iota(jnp.int32, (c, c), 0)
    ci = lax.broadcasted_iota(jnp.int32, (c, c), 1)
    tril = jnp.where(ri >= ci, 1.0, 0.0).astype(F32)

    def per_head(hh, carry):
        c0 = pl.multiple_of(hh * HG_D, HG_D)
        qraw = u_scr[:, pl.ds(c0, HG_D)]
        f = u_scr[:, pl.ds(HG_K + c0, HG_D)]
        iv = u_scr[:, pl.ds(2 * HG_K + c0, HG_D)]
        lb = lb_scr[0:1, pl.ds(c0, HG_D)]
        fg = lb + (1.0 - lb) * jax.nn.sigmoid(f)
        kk = 1.0 - fg
        g = jnp.log(fg)
        qs = _silu(qraw)
        gc = jnp.dot(tril, g, precision=HIGHEST, preferred_element_type=F32)
        at = _hgrn_chunk_scores_t(qs, kk, gc)
        state = st_ref[0, hh]
        o = _dot(at.T, iv) + _dot(qs * jnp.exp(gc), state)
        g_last = gc[c - 1:c, :]
        kdec = kk * jnp.exp(g_last - gc)
        decay_col = jnp.exp(jnp.sum(g.T, axis=1, keepdims=True))
        st_ref[0, hh] = decay_col * state + _dot(kdec.T, iv)
        ms = jnp.mean(o * o, axis=-1, keepdims=True)
        br_scr[:, pl.ds(c0, HG_D)] = o * lax.rsqrt(ms + RMS_EPS) * nw_ref[0:1, pl.ds(c0, HG_D)]
        return carry

    lax.fori_loop(0, HG_HEADS, per_head, 0)

    _mem_attend_rows(u_scr, 3 * HG_K, mk_ref, mv_ref, br_scr, HG_K)
    out_ref[0] = _gate_project_norm(x, u_scr, 3 * HG_K + MEM_Q, br_scr, wout_ref, lnw_ref, lnb_ref)


def _hgrn_prompt_layer(h, w_in, w_out, kv_bf, layer, lb_logits, norm_w, ln_w, ln_b):
    bsz, t, d = h.shape
    rows = HG_CHUNK
    kernel = functools.partial(_hgrn_prompt_kernel, layer=layer)
    return pl.pallas_call(
        kernel,
        grid=(bsz, t // rows),
        in_specs=[pl.BlockSpec((1, rows, d), lambda b, i: (b, i, 0)),
                  _resident((d, HG_WIDTH)),
                  _resident((d, d)),
                  _resident((DEPTH, HG_K)),
                  _resident((1, HG_K)),
                  pl.BlockSpec((N_MEM, MEM_Q), lambda b, i: (b, layer)),
                  pl.BlockSpec((N_MEM, MEM_Q), lambda b, i: (b, DEPTH + layer)),
                  _resident((1, d)), _resident((1, d))],
        out_specs=[pl.BlockSpec((1, rows, d), lambda b, i: (b, i, 0)),
                   pl.BlockSpec((1, HG_HEADS, HG_D, HG_D), lambda b, i: (b, 0, 0, 0))],
        out_shape=[jax.ShapeDtypeStruct((bsz, t, d), F32),
                   jax.ShapeDtypeStruct((bsz, HG_HEADS, HG_D, HG_D), F32)],
        scratch_shapes=[pltpu.VMEM((rows, HG_WIDTH), F32),
                        pltpu.VMEM((rows, d), F32),
                        pltpu.VMEM((V7X_SUBLANES, HG_K), F32)],
        compiler_params=_cparams(2),
        name="hgrn_prompt_layer",
    )(h, w_in, w_out, lb_logits, norm_w.reshape(1, HG_K), kv_bf, kv_bf, ln_w.reshape(1, d), ln_b.reshape(1, d))


def _rows8(row):
    return jnp.broadcast_to(row, (V7X_SUBLANES, row.shape[-1]))


def _sample_mem_attend(urow, col0, cmk_ref, cmv_ref):
    pieces = []
    for hm in range(MEM_HEADS):
        lo = hm * MEM_HD
        mq = _rows8(urow[:, col0 + lo:col0 + lo + MEM_HD])
        s = _dot_nt(mq, cmk_ref[0, :, lo:lo + MEM_HD]) * (MEM_HD ** -0.5)
        m = jnp.max(s, axis=-1, keepdims=True)
        p = jnp.exp(s - m)
        l = jnp.sum(p, axis=-1, keepdims=True)
        o = _dot(p, cmv_ref[0, :, lo:lo + MEM_HD]) / l
        pieces.append(o[0:1, :])
    return jnp.concatenate(pieces, axis=1)


def _sample_swa_kernel(relb_ref, sink_ref, u_ref, ck_ref, cv_ref, cmk_ref, cmv_ref,
                       br_ref, ko_ref, vo_ref, bias_scr, sink_scr):
    nbuf = WINDOW

    @pl.when(pl.program_id(0) == 0)
    def _build_tables():
        dist = nbuf - 1 - lax.broadcasted_iota(jnp.int32, (V7X_SUBLANES, nbuf), 1)
        rid = lax.broadcasted_iota(jnp.int32, (V7X_SUBLANES, nbuf), 0)
        for g in range(SWA_KVH):
            acc = jnp.zeros((V7X_SUBLANES, nbuf), F32)
            sk = jnp.zeros((V7X_SUBLANES, nbuf), F32)
            for j in range(SWA_GROUP):
                hq = g * SWA_GROUP + j
                sk = jnp.where(rid == j, sink_ref[hq], sk)
                for bk, rng in enumerate(_BUCKET_RANGES):
                    if rng is not None:
                        hit = jnp.logical_and(rid == j, jnp.logical_and(dist >= rng[0], dist <= rng[1]))
                        acc = jnp.where(hit, relb_ref[bk, hq], acc)
            bias_scr[g] = acc
            sink_scr[g] = sk

    urow = u_ref[0]
    k_new = urow[:, SWA_Q:SWA_Q + SWA_KV]
    v_new = urow[:, SWA_Q + SWA_KV:SWA_Q + 2 * SWA_KV]
    kc = jnp.concatenate([ck_ref[0, 1:nbuf, :], k_new], axis=0)
    vc = jnp.concatenate([cv_ref[0, 1:nbuf, :], v_new], axis=0)
    ko_ref[0] = kc
    vo_ref[0] = vc

    pieces = []
    pad = jnp.zeros((V7X_SUBLANES - SWA_GROUP, SWA_HD), F32)
    for g in range(SWA_KVH):
        qg = jnp.concatenate(
            [urow[:, (g * SWA_GROUP + j) * SWA_HD:(g * SWA_GROUP + j + 1) * SWA_HD] for j in range(SWA_GROUP)]
            + [pad], axis=0) * (SWA_HD ** -0.5)
        s = _dot_nt(qg, kc[:, g * SWA_HD:(g + 1) * SWA_HD]) + bias_scr[g]
        sink = sink_scr[g][:, 0:1]
        m = jnp.maximum(jnp.max(s, axis=-1, keepdims=True), sink)
        p = jnp.exp(s - m)
        l = jnp.sum(p, axis=-1, keepdims=True) + jnp.exp(sink - m)
        o = _dot(p, vc[:, g * SWA_HD:(g + 1) * SWA_HD]) / l
        pieces.extend(o[j:j + 1, :] for j in range(SWA_GROUP))
    mix = jnp.concatenate(pieces, axis=1)
    mem_o = _sample_mem_attend(urow, SWA_Q + 2 * SWA_KV, cmk_ref, cmv_ref)
    g0 = SWA_Q + 2 * SWA_KV + MEM_Q
    br_ref[0] = jnp.concatenate([mix, mem_o], axis=1) * _silu(urow[:, g0:g0 + D_MODEL])


def _sample_swa_attend(u, ck, cv, cmk, cmv, rel_bias, sinks):
    ns = u.shape[0]
    nbuf = ck.shape[1]
    u3 = u.reshape(ns, 1, SWA_WIDTH)
    per_req = lambda shape: pl.BlockSpec((1,) + shape, lambda i: (i, 0, 0))
    br, ko, vo = pl.pallas_call(
        _sample_swa_kernel,
        grid=(ns,),
        in_specs=[_SMEM, _SMEM, per_req((1, SWA_WIDTH)), per_req((nbuf, SWA_KV)), per_req((nbuf, SWA_KV)),
                  per_req((N_MEM, MEM_Q)), per_req((N_MEM, MEM_Q))],
        out_specs=[per_req((1, D_MODEL)), per_req((nbuf, SWA_KV)), per_req((nbuf, SWA_KV))],
        out_shape=[jax.ShapeDtypeStruct((ns, 1, D_MODEL), F32),
                   jax.ShapeDtypeStruct((ns, nbuf, SWA_KV), F32),
                   jax.ShapeDtypeStruct((ns, nbuf, SWA_KV), F32)],
        scratch_shapes=[pltpu.VMEM((SWA_KVH, V7X_SUBLANES, nbuf), F32),
                        pltpu.VMEM((SWA_KVH, V7X_SUBLANES, nbuf), F32)],
        compiler_params=_cparams(1),
        name="sample_swa_attend",
    )(rel_bias, sinks, u3, ck, cv, cmk, cmv)
    return br.reshape(ns, D_MODEL), ko, vo


def _sample_hgrn_gate_kernel(u_ref, mix_ref, cmk_ref, cmv_ref, br_ref):
    urow = u_ref[0]
    mem_o = _sample_mem_attend(urow, 3 * HG_K, cmk_ref, cmv_ref)
    g0 = 3 * HG_K + MEM_Q
    br_ref[0] = jnp.concatenate([mix_ref[0], mem_o], axis=1) * _silu(urow[:, g0:g0 + D_MODEL])


def _sample_hgrn_gate(u, mix, cmk, cmv):
    ns = u.shape[0]
    per_req = lambda shape: pl.BlockSpec((1,) + shape, lambda i: (i, 0, 0))
    br = pl.pallas_call(
        _sample_hgrn_gate_kernel,
        grid=(ns,),
        in_specs=[per_req((1, HG_WIDTH)), per_req((1, HG_K)), per_req((N_MEM, MEM_Q)), per_req((N_MEM, MEM_Q))],
        out_specs=per_req((1, D_MODEL)),
        out_shape=jax.ShapeDtypeStruct((ns, 1, D_MODEL), F32),
        compiler_params=_cparams(1),
        name="sample_hgrn_gate",
    )(u.reshape(ns, 1, HG_WIDTH), mix.reshape(ns, 1, HG_K), cmk, cmv)
    return br.reshape(ns, D_MODEL)


def _hgrn_sample_kernel(q_ref, f_ref, iv_ref, lbl_ref, nw_ref, st_ref, mix_ref, so_ref, o_scr, *, layer, ns):
    lb = _lower_bound(lbl_ref[...], layer)
    fg = lb + (1.0 - lb) * jax.nn.sigmoid(f_ref[...])
    kk_t = (1.0 - fg).T
    fg_t = fg.T
    qs_t = _silu(q_ref[...]).T
    for r in range(ns):
        state = fg_t[:, r:r + 1] * st_ref[r, 0] + kk_t[:, r:r + 1] * iv_ref[r:r + 1, :]
        so_ref[r, 0] = state
        o_scr[r:r + 1, :] = jnp.sum(qs_t[:, r:r + 1] * state, axis=0, keepdims=True)
    o = o_scr[...]
    ms = jnp.mean(o * o, axis=-1, keepdims=True)
    mix_ref[...] = o * lax.rsqrt(ms + RMS_EPS) * nw_ref[...]


def _hgrn_sample_step(u, state, lb_logits, norm_w, layer):
    ns = u.shape[0]
    col = lambda off: pl.BlockSpec((ns, HG_D), lambda h: (0, off + h))
    st_spec = pl.BlockSpec((ns, 1, HG_D, HG_D), lambda h: (0, h, 0, 0))
    kernel = functools.partial(_hgrn_sample_kernel, layer=layer, ns=ns)
    return pl.pallas_call(
        kernel,
        grid=(HG_HEADS,),
        in_specs=[col(0), col(HG_HEADS), col(2 * HG_HEADS),
                  pl.BlockSpec((DEPTH, HG_D), lambda h: (0, h)),
                  pl.BlockSpec((1, HG_D), lambda h: (0, h)),
                  st_spec],
        out_specs=[pl.BlockSpec((ns, HG_D), lambda h: (0, h)), st_spec],
        out_shape=[jax.ShapeDtypeStruct((ns, HG_K), F32),
                   jax.ShapeDtypeStruct(state.shape, F32)],
        scratch_shapes=[pltpu.VMEM((ns, HG_D), F32)],
        compiler_params=_cparams(1),
        name="hgrn_sample_step",
    )(u, u, u, lb_logits, norm_w.reshape(1, HG_K), state)


def _outproj_ln_kernel(br_ref, h_ref, wout_ref, lnw_ref, lnb_ref, out_ref):
    y = jnp.dot(br_ref[...].astype(BF16), wout_ref[...], preferred_element_type=F32)
    out_ref[...] = _layer_norm(DEEPNORM_ALPHA * h_ref[...] + y, lnw_ref[...], lnb_ref[...])


def _outproj_ln(branch, h, w_out, ln_w, ln_b):
    ns, d = h.shape
    full = lambda shape: pl.BlockSpec(shape, lambda i: (0, 0))
    return pl.pallas_call(
        _outproj_ln_kernel,
        grid=(1,),
        in_specs=[full((ns, d)), full((ns, d)), full((d, d)), full((1, d)), full((1, d))],
        out_specs=full((ns, d)),
        out_shape=jax.ShapeDtypeStruct((ns, d), F32),
        compiler_params=_cparams(1),
        name="outproj_ln",
    )(branch, h, w_out, ln_w.reshape(1, d), ln_b.reshape(1, d))


def kernel(x_prompt, x_sample, cache_mem_k, cache_mem_v, cache_swa_k, cache_swa_v, state_hgrn, mem_prompt, rel_bias, swa_w_in, swa_sinks, hg_w_in, hg_lb_logits, hg_norm_w, w_mem_k, w_mem_v, w_out, ln_w, ln_b):
    bp, t, d = x_prompt.shape
    ns = x_sample.shape[0]
    assert d == D_MODEL and x_sample.shape[1] == 1 and t % HG_CHUNK == 0 and t >= WINDOW
    assert cache_swa_k.shape[2] == WINDOW <= PAST_LEN
    assert w_mem_k.shape[0] == DEPTH

    swa_w = swa_w_in.astype(BF16)
    hg_w = hg_w_in.astype(BF16)
    wo = w_out.astype(BF16)
    w_kv = jnp.concatenate([w_mem_k[i] for i in range(DEPTH)] + [w_mem_v[i] for i in range(DEPTH)],
                           axis=1).astype(BF16)

    kv, kv_bf = _proj(mem_prompt.reshape(bp * N_MEM, d), w_kv, 512, MEM_Q, with_bf16=True, name="mem_kv_proj")
    kv = kv.reshape(bp, N_MEM, 2 * DEPTH, MEM_HEADS, MEM_HD)
    mem_k_prompt = jnp.moveaxis(kv[:, :, :DEPTH], 2, 0)
    mem_v_prompt = jnp.moveaxis(kv[:, :, DEPTH:], 2, 0)

    cmk = cache_mem_k.reshape(DEPTH, ns, N_MEM, MEM_Q)
    cmv = cache_mem_v.reshape(DEPTH, ns, N_MEM, MEM_Q)

    hp = x_prompt
    hs = x_sample.reshape(ns, d)
    swa_kp, swa_vp, swa_ks, swa_vs, hg_sp, hg_ss = [], [], [], [], [], []
    for i in range(DEPTH):
        j = i // 2
        if i % 2 == 0:
            hp, kw, vw = _swa_prompt_layer(hp, swa_w[j], wo[i], kv_bf, i, rel_bias, swa_sinks[j],
                                           ln_w[i], ln_b[i], rows=WINDOW)
            swa_kp.append(kw.reshape(bp, WINDOW, SWA_KVH, SWA_HD))
            swa_vp.append(vw.reshape(bp, WINDOW, SWA_KVH, SWA_HD))
            u = _proj(hs, swa_w[j], ns, 512, name="sample_swa_proj")[0]
            br, ko, vo = _sample_swa_attend(u, cache_swa_k[j].reshape(ns, WINDOW, SWA_KV),
                                            cache_swa_v[j].reshape(ns, WINDOW, SWA_KV),
                                            cmk[i], cmv[i], rel_bias, swa_sinks[j])
            swa_ks.append(ko.reshape(ns, WINDOW, SWA_KVH, SWA_HD))
            swa_vs.append(vo.reshape(ns, WINDOW, SWA_KVH, SWA_HD))
        else:
            hp, st = _hgrn_prompt_layer(hp, hg_w[j], wo[i], kv_bf, i, hg_lb_logits, hg_norm_w[j],
                                        ln_w[i], ln_b[i])
            hg_sp.append(st)
            u = _proj(hs, hg_w[j], ns, 512, name="sample_hgrn_proj")[0]
            mix, so = _hgrn_sample_step(u, state_hgrn[j], hg_lb_logits, hg_norm_w[j], i)
            hg_ss.append(so)
            br = _sample_hgrn_gate(u, mix, cmk[i], cmv[i])
        hs = _outproj_ln(br, hs, wo[i], ln_w[i], ln_b[i])

    return (hp, hs.reshape(ns, 1, d), mem_k_prompt, mem_v_prompt,
            jnp.stack(swa_kp), jnp.stack(swa_vp), jnp.stack(hg_sp),
            jnp.stack(swa_ks), jnp.stack(swa_vs), jnp.stack(hg_ss))
```

```python
import functools
import math

import numpy as np
import jax
import jax.numpy as jnp
from jax import lax
from jax.experimental import pallas as pl
from jax.experimental.pallas import tpu as pltpu

F32 = jnp.float32
BF16 = jnp.bfloat16

D_MODEL = 2048
DEPTH = 2
PAST_LEN = 8192
N_MEM = 256
MEM_HEADS = 4
MEM_HD = 128
MEM_Q = MEM_HEADS * MEM_HD
WINDOW = 128
SWA_HD = 64
SWA_KVH = 4
SWA_GROUP = 6
SWA_QH = SWA_KVH * SWA_GROUP
SWA_Q = SWA_QH * SWA_HD
SWA_KV = SWA_KVH * SWA_HD
N_BUCKETS = 32
HG_HEADS = 12
HG_D = 128
HG_K = HG_HEADS * HG_D
SWA_WIDTH = SWA_Q + 2 * SWA_KV + MEM_Q + D_MODEL
HG_WIDTH = 3 * HG_K + MEM_Q + D_MODEL
DEEPNORM_ALPHA = (2.0 * DEPTH) ** 0.25
LN_EPS = 1e-5
RMS_EPS = 1e-6
MASKED = -1e30

V7X_SUBLANES = 8
V7X_LANES = 128
V7X_VMEM_LIMIT = 60000 * 1024
PROJ_COLS = 512


def _t5_bucket_ranges():
    n = np.arange(WINDOW)
    max_exact = N_BUCKETS // 2
    nf = np.maximum(n, 1).astype(np.float64)
    large = max_exact + (np.log(nf / max_exact) / math.log(WINDOW / max_exact)
                         * (N_BUCKETS - max_exact)).astype(np.int64)
    bucket = np.where(n < max_exact, n, np.minimum(large, N_BUCKETS - 1))
    ranges = []
    for b in range(N_BUCKETS):
        idx = np.nonzero(bucket == b)[0]
        ranges.append((int(idx[0]), int(idx[-1])) if idx.size else None)
    return ranges


_BUCKET_RANGES = _t5_bucket_ranges()


def _dot(a, b):
    return jnp.dot(a.astype(BF16), b.astype(BF16), preferred_element_type=F32)


def _dot_nt(a, b):
    return lax.dot_general(a.astype(BF16), b.astype(BF16), (((1,), (1,)), ((), ())),
                           preferred_element_type=F32)


def _layer_norm(z, w, b):
    mu = jnp.mean(z, axis=-1, keepdims=True)
    zc = z - mu
    var = jnp.mean(zc * zc, axis=-1, keepdims=True)
    return zc * lax.rsqrt(var + LN_EPS) * w + b


def _silu(x):
    return x * jax.nn.sigmoid(x)


def _cparams(n_axes):
    return pltpu.CompilerParams(dimension_semantics=("arbitrary",) * n_axes,
                                vmem_limit_bytes=V7X_VMEM_LIMIT)


def _resident(shape):
    nd = len(shape)
    return pl.BlockSpec(shape, lambda *_: (0,) * nd, pipeline_mode=pl.Buffered(1))


_SMEM = pl.BlockSpec(memory_space=pltpu.SMEM)


def _mem_kv_kernel(x_ref, wk_ref, wv_ref, ko_ref, vo_ref, kb_ref, vb_ref):
    xb = x_ref[...].astype(BF16)
    for w_ref, o_ref, b_ref in ((wk_ref, ko_ref, kb_ref), (wv_ref, vo_ref, vb_ref)):
        y = jnp.dot(xb, w_ref[...].astype(BF16), preferred_element_type=F32)
        b_ref[...] = y.astype(BF16)
        for hm in range(MEM_HEADS):
            o_ref[:, hm, :] = y[:, hm * MEM_HD:(hm + 1) * MEM_HD]


def _mem_kv_proj(mem, w_k, w_v):
    bsz, n, d = mem.shape
    nl = w_k.shape[0]
    w_spec = pl.BlockSpec((None, d, MEM_Q), lambda l, b: (l, 0, 0))
    o_spec = pl.BlockSpec((None, None, n, MEM_HEADS, MEM_HD), lambda l, b: (l, b, 0, 0, 0))
    b_spec = pl.BlockSpec((None, n, MEM_Q), lambda l, b: (l, b, 0))
    o_shape = jax.ShapeDtypeStruct((nl, bsz, n, MEM_HEADS, MEM_HD), F32)
    b_shape = jax.ShapeDtypeStruct((nl, bsz * n, MEM_Q), BF16)
    return pl.pallas_call(
        _mem_kv_kernel,
        grid=(nl, bsz),
        in_specs=[pl.BlockSpec((None, n, d), lambda l, b: (b, 0, 0)), w_spec, w_spec],
        out_specs=[o_spec, o_spec, b_spec, b_spec],
        out_shape=[o_shape, o_shape, b_shape, b_shape],
        compiler_params=_cparams(2),
        name="mem_kv_proj",
    )(mem, w_k, w_v)


def _proj_kernel(x_ref, w_ref, o_ref):
    o_ref[...] = jnp.dot(x_ref[...].astype(BF16), w_ref[...], preferred_element_type=F32)


def _proj(x, w, name):
    m, k = x.shape
    n = w.shape[1]
    return pl.pallas_call(
        _proj_kernel,
        grid=(n // PROJ_COLS,),
        in_specs=[pl.BlockSpec((m, k), lambda j: (0, 0)),
                  pl.BlockSpec((k, PROJ_COLS), lambda j: (0, j))],
        out_specs=pl.BlockSpec((m, PROJ_COLS), lambda j: (0, j)),
        out_shape=jax.ShapeDtypeStruct((m, n), F32),
        compiler_params=_cparams(1),
        name=name,
    )(x, w)


def _mem_attend_rows(mq_all, mk_ref, mv_ref, br_scr, out_col0):
    for hm in range(MEM_HEADS):
        lo = hm * MEM_HD
        s = _dot_nt(mq_all[:, lo:lo + MEM_HD], mk_ref[:, lo:lo + MEM_HD]) * (MEM_HD ** -0.5)
        m = jnp.max(s, axis=-1, keepdims=True)
        p = jnp.exp(s - m)
        l = jnp.sum(p, axis=-1, keepdims=True)
        o = _dot(p, mv_ref[:, lo:lo + MEM_HD]) / l
        br_scr[:, out_col0 + lo:out_col0 + lo + MEM_HD] = o


def _gate_project_norm(x, gate, br_scr, wout_ref, lnw_ref, lnb_ref):
    branch = br_scr[...] * _silu(gate)
    y = jnp.dot(branch.astype(BF16), wout_ref[...], preferred_element_type=F32)
    return _layer_norm(DEEPNORM_ALPHA * x + y, lnw_ref[...], lnb_ref[...])


def _swa_prompt_kernel(relb_ref, sink_ref, h_ref, win_ref, wout_ref, mk_ref, mv_ref, lnw_ref, lnb_ref,
                       out_ref, kout_ref, vout_ref,
                       bias_scr, k_scr, v_scr, u_scr, br_scr, *, rows):
    b = pl.program_id(0)
    t = pl.program_id(1)
    w2 = 2 * WINDOW

    @pl.when(jnp.logical_and(b == 0, t == 0))
    def _build_bias():
        qi = lax.broadcasted_iota(jnp.int32, (WINDOW, w2), 0)
        kj = lax.broadcasted_iota(jnp.int32, (WINDOW, w2), 1)
        dist = qi + WINDOW - kj

        def per_head(hq, carry):
            acc = jnp.full((WINDOW, w2), MASKED, F32)
            for bk, rng in enumerate(_BUCKET_RANGES):
                if rng is not None:
                    hit = jnp.logical_and(dist >= rng[0], dist <= rng[1])
                    acc = jnp.where(hit, relb_ref[bk, hq], acc)
            bias_scr[0, hq] = acc
            bias_scr[1, hq] = jnp.where(kj < WINDOW, MASKED, acc)
            return carry

        lax.fori_loop(0, SWA_QH, per_head, 0)

    @pl.when(t == 0)
    def _reset_window():
        k_scr[0:WINDOW, :] = jnp.zeros((WINDOW, SWA_KV), BF16)
        v_scr[0:WINDOW, :] = jnp.zeros((WINDOW, SWA_KV), BF16)

    x = h_ref[0]
    u_scr[...] = jnp.dot(x.astype(BF16), win_ref[...], preferred_element_type=F32)
    k = u_scr[:, SWA_Q:SWA_Q + SWA_KV]
    v = u_scr[:, SWA_Q + SWA_KV:SWA_Q + 2 * SWA_KV]
    kout_ref[0] = k[rows - WINDOW:, :]
    vout_ref[0] = v[rows - WINDOW:, :]
    k_scr[WINDOW:WINDOW + rows, :] = k.astype(BF16)
    v_scr[WINDOW:WINDOW + rows, :] = v.astype(BF16)

    for qb in range(rows // WINDOW):
        r0 = qb * WINDOW
        table = jnp.where(t == 0, 1, 0) if qb == 0 else 0
        for g in range(SWA_KVH):
            kwin = k_scr[r0:r0 + w2, g * SWA_HD:(g + 1) * SWA_HD]
            vwin = v_scr[r0:r0 + w2, g * SWA_HD:(g + 1) * SWA_HD]
            for j in range(SWA_GROUP):
                hq = g * SWA_GROUP + j
                q = u_scr[r0:r0 + WINDOW, hq * SWA_HD:(hq + 1) * SWA_HD] * (SWA_HD ** -0.5)
                s = _dot_nt(q, kwin) + bias_scr[table, hq]
                sink = sink_ref[hq]
                m = jnp.maximum(jnp.max(s, axis=-1, keepdims=True), sink)
                p = jnp.exp(s - m)
                l = jnp.sum(p, axis=-1, keepdims=True) + jnp.exp(sink - m)
                o = _dot(p, vwin) / l
                br_scr[r0:r0 + WINDOW, hq * SWA_HD:(hq + 1) * SWA_HD] = o

    k_scr[0:WINDOW, :] = k_scr[rows:rows + WINDOW, :]
    v_scr[0:WINDOW, :] = v_scr[rows:rows + WINDOW, :]

    mq0 = SWA_Q + 2 * SWA_KV
    _mem_attend_rows(u_scr[:, mq0:mq0 + MEM_Q], mk_ref, mv_ref, br_scr, SWA_Q)
    gate = u_scr[:, mq0 + MEM_Q:mq0 + MEM_Q + D_MODEL]
    out_ref[0] = _gate_project_norm(x, gate, br_scr, wout_ref, lnw_ref, lnb_ref)


def _swa_prompt_layer(h, w_in, w_out, mk_bf, mv_bf, layer, rel_bias, sinks, ln_w, ln_b, rows):
    bsz, t, d = h.shape
    kernel = functools.partial(_swa_prompt_kernel, rows=rows)
    mem_spec = pl.BlockSpec((None, N_MEM, MEM_Q), lambda b, i: (layer, b, 0))
    return pl.pallas_call(
        kernel,
        grid=(bsz, t // rows),
        in_specs=[_SMEM, _SMEM,
                  pl.BlockSpec((1, rows, d), lambda b, i: (b, i, 0)),
                  _resident((d, SWA_WIDTH)),
                  _resident((d, d)),
                  mem_spec, mem_spec,
                  _resident((1, d)), _resident((1, d))],
        out_specs=[pl.BlockSpec((1, rows, d), lambda b, i: (b, i, 0)),
                   pl.BlockSpec((1, WINDOW, SWA_KV), lambda b, i: (b, 0, 0)),
                   pl.BlockSpec((1, WINDOW, SWA_KV), lambda b, i: (b, 0, 0))],
        out_shape=[jax.ShapeDtypeStruct((bsz, t, d), F32),
                   jax.ShapeDtypeStruct((bsz, WINDOW, SWA_KV), F32),
                   jax.ShapeDtypeStruct((bsz, WINDOW, SWA_KV), F32)],
        scratch_shapes=[pltpu.VMEM((2, SWA_QH, WINDOW, 2 * WINDOW), F32),
                        pltpu.VMEM((WINDOW + rows, SWA_KV), BF16),
                        pltpu.VMEM((WINDOW + rows, SWA_KV), BF16),
                        pltpu.VMEM((rows, SWA_WIDTH), F32),
                        pltpu.VMEM((rows, d), F32)],
        compiler_params=_cparams(2),
        name="swa_prompt_layer",
    )(rel_bias, sinks, h, w_in, w_out, mk_bf, mv_bf, ln_w.reshape(1, d), ln_b.reshape(1, d))


HG_CHUNK = 128
HG_DIRECT = 8
HG_HEAD_COLS = 3 * HG_D


def _hgrn_head_major(w):
    d = w.shape[0]
    qfi = w[:, :3 * HG_K].reshape(d, 3, HG_HEADS, HG_D).transpose(0, 2, 1, 3).reshape(d, 3 * HG_K)
    return jnp.concatenate([qfi, w[:, 3 * HG_K:]], axis=1)


def _lower_bound(logits, layer):
    e = jnp.exp(logits - jnp.max(logits, axis=0, keepdims=True))
    sm = e / jnp.sum(e, axis=0, keepdims=True)
    return jnp.sum(sm[1:layer + 1, :], axis=0, keepdims=True)


def _cumsum_rows(tril_bf, g):
    g1 = g.astype(BF16)
    r1 = g - g1.astype(F32)
    g2 = r1.astype(BF16)
    g3 = (r1 - g2.astype(F32)).astype(BF16)
    d = jnp.dot(tril_bf, jnp.concatenate([g1, g2, g3], axis=1), preferred_element_type=F32)
    return (d[:, :HG_D] + d[:, HG_D:2 * HG_D]) + d[:, 2 * HG_D:]


def _hgrn_chunk_scores(qs, kk, gc):
    c = HG_CHUNK
    nb = c // HG_DIRECT
    q3 = qs.reshape(nb, HG_DIRECT, HG_D)
    k3 = kk.reshape(nb, HG_DIRECT, HG_D)
    g3 = gc.reshape(nb, HG_DIRECT, HG_D)
    lane = lax.broadcasted_iota(jnp.int32, (nb, HG_DIRECT, c), 2)
    blk = lax.broadcasted_iota(jnp.int32, (nb, HG_DIRECT, c), 0)
    trow = lax.broadcasted_iota(jnp.int32, (nb, HG_DIRECT, c), 1)
    a3 = jnp.zeros((nb, HG_DIRECT, c), F32)
    for sl in range(HG_DIRECT):
        e = jnp.exp(g3 - g3[:, sl:sl + 1, :])
        col = jnp.sum(e * q3 * k3[:, sl:sl + 1, :], axis=-1, keepdims=True)
        hit = jnp.logical_and(lane == blk * HG_DIRECT + sl, trow >= sl)
        a3 = jnp.where(hit, col, a3)
    a = a3.reshape(c, c)

    ti = lax.broadcasted_iota(jnp.int32, (c, c), 0)
    si = lax.broadcasted_iota(jnp.int32, (c, c), 1)
    ri = lax.broadcasted_iota(jnp.int32, (c, 1), 0)
    half = HG_DIRECT
    while half < c:
        span = 2 * half
        gsp = gc.reshape(c // span, span, HG_D)
        gref = jnp.broadcast_to(gsp[:, half - 1:half, :], gsp.shape).reshape(c, HG_D)
        upper = (ri % span) >= half
        eq = jnp.where(upper, jnp.exp(gc - gref), 0.0)
        ek = jnp.where(upper, 0.0, jnp.exp(gref - gc))
        a = a + jnp.where((ti // span) == (si // span), _dot_nt(qs * eq, kk * ek), 0.0)
        half = span
    return a


def _hgrn_prompt_kernel(h_ref, win_ref, wout_ref, lbl_ref, nw_ref, mk_ref, mv_ref, lnw_ref, lnb_ref,
                        out_ref, st_ref,
                        st_scr, br_scr, *, layer):
    t = pl.program_id(1)
    c = HG_CHUNK

    @pl.when(t == 0)
    def _reset_state():
        st_scr[...] = jnp.zeros(st_scr.shape, F32)

    lb_all = _lower_bound(lbl_ref[...], layer)
    x = h_ref[0]
    xb = x.astype(BF16)
    ri = lax.broadcasted_iota(jnp.int32, (c, c), 0)
    ci = lax.broadcasted_iota(jnp.int32, (c, c), 1)
    tril_bf = jnp.where(ri >= ci, 1.0, 0.0).astype(BF16)

    for hh in range(HG_HEADS):
        c0 = hh * HG_D
        uh = jnp.dot(xb, win_ref[:, hh * HG_HEAD_COLS:(hh + 1) * HG_HEAD_COLS], preferred_element_type=F32)
        qs = _silu(uh[:, :HG_D])
        iv = uh[:, 2 * HG_D:]
        lb = lb_all[:, c0:c0 + HG_D]
        fg = lb + (1.0 - lb) * jax.nn.sigmoid(uh[:, HG_D:2 * HG_D])
        kk = 1.0 - fg
        gc = _cumsum_rows(tril_bf, jnp.log(fg))
        a = _hgrn_chunk_scores(qs, kk, gc)
        st_t = st_scr[hh]
        o = _dot(a, iv) + _dot_nt(qs * jnp.exp(gc), st_t)
        g_last = gc[c - 1:c, :]
        kdec = kk * jnp.exp(g_last - gc)
        st_scr[hh] = st_t * jnp.exp(g_last) + _dot(iv.T, kdec)
        ms = jnp.mean(o * o, axis=-1, keepdims=True)
        br_scr[:, c0:c0 + HG_D] = o * lax.rsqrt(ms + RMS_EPS) * nw_ref[:, c0:c0 + HG_D]

    @pl.when(t == pl.num_programs(1) - 1)
    def _emit_state():
        for hh in range(HG_HEADS):
            st_ref[0, hh] = st_scr[hh].T

    ug = jnp.dot(xb, win_ref[:, 3 * HG_K:], preferred_element_type=F32)
    _mem_attend_rows(ug[:, :MEM_Q], mk_ref, mv_ref, br_scr, HG_K)
    out_ref[0] = _gate_project_norm(x, ug[:, MEM_Q:], br_scr, wout_ref, lnw_ref, lnb_ref)


def _hgrn_prompt_layer(h, w_in_hm, w_out, mk_bf, mv_bf, layer, lb_logits, norm_w, ln_w, ln_b):
    bsz, t, d = h.shape
    rows = HG_CHUNK
    kernel = functools.partial(_hgrn_prompt_kernel, layer=layer)
    mem_spec = pl.BlockSpec((None, N_MEM, MEM_Q), lambda b, i: (layer, b, 0))
    return pl.pallas_call(
        kernel,
        grid=(bsz, t // rows),
        in_specs=[pl.BlockSpec((1, rows, d), lambda b, i: (b, i, 0)),
                  _resident((d, HG_WIDTH)),
                  _resident((d, d)),
                  _resident((DEPTH, HG_K)),
                  _resident((1, HG_K)),
                  mem_spec, mem_spec,
                  _resident((1, d)), _resident((1, d))],
        out_specs=[pl.BlockSpec((1, rows, d), lambda b, i: (b, i, 0)),
                   pl.BlockSpec((1, HG_HEADS, HG_D, HG_D), lambda b, i: (b, 0, 0, 0))],
        out_shape=[jax.ShapeDtypeStruct((bsz, t, d), F32),
                   jax.ShapeDtypeStruct((bsz, HG_HEADS, HG_D, HG_D), F32)],
        scratch_shapes=[pltpu.VMEM((HG_HEADS, HG_D, HG_D), F32),
                        pltpu.VMEM((rows, d), F32)],
        compiler_params=_cparams(2),
        name="hgrn_prompt_layer",
    )(h, w_in_hm, w_out, lb_logits, norm_w.reshape(1, HG_K), mk_bf, mv_bf, ln_w.reshape(1, d), ln_b.reshape(1, d))


def _rows8(row):
    return jnp.broadcast_to(row, (V7X_SUBLANES, row.shape[-1]))


def _sample_mem_attend(urow, col0, cmk_ref, cmv_ref):
    pieces = []
    for hm in range(MEM_HEADS):
        lo = col0 + hm * MEM_HD
        s = _dot_nt(_rows8(urow[:, lo:lo + MEM_HD]), cmk_ref[:, hm, :]) * (MEM_HD ** -0.5)
        m = jnp.max(s, axis=-1, keepdims=True)
        p = jnp.exp(s - m)
        l = jnp.sum(p, axis=-1, keepdims=True)
        o = _dot(p, cmv_ref[:, hm, :]) / l
        pieces.append(o[0:1, :])
    return jnp.concatenate(pieces, axis=1)


def _sample_swa_kernel(relb_ref, sink_ref, u_ref, ck_ref, cv_ref, cmk_ref, cmv_ref,
                       br_ref, ko_ref, vo_ref, bias_scr, sink_scr):
    nbuf = WINDOW

    @pl.when(pl.program_id(0) == 0)
    def _build_tables():
        dist = nbuf - 1 - lax.broadcasted_iota(jnp.int32, (V7X_SUBLANES, nbuf), 1)
        rid = lax.broadcasted_iota(jnp.int32, (V7X_SUBLANES, nbuf), 0)
        for g in range(SWA_KVH):
            acc = jnp.zeros((V7X_SUBLANES, nbuf), F32)
            sk = jnp.zeros((V7X_SUBLANES, nbuf), F32)
            for j in range(SWA_GROUP):
                hq = g * SWA_GROUP + j
                sk = jnp.where(rid == j, sink_ref[hq], sk)
                for bk, rng in enumerate(_BUCKET_RANGES):
                    if rng is not None:
                        hit = jnp.logical_and(rid == j, jnp.logical_and(dist >= rng[0], dist <= rng[1]))
                        acc = jnp.where(hit, relb_ref[bk, hq], acc)
            bias_scr[g] = acc
            sink_scr[g] = sk

    urow = u_ref[0]
    pieces = []
    pad = jnp.zeros((V7X_SUBLANES - SWA_GROUP, SWA_HD), F32)
    for g in range(SWA_KVH):
        k_new = urow[:, SWA_Q + g * SWA_HD:SWA_Q + (g + 1) * SWA_HD]
        v_new = urow[:, SWA_Q + SWA_KV + g * SWA_HD:SWA_Q + SWA_KV + (g + 1) * SWA_HD]
        kc = jnp.concatenate([ck_ref[1:nbuf, g, :], k_new], axis=0)
        vc = jnp.concatenate([cv_ref[1:nbuf, g, :], v_new], axis=0)
        ko_ref[:, g, :] = kc
        vo_ref[:, g, :] = vc
        qg = jnp.concatenate(
            [urow[:, (g * SWA_GROUP + j) * SWA_HD:(g * SWA_GROUP + j + 1) * SWA_HD] for j in range(SWA_GROUP)]
            + [pad], axis=0) * (SWA_HD ** -0.5)
        s = _dot_nt(qg, kc) + bias_scr[g]
        sink = sink_scr[g][:, 0:1]
        m = jnp.maximum(jnp.max(s, axis=-1, keepdims=True), sink)
        p = jnp.exp(s - m)
        l = jnp.sum(p, axis=-1, keepdims=True) + jnp.exp(sink - m)
        o = _dot(p, vc) / l
        pieces.extend(o[j:j + 1, :] for j in range(SWA_GROUP))
    mix = jnp.concatenate(pieces, axis=1)
    mem_o = _sample_mem_attend(urow, SWA_Q + 2 * SWA_KV, cmk_ref, cmv_ref)
    g0 = SWA_Q + 2 * SWA_KV + MEM_Q
    br_ref[0] = jnp.concatenate([mix, mem_o], axis=1) * _silu(urow[:, g0:g0 + D_MODEL])


def _cache_spec(layer, shape):
    return pl.BlockSpec((None, None) + shape, lambda i: (layer, i) + (0,) * len(shape))


def _sample_swa_attend(u, cache_k, cache_v, j, cache_mk, cache_mv, layer, rel_bias, sinks):
    ns = u.shape[0]
    nbuf = cache_k.shape[2]
    swa_shape = (nbuf, SWA_KVH, SWA_HD)
    mem_shape = (N_MEM, MEM_HEADS, MEM_HD)
    row_spec = lambda w: pl.BlockSpec((1, 1, w), lambda i: (i, 0, 0))
    new_cache = pl.BlockSpec((None,) + swa_shape, lambda i: (i, 0, 0, 0))
    br, ko, vo = pl.pallas_call(
        _sample_swa_kernel,
        grid=(ns,),
        in_specs=[_SMEM, _SMEM, row_spec(SWA_WIDTH), _cache_spec(j, swa_shape), _cache_spec(j, swa_shape),
                  _cache_spec(layer, mem_shape), _cache_spec(layer, mem_shape)],
        out_specs=[row_spec(D_MODEL), new_cache, new_cache],
        out_shape=[jax.ShapeDtypeStruct((ns, 1, D_MODEL), F32),
                   jax.ShapeDtypeStruct((ns,) + swa_shape, F32),
                   jax.ShapeDtypeStruct((ns,) + swa_shape, F32)],
        scratch_shapes=[pltpu.VMEM((SWA_KVH, V7X_SUBLANES, nbuf), F32),
                        pltpu.VMEM((SWA_KVH, V7X_SUBLANES, nbuf), F32)],
        compiler_params=_cparams(1),
        name="sample_swa_attend",
    )(rel_bias, sinks, u.reshape(ns, 1, SWA_WIDTH), cache_k, cache_v, cache_mk, cache_mv)
    return br.reshape(ns, D_MODEL), ko, vo


def _sample_hgrn_gate_kernel(u_ref, mix_ref, cmk_ref, cmv_ref, br_ref):
    urow = u_ref[0]
    mem_o = _sample_mem_attend(urow, 3 * HG_K, cmk_ref, cmv_ref)
    g0 = 3 * HG_K + MEM_Q
    br_ref[0] = jnp.concatenate([mix_ref[0], mem_o], axis=1) * _silu(urow[:, g0:g0 + D_MODEL])


def _sample_hgrn_gate(u, mix, cache_mk, cache_mv, layer):
    ns = u.shape[0]
    mem_shape = (N_MEM, MEM_HEADS, MEM_HD)
    row_spec = lambda w: pl.BlockSpec((1, 1, w), lambda i: (i, 0, 0))
    br = pl.pallas_call(
        _sample_hgrn_gate_kernel,
        grid=(ns,),
        in_specs=[row_spec(HG_WIDTH), row_spec(HG_K), _cache_spec(layer, mem_shape), _cache_spec(layer, mem_shape)],
        out_specs=row_spec(D_MODEL),
        out_shape=jax.ShapeDtypeStruct((ns, 1, D_MODEL), F32),
        compiler_params=_cparams(1),
        name="sample_hgrn_gate",
    )(u.reshape(ns, 1, HG_WIDTH), mix.reshape(ns, 1, HG_K), cache_mk, cache_mv)
    return br.reshape(ns, D_MODEL)


def _hgrn_sample_kernel(q_ref, f_ref, iv_ref, lbl_ref, nw_ref, st_ref, mix_ref, so_ref, o_scr, *, layer, ns):
    lb = _lower_bound(lbl_ref[...], layer)
    fg = lb + (1.0 - lb) * jax.nn.sigmoid(f_ref[...])
    kk_t = (1.0 - fg).T
    fg_t = fg.T
    qs_t = _silu(q_ref[...]).T
    for r in range(ns):
        state = fg_t[:, r:r + 1] * st_ref[r] + kk_t[:, r:r + 1] * iv_ref[r:r + 1, :]
        so_ref[r] = state
        o_scr[r:r + 1, :] = jnp.sum(qs_t[:, r:r + 1] * state, axis=0, keepdims=True)
    o = o_scr[...]
    ms = jnp.mean(o * o, axis=-1, keepdims=True)
    mix_ref[...] = o * lax.rsqrt(ms + RMS_EPS) * nw_ref[...]


def _hgrn_sample_step(u_hm, state, j, lb_logits, norm_w, layer):
    ns = u_hm.shape[0]
    col = lambda off: pl.BlockSpec((ns, HG_D), lambda h: (0, 3 * h + off))
    kernel = functools.partial(_hgrn_sample_kernel, layer=layer, ns=ns)
    return pl.pallas_call(
        kernel,
        grid=(HG_HEADS,),
        in_specs=[col(0), col(1), col(2),
                  pl.BlockSpec((DEPTH, HG_D), lambda h: (0, h)),
                  pl.BlockSpec((1, HG_D), lambda h: (0, h)),
                  pl.BlockSpec((None, ns, None, HG_D, HG_D), lambda h: (j, 0, h, 0, 0))],
        out_specs=[pl.BlockSpec((ns, HG_D), lambda h: (0, h)),
                   pl.BlockSpec((ns, None, HG_D, HG_D), lambda h: (0, h, 0, 0))],
        out_shape=[jax.ShapeDtypeStruct((ns, HG_K), F32),
                   jax.ShapeDtypeStruct(state.shape[1:], F32)],
        scratch_shapes=[pltpu.VMEM((ns, HG_D), F32)],
        compiler_params=_cparams(1),
        name="hgrn_sample_step",
    )(u_hm, u_hm, u_hm, lb_logits, norm_w.reshape(1, HG_K), state)


def _outproj_ln_kernel(br_ref, h_ref, wout_ref, lnw_ref, lnb_ref, out_ref):
    y = jnp.dot(br_ref[...].astype(BF16), wout_ref[...], preferred_element_type=F32)
    out_ref[...] = _layer_norm(DEEPNORM_ALPHA * h_ref[...] + y, lnw_ref[...], lnb_ref[...])


def _outproj_ln(branch, h, w_out, ln_w, ln_b):
    ns, d = h.shape
    full = lambda shape: pl.BlockSpec(shape, lambda i: (0, 0))
    return pl.pallas_call(
        _outproj_ln_kernel,
        grid=(1,),
        in_specs=[full((ns, d)), full((ns, d)), full((d, d)), full((1, d)), full((1, d))],
        out_specs=full((ns, d)),
        out_shape=jax.ShapeDtypeStruct((ns, d), F32),
        compiler_params=_cparams(1),
        name="outproj_ln",
    )(branch, h, w_out, ln_w.reshape(1, d), ln_b.reshape(1, d))


def kernel(x_prompt, x_sample, cache_mem_k, cache_mem_v, cache_swa_k, cache_swa_v, state_hgrn, mem_prompt, rel_bias, swa_w_in, swa_sinks, hg_w_in, hg_lb_logits, hg_norm_w, w_mem_k, w_mem_v, w_out, ln_w, ln_b):
    bp, t, d = x_prompt.shape
    ns = x_sample.shape[0]
    assert d == D_MODEL and x_sample.shape[1] == 1 and t % HG_CHUNK == 0 and t >= WINDOW
    assert cache_swa_k.shape[2] == WINDOW <= PAST_LEN
    assert w_mem_k.shape[0] == DEPTH

    wo = w_out.astype(BF16)
    mem_k_prompt, mem_v_prompt, mk_bf, mv_bf = _mem_kv_proj(mem_prompt, w_mem_k, w_mem_v)

    hp = x_prompt
    hs = x_sample.reshape(ns, d)
    swa_kp, swa_vp, swa_ks, swa_vs, hg_sp, hg_ss = [], [], [], [], [], []
    for i in range(DEPTH):
        j = i // 2
        if i % 2 == 0:
            w_in = swa_w_in[j].astype(BF16)
            hp, kw, vw = _swa_prompt_layer(hp, w_in, wo[i], mk_bf, mv_bf, i, rel_bias, swa_sinks[j],
                                           ln_w[i], ln_b[i], rows=WINDOW)
            swa_kp.append(kw.reshape(bp, WINDOW, SWA_KVH, SWA_HD))
            swa_vp.append(vw.reshape(bp, WINDOW, SWA_KVH, SWA_HD))
            u = _proj(hs, w_in, "sample_swa_proj")
            br, ko, vo = _sample_swa_attend(u, cache_swa_k, cache_swa_v, j, cache_mem_k, cache_mem_v, i,
                                            rel_bias, swa_sinks[j])
            swa_ks.append(ko)
            swa_vs.append(vo)
        else:
            w_in = _hgrn_head_major(hg_w_in[j]).astype(BF16)
            hp, st = _hgrn_prompt_layer(hp, w_in, wo[i], mk_bf, mv_bf, i, hg_lb_logits, hg_norm_w[j],
                                        ln_w[i], ln_b[i])
            hg_sp.append(st)
            u = _proj(hs, w_in, "sample_hgrn_proj")
            mix, so = _hgrn_sample_step(u, state_hgrn, j, hg_lb_logits, hg_norm_w[j], i)
            hg_ss.append(so)
            br = _sample_hgrn_gate(u, mix, cache_mem_k, cache_mem_v, i)
        hs = _outproj_ln(br, hs, wo[i], ln_w[i], ln_b[i])

    return (hp, hs.reshape(ns, 1, d), mem_k_prompt, mem_v_prompt,
            jnp.stack(swa_kp), jnp.stack(swa_vp), jnp.stack(hg_sp),
            jnp.stack(swa_ks), jnp.stack(swa_vs), jnp.stack(hg_ss))
```

```python
import functools
import math

import numpy as np
import jax
import jax.numpy as jnp
from jax import lax
from jax.experimental import pallas as pl
from jax.experimental.pallas import tpu as pltpu

F32 = jnp.float32
BF16 = jnp.bfloat16

D_MODEL = 2048
DEPTH = 2
PAST_LEN = 8192
N_MEM = 256
MEM_HEADS = 4
MEM_HD = 128
MEM_Q = MEM_HEADS * MEM_HD
WINDOW = 128
SWA_HD = 64
SWA_KVH = 4
SWA_GROUP = 6
SWA_QH = SWA_KVH * SWA_GROUP
SWA_Q = SWA_QH * SWA_HD
SWA_KV = SWA_KVH * SWA_HD
N_BUCKETS = 32
HG_HEADS = 12
HG_D = 128
HG_K = HG_HEADS * HG_D
SWA_WIDTH = SWA_Q + 2 * SWA_KV + MEM_Q + D_MODEL
HG_WIDTH = 3 * HG_K + MEM_Q + D_MODEL
DEEPNORM_ALPHA = (2.0 * DEPTH) ** 0.25
LN_EPS = 1e-5
RMS_EPS = 1e-6
MASKED = -1e30

V7X_SUBLANES = 8
V7X_LANES = 128
V7X_VMEM_LIMIT = 60000 * 1024
PROJ_COLS = 512


def _t5_bucket_ranges():
    n = np.arange(WINDOW)
    max_exact = N_BUCKETS // 2
    nf = np.maximum(n, 1).astype(np.float64)
    large = max_exact + (np.log(nf / max_exact) / math.log(WINDOW / max_exact)
                         * (N_BUCKETS - max_exact)).astype(np.int64)
    bucket = np.where(n < max_exact, n, np.minimum(large, N_BUCKETS - 1))
    ranges = []
    for b in range(N_BUCKETS):
        idx = np.nonzero(bucket == b)[0]
        ranges.append((int(idx[0]), int(idx[-1])) if idx.size else None)
    return ranges


_BUCKET_RANGES = _t5_bucket_ranges()


def _dot(a, b):
    return jnp.dot(a.astype(BF16), b.astype(BF16), preferred_element_type=F32)


def _dot_nt(a, b):
    return lax.dot_general(a.astype(BF16), b.astype(BF16), (((1,), (1,)), ((), ())),
                           preferred_element_type=F32)


def _layer_norm(z, w, b):
    mu = jnp.mean(z, axis=-1, keepdims=True)
    zc = z - mu
    var = jnp.mean(zc * zc, axis=-1, keepdims=True)
    return zc * lax.rsqrt(var + LN_EPS) * w + b


def _silu(x):
    return x * jax.nn.sigmoid(x)


def _cparams(n_axes):
    return pltpu.CompilerParams(dimension_semantics=("arbitrary",) * n_axes,
                                vmem_limit_bytes=V7X_VMEM_LIMIT)


def _resident(shape):
    nd = len(shape)
    return pl.BlockSpec(shape, lambda *_: (0,) * nd, pipeline_mode=pl.Buffered(1))


_SMEM = pl.BlockSpec(memory_space=pltpu.SMEM)


def _mem_kv_kernel(x_ref, wk_ref, wv_ref, ko_ref, vo_ref, kb_ref, vb_ref):
    xb = x_ref[...].astype(BF16)
    for w_ref, o_ref, b_ref in ((wk_ref, ko_ref, kb_ref), (wv_ref, vo_ref, vb_ref)):
        y = jnp.dot(xb, w_ref[...].astype(BF16), preferred_element_type=F32)
        b_ref[...] = y.astype(BF16)
        for hm in range(MEM_HEADS):
            o_ref[:, hm, :] = y[:, hm * MEM_HD:(hm + 1) * MEM_HD]


def _mem_kv_proj(mem, w_k, w_v):
    bsz, n, d = mem.shape
    nl = w_k.shape[0]
    w_spec = pl.BlockSpec((None, d, MEM_Q), lambda l, b: (l, 0, 0))
    o_spec = pl.BlockSpec((None, None, n, MEM_HEADS, MEM_HD), lambda l, b: (l, b, 0, 0, 0))
    b_spec = pl.BlockSpec((None, n, MEM_Q), lambda l, b: (l, b, 0))
    o_shape = jax.ShapeDtypeStruct((nl, bsz, n, MEM_HEADS, MEM_HD), F32)
    b_shape = jax.ShapeDtypeStruct((nl, bsz * n, MEM_Q), BF16)
    return pl.pallas_call(
        _mem_kv_kernel,
        grid=(nl, bsz),
        in_specs=[pl.BlockSpec((None, n, d), lambda l, b: (b, 0, 0)), w_spec, w_spec],
        out_specs=[o_spec, o_spec, b_spec, b_spec],
        out_shape=[o_shape, o_shape, b_shape, b_shape],
        compiler_params=_cparams(2),
        name="mem_kv_proj",
    )(mem, w_k, w_v)


def _proj_kernel(x_ref, w_ref, o_ref):
    o_ref[...] = jnp.dot(x_ref[...].astype(BF16), w_ref[...], preferred_element_type=F32)


def _proj(x, w, name):
    m, k = x.shape
    n = w.shape[1]
    return pl.pallas_call(
        _proj_kernel,
        grid=(n // PROJ_COLS,),
        in_specs=[pl.BlockSpec((m, k), lambda j: (0, 0)),
                  pl.BlockSpec((k, PROJ_COLS), lambda j: (0, j))],
        out_specs=pl.BlockSpec((m, PROJ_COLS), lambda j: (0, j)),
        out_shape=jax.ShapeDtypeStruct((m, n), F32),
        compiler_params=_cparams(1),
        name=name,
    )(x, w)


def _mem_scores(mq_all, mk_ref):
    return [_dot_nt(mq_all[:, hm * MEM_HD:(hm + 1) * MEM_HD], mk_ref[:, hm * MEM_HD:(hm + 1) * MEM_HD])
            * (MEM_HD ** -0.5) for hm in range(MEM_HEADS)]


def _softmax_weights(s):
    p = jnp.exp(s - jnp.max(s, axis=-1, keepdims=True))
    return p, jnp.sum(p, axis=-1, keepdims=True)


def _mem_values(weights, mv_ref, br_scr, out_col0):
    for hm, (p, l) in enumerate(weights):
        lo = hm * MEM_HD
        br_scr[:, out_col0 + lo:out_col0 + lo + MEM_HD] = _dot(p, mv_ref[:, lo:lo + MEM_HD]) / l


def _gate_project_norm(x, gate, br_scr, wout_ref, lnw_ref, lnb_ref):
    branch = br_scr[...] * _silu(gate)
    y = jnp.dot(branch.astype(BF16), wout_ref[...], preferred_element_type=F32)
    return _layer_norm(DEEPNORM_ALPHA * x + y, lnw_ref[...], lnb_ref[...])


def _swa_prompt_kernel(relb_ref, sink_ref, h_ref, win_ref, wout_ref, mk_ref, mv_ref, lnw_ref, lnb_ref,
                       out_ref, kout_ref, vout_ref,
                       bias_scr, k_scr, v_scr, br_scr, *, rows):
    b = pl.program_id(0)
    t = pl.program_id(1)
    w2 = 2 * WINDOW

    @pl.when(jnp.logical_and(b == 0, t == 0))
    def _build_bias():
        qi = lax.broadcasted_iota(jnp.int32, (WINDOW, w2), 0)
        kj = lax.broadcasted_iota(jnp.int32, (WINDOW, w2), 1)
        dist = qi + WINDOW - kj

        def per_head(hq, carry):
            acc = jnp.full((WINDOW, w2), MASKED, F32)
            for bk, rng in enumerate(_BUCKET_RANGES):
                if rng is not None:
                    hit = jnp.logical_and(dist >= rng[0], dist <= rng[1])
                    acc = jnp.where(hit, relb_ref[bk, hq], acc)
            bias_scr[0, hq] = acc
            bias_scr[1, hq] = jnp.where(kj < WINDOW, MASKED, acc)
            return carry

        lax.fori_loop(0, SWA_QH, per_head, 0)

    @pl.when(t == 0)
    def _reset_window():
        k_scr[0:WINDOW, :] = jnp.zeros((WINDOW, SWA_KV), BF16)
        v_scr[0:WINDOW, :] = jnp.zeros((WINDOW, SWA_KV), BF16)

    x = h_ref[0]
    xb = x.astype(BF16)
    qkv_w = SWA_Q + 2 * SWA_KV
    qkv = jnp.dot(xb, win_ref[:, :qkv_w], preferred_element_type=F32)
    rest = jnp.dot(xb, win_ref[:, qkv_w:], preferred_element_type=F32)
    k = qkv[:, SWA_Q:SWA_Q + SWA_KV]
    v = qkv[:, SWA_Q + SWA_KV:]
    kout_ref[0] = k[rows - WINDOW:, :]
    vout_ref[0] = v[rows - WINDOW:, :]
    k_scr[WINDOW:WINDOW + rows, :] = k.astype(BF16)
    v_scr[WINDOW:WINDOW + rows, :] = v.astype(BF16)

    heads = [(qb, g, j) for qb in range(rows // WINDOW) for g in range(SWA_KVH) for j in range(SWA_GROUP)]

    scores = []
    for qb, g, j in heads:
        r0 = qb * WINDOW
        hq = g * SWA_GROUP + j
        table = jnp.where(t == 0, 1, 0) if qb == 0 else 0
        q = qkv[r0:r0 + WINDOW, hq * SWA_HD:(hq + 1) * SWA_HD] * (SWA_HD ** -0.5)
        scores.append(_dot_nt(q, k_scr[r0:r0 + w2, g * SWA_HD:(g + 1) * SWA_HD]) + bias_scr[table, hq])
    mem_s = _mem_scores(rest[:, :MEM_Q], mk_ref)

    weights = []
    for (qb, g, j), s in zip(heads, scores):
        sink = sink_ref[g * SWA_GROUP + j]
        m = jnp.maximum(jnp.max(s, axis=-1, keepdims=True), sink)
        p = jnp.exp(s - m)
        weights.append((p, jnp.sum(p, axis=-1, keepdims=True) + jnp.exp(sink - m)))
    mem_w = [_softmax_weights(s) for s in mem_s]

    for (qb, g, j), (p, l) in zip(heads, weights):
        r0 = qb * WINDOW
        hq = g * SWA_GROUP + j
        o = _dot(p, v_scr[r0:r0 + w2, g * SWA_HD:(g + 1) * SWA_HD]) / l
        br_scr[r0:r0 + WINDOW, hq * SWA_HD:(hq + 1) * SWA_HD] = o
    _mem_values(mem_w, mv_ref, br_scr, SWA_Q)

    k_scr[0:WINDOW, :] = k_scr[rows:rows + WINDOW, :]
    v_scr[0:WINDOW, :] = v_scr[rows:rows + WINDOW, :]

    out_ref[0] = _gate_project_norm(x, rest[:, MEM_Q:], br_scr, wout_ref, lnw_ref, lnb_ref)


def _swa_prompt_layer(h, w_in, w_out, mk_bf, mv_bf, layer, rel_bias, sinks, ln_w, ln_b, rows):
    bsz, t, d = h.shape
    kernel = functools.partial(_swa_prompt_kernel, rows=rows)
    mem_spec = pl.BlockSpec((None, N_MEM, MEM_Q), lambda b, i: (layer, b, 0))
    return pl.pallas_call(
        kernel,
        grid=(bsz, t // rows),
        in_specs=[_SMEM, _SMEM,
                  pl.BlockSpec((1, rows, d), lambda b, i: (b, i, 0)),
                  _resident((d, SWA_WIDTH)),
                  _resident((d, d)),
                  mem_spec, mem_spec,
                  _resident((1, d)), _resident((1, d))],
        out_specs=[pl.BlockSpec((1, rows, d), lambda b, i: (b, i, 0)),
                   pl.BlockSpec((1, WINDOW, SWA_KV), lambda b, i: (b, 0, 0)),
                   pl.BlockSpec((1, WINDOW, SWA_KV), lambda b, i: (b, 0, 0))],
        out_shape=[jax.ShapeDtypeStruct((bsz, t, d), F32),
                   jax.ShapeDtypeStruct((bsz, WINDOW, SWA_KV), F32),
                   jax.ShapeDtypeStruct((bsz, WINDOW, SWA_KV), F32)],
        scratch_shapes=[pltpu.VMEM((2, SWA_QH, WINDOW, 2 * WINDOW), F32),
                        pltpu.VMEM((WINDOW + rows, SWA_KV), BF16),
                        pltpu.VMEM((WINDOW + rows, SWA_KV), BF16),
                        pltpu.VMEM((rows, d), F32)],
        compiler_params=_cparams(2),
        name="swa_prompt_layer",
    )(rel_bias, sinks, h, w_in, w_out, mk_bf, mv_bf, ln_w.reshape(1, d), ln_b.reshape(1, d))


HG_CHUNK = 128
HG_DIRECT = 8
HG_HEAD_COLS = 3 * HG_D


def _hgrn_head_major(w):
    d = w.shape[0]
    qfi = w[:, :3 * HG_K].reshape(d, 3, HG_HEADS, HG_D).transpose(0, 2, 1, 3).reshape(d, 3 * HG_K)
    return jnp.concatenate([qfi, w[:, 3 * HG_K:]], axis=1)


def _lower_bound(logits, layer):
    e = jnp.exp(logits - jnp.max(logits, axis=0, keepdims=True))
    sm = e / jnp.sum(e, axis=0, keepdims=True)
    return jnp.sum(sm[1:layer + 1, :], axis=0, keepdims=True)


def _cumsum_rows(tril_bf, g):
    g1 = g.astype(BF16)
    r1 = g - g1.astype(F32)
    g2 = r1.astype(BF16)
    g3 = (r1 - g2.astype(F32)).astype(BF16)
    d = jnp.dot(tril_bf, jnp.concatenate([g1, g2, g3], axis=1), preferred_element_type=F32)
    return (d[:, :HG_D] + d[:, HG_D:2 * HG_D]) + d[:, 2 * HG_D:]


def _hgrn_chunk_scores(qs, kk, gc):
    c = HG_CHUNK
    nb = c // HG_DIRECT
    q3 = qs.reshape(nb, HG_DIRECT, HG_D)
    k3 = kk.reshape(nb, HG_DIRECT, HG_D)
    g3 = gc.reshape(nb, HG_DIRECT, HG_D)
    lane = lax.broadcasted_iota(jnp.int32, (nb, HG_DIRECT, c), 2)
    blk = lax.broadcasted_iota(jnp.int32, (nb, HG_DIRECT, c), 0)
    trow = lax.broadcasted_iota(jnp.int32, (nb, HG_DIRECT, c), 1)
    a3 = jnp.zeros((nb, HG_DIRECT, c), F32)
    for sl in range(HG_DIRECT):
        e = jnp.exp(g3 - g3[:, sl:sl + 1, :])
        col = jnp.sum(e * q3 * k3[:, sl:sl + 1, :], axis=-1, keepdims=True)
        hit = jnp.logical_and(lane == blk * HG_DIRECT + sl, trow >= sl)
        a3 = jnp.where(hit, col, a3)
    a = a3.reshape(c, c)

    ti = lax.broadcasted_iota(jnp.int32, (c, c), 0)
    si = lax.broadcasted_iota(jnp.int32, (c, c), 1)
    ri = lax.broadcasted_iota(jnp.int32, (c, 1), 0)
    half = HG_DIRECT
    while half < c:
        span = 2 * half
        gsp = gc.reshape(c // span, span, HG_D)
        gref = jnp.broadcast_to(gsp[:, half - 1:half, :], gsp.shape).reshape(c, HG_D)
        upper = (ri % span) >= half
        eq = jnp.where(upper, jnp.exp(gc - gref), 0.0)
        ek = jnp.where(upper, 0.0, jnp.exp(gref - gc))
        a = a + jnp.where((ti // span) == (si // span), _dot_nt(qs * eq, kk * ek), 0.0)
        half = span
    return a


def _hgrn_prompt_kernel(h_ref, win_ref, wout_ref, lbl_ref, nw_ref, mk_ref, mv_ref, lnw_ref, lnb_ref,
                        out_ref, st_ref,
                        st_scr, br_scr, *, layer):
    t = pl.program_id(1)
    c = HG_CHUNK

    @pl.when(t == 0)
    def _reset_state():
        st_scr[...] = jnp.zeros(st_scr.shape, F32)

    lb_all = _lower_bound(lbl_ref[...], layer)
    x = h_ref[0]
    xb = x.astype(BF16)
    ri = lax.broadcasted_iota(jnp.int32, (c, c), 0)
    ci = lax.broadcasted_iota(jnp.int32, (c, c), 1)
    tril_bf = jnp.where(ri >= ci, 1.0, 0.0).astype(BF16)

    heads = range(HG_HEADS)
    uh = [jnp.dot(xb, win_ref[:, hh * HG_HEAD_COLS:(hh + 1) * HG_HEAD_COLS], preferred_element_type=F32)
          for hh in heads]
    ug = jnp.dot(xb, win_ref[:, 3 * HG_K:], preferred_element_type=F32)
    mem_s = _mem_scores(ug[:, :MEM_Q], mk_ref)

    qs, kk, iv, gc = [], [], [], []
    for hh in heads:
        lb = lb_all[:, hh * HG_D:(hh + 1) * HG_D]
        fg = lb + (1.0 - lb) * jax.nn.sigmoid(uh[hh][:, HG_D:2 * HG_D])
        qs.append(_silu(uh[hh][:, :HG_D]))
        iv.append(uh[hh][:, 2 * HG_D:])
        kk.append(1.0 - fg)
        gc.append(_cumsum_rows(tril_bf, jnp.log(fg)))
    _mem_values([_softmax_weights(s) for s in mem_s], mv_ref, br_scr, HG_K)

    a = [_hgrn_chunk_scores(qs[hh], kk[hh], gc[hh]) for hh in heads]

    iv_t = [iv[hh].T for hh in heads]
    for hh in heads:
        lhs = jnp.concatenate([a[hh], qs[hh] * jnp.exp(gc[hh])], axis=1)
        rhs = jnp.concatenate([iv_t[hh], st_scr[hh]], axis=1)
        o = _dot_nt(lhs, rhs)
        ms = jnp.mean(o * o, axis=-1, keepdims=True)
        br_scr[:, hh * HG_D:(hh + 1) * HG_D] = o * lax.rsqrt(ms + RMS_EPS) * nw_ref[:, hh * HG_D:(hh + 1) * HG_D]

    for hh in heads:
        g_last = gc[hh][c - 1:c, :]
        kdec = kk[hh] * jnp.exp(g_last - gc[hh])
        st_scr[hh] = st_scr[hh] * jnp.exp(g_last) + _dot(iv_t[hh], kdec)

    @pl.when(t == pl.num_programs(1) - 1)
    def _emit_state():
        for hh in heads:
            st_ref[0, hh] = st_scr[hh].T

    out_ref[0] = _gate_project_norm(x, ug[:, MEM_Q:], br_scr, wout_ref, lnw_ref, lnb_ref)


def _hgrn_prompt_layer(h, w_in_hm, w_out, mk_bf, mv_bf, layer, lb_logits, norm_w, ln_w, ln_b):
    bsz, t, d = h.shape
    rows = HG_CHUNK
    kernel = functools.partial(_hgrn_prompt_kernel, layer=layer)
    mem_spec = pl.BlockSpec((None, N_MEM, MEM_Q), lambda b, i: (layer, b, 0))
    return pl.pallas_call(
        kernel,
        grid=(bsz, t // rows),
        in_specs=[pl.BlockSpec((1, rows, d), lambda b, i: (b, i, 0)),
                  _resident((d, HG_WIDTH)),
                  _resident((d, d)),
                  _resident((DEPTH, HG_K)),
                  _resident((1, HG_K)),
                  mem_spec, mem_spec,
                  _resident((1, d)), _resident((1, d))],
        out_specs=[pl.BlockSpec((1, rows, d), lambda b, i: (b, i, 0)),
                   pl.BlockSpec((1, HG_HEADS, HG_D, HG_D), lambda b, i: (b, 0, 0, 0))],
        out_shape=[jax.ShapeDtypeStruct((bsz, t, d), F32),
                   jax.ShapeDtypeStruct((bsz, HG_HEADS, HG_D, HG_D), F32)],
        scratch_shapes=[pltpu.VMEM((HG_HEADS, HG_D, HG_D), F32),
                        pltpu.VMEM((rows, d), F32)],
        compiler_params=_cparams(2),
        name="hgrn_prompt_layer",
    )(h, w_in_hm, w_out, lb_logits, norm_w.reshape(1, HG_K), mk_bf, mv_bf, ln_w.reshape(1, d), ln_b.reshape(1, d))


REQS_PER_STEP = 4


def _heads_to_sublanes(row, col0, n, width):
    return jnp.concatenate([row[:, col0 + i * width:col0 + (i + 1) * width] for i in range(n)], axis=0)


def _sample_mem_attend(urow, col0, cmk_ref, cmv_ref, r):
    pieces = []
    for hm in range(MEM_HEADS):
        lo = col0 + hm * MEM_HD
        rows = pl.ds(hm, N_MEM, stride=MEM_HEADS)
        s = jnp.sum(cmk_ref[r, rows, :] * urow[:, lo:lo + MEM_HD], axis=-1, keepdims=True) * (MEM_HD ** -0.5)
        s = jnp.broadcast_to(s, (N_MEM, MEM_HD))
        p = jnp.exp(s - jnp.max(s, axis=0, keepdims=True))
        pieces.append(jnp.sum(p * cmv_ref[r, rows, :], axis=0, keepdims=True) / jnp.sum(p, axis=0, keepdims=True))
    return jnp.concatenate(pieces, axis=1)


def _sample_swa_kernel(relb_ref, sink_ref, u_ref, ck_ref, cv_ref, cmk_ref, cmv_ref,
                       br_ref, ko_ref, vo_ref, bias_scr, bias0_scr, sink_scr):
    nbuf = WINDOW

    @pl.when(pl.program_id(0) == 0)
    def _build_tables():
        dist = nbuf - lax.broadcasted_iota(jnp.int32, (V7X_SUBLANES, nbuf), 1)
        rid = lax.broadcasted_iota(jnp.int32, (V7X_SUBLANES, nbuf), 0)
        for g in range(SWA_KVH):
            acc = jnp.where(rid < SWA_GROUP, MASKED, 0.0).astype(F32)
            acc0 = jnp.zeros((V7X_SUBLANES, nbuf), F32)
            sk = jnp.zeros((V7X_SUBLANES, nbuf), F32)
            for j in range(SWA_GROUP):
                hq = g * SWA_GROUP + j
                sk = jnp.where(rid == j, sink_ref[hq], sk)
                acc0 = jnp.where(rid == j, relb_ref[0, hq], acc0)
                for bk, rng in enumerate(_BUCKET_RANGES):
                    if rng is not None:
                        hit = jnp.logical_and(rid == j, jnp.logical_and(dist >= rng[0], dist <= rng[1]))
                        acc = jnp.where(hit, relb_ref[bk, hq], acc)
            bias_scr[g] = acc
            bias0_scr[g] = acc0
            sink_scr[g] = sk

    g0 = SWA_Q + 2 * SWA_KV + MEM_Q
    pad = jnp.zeros((V7X_SUBLANES - SWA_GROUP, SWA_HD), F32)
    pairs = [(r, g) for r in range(REQS_PER_STEP) for g in range(SWA_KVH)]
    urows = [u_ref[r] for r in range(REQS_PER_STEP)]

    scores = []
    for r, g in pairs:
        kg = ck_ref[r, pl.ds(g, nbuf, stride=SWA_KVH), :]
        k_new = urows[r][:, SWA_Q + g * SWA_HD:SWA_Q + (g + 1) * SWA_HD]
        v_new = urows[r][:, SWA_Q + SWA_KV + g * SWA_HD:SWA_Q + SWA_KV + (g + 1) * SWA_HD]
        older = pl.ds(SWA_KVH + g, nbuf - 1, stride=SWA_KVH)
        kept = pl.ds(g, nbuf - 1, stride=SWA_KVH)
        ko_ref[r, kept, :] = ck_ref[r, older, :]
        vo_ref[r, kept, :] = cv_ref[r, older, :]
        last = (nbuf - 1) * SWA_KVH + g
        ko_ref[r, last:last + 1, :] = k_new
        vo_ref[r, last:last + 1, :] = v_new
        qg = jnp.concatenate(
            [urows[r][:, (g * SWA_GROUP + j) * SWA_HD:(g * SWA_GROUP + j + 1) * SWA_HD] for j in range(SWA_GROUP)]
            + [pad], axis=0) * (SWA_HD ** -0.5)
        s = _dot_nt(qg, kg) + bias_scr[g]
        s_new = jnp.sum(qg * k_new, axis=-1, keepdims=True) + bias0_scr[g][:, 0:1]
        scores.append((s, s_new, v_new))

    weights = []
    for (r, g), (s, s_new, v_new) in zip(pairs, scores):
        sink = sink_scr[g][:, 0:1]
        m = jnp.maximum(jnp.max(s, axis=-1, keepdims=True), jnp.maximum(s_new, sink))
        p = jnp.exp(s - m)
        p_new = jnp.exp(s_new - m)
        l = jnp.sum(p, axis=-1, keepdims=True) + p_new + jnp.exp(sink - m)
        weights.append((p, p_new * v_new, l))

    pieces = [[] for _ in range(REQS_PER_STEP)]
    for (r, g), (p, o_new, l) in zip(pairs, weights):
        o = (_dot(p, cv_ref[r, pl.ds(g, nbuf, stride=SWA_KVH), :]) + o_new) / l
        pieces[r].extend(o[j:j + 1, :] for j in range(SWA_GROUP))

    for r in range(REQS_PER_STEP):
        mem_o = _sample_mem_attend(urows[r], SWA_Q + 2 * SWA_KV, cmk_ref, cmv_ref, r)
        br_ref[r] = jnp.concatenate(pieces[r] + [mem_o], axis=1) * _silu(urows[r][:, g0:g0 + D_MODEL])


def _cache_spec(layer, shape):
    return pl.BlockSpec((None, REQS_PER_STEP) + shape, lambda i: (layer, i) + (0,) * len(shape))


def _row_spec(width):
    return pl.BlockSpec((REQS_PER_STEP, 1, width), lambda i: (i, 0, 0))


def _head_minor_rows(cache):
    nl, ns, rows, heads, hd = cache.shape
    return cache.reshape(nl, ns, rows * heads, hd)


def _sample_swa_attend(u, cache_k, cache_v, j, cache_mk, cache_mv, layer, rel_bias, sinks):
    ns = u.shape[0]
    nbuf = cache_k.shape[2] // SWA_KVH
    swa_shape = (nbuf * SWA_KVH, SWA_HD)
    mem_shape = (N_MEM * MEM_HEADS, MEM_HD)
    new_cache = pl.BlockSpec((REQS_PER_STEP,) + swa_shape, lambda i: (i, 0, 0))
    br, ko, vo = pl.pallas_call(
        _sample_swa_kernel,
        grid=(ns // REQS_PER_STEP,),
        in_specs=[_SMEM, _SMEM, _row_spec(SWA_WIDTH), _cache_spec(j, swa_shape), _cache_spec(j, swa_shape),
                  _cache_spec(layer, mem_shape), _cache_spec(layer, mem_shape)],
        out_specs=[_row_spec(D_MODEL), new_cache, new_cache],
        out_shape=[jax.ShapeDtypeStruct((ns, 1, D_MODEL), F32),
                   jax.ShapeDtypeStruct((ns,) + swa_shape, F32),
                   jax.ShapeDtypeStruct((ns,) + swa_shape, F32)],
        scratch_shapes=[pltpu.VMEM((SWA_KVH, V7X_SUBLANES, nbuf), F32)] * 3,
        compiler_params=_cparams(1),
        name="sample_swa_attend",
    )(rel_bias, sinks, u.reshape(ns, 1, SWA_WIDTH), cache_k, cache_v, cache_mk, cache_mv)
    return br.reshape(ns, D_MODEL), ko, vo


def _sample_hgrn_gate_kernel(u_ref, mix_ref, cmk_ref, cmv_ref, br_ref):
    g0 = 3 * HG_K + MEM_Q
    for r in range(REQS_PER_STEP):
        urow = u_ref[r]
        mem_o = _sample_mem_attend(urow, 3 * HG_K, cmk_ref, cmv_ref, r)
        br_ref[r] = jnp.concatenate([mix_ref[r], mem_o], axis=1) * _silu(urow[:, g0:g0 + D_MODEL])


def _sample_hgrn_gate(u, mix, cache_mk, cache_mv, layer):
    ns = u.shape[0]
    mem_shape = (N_MEM * MEM_HEADS, MEM_HD)
    br = pl.pallas_call(
        _sample_hgrn_gate_kernel,
        grid=(ns // REQS_PER_STEP,),
        in_specs=[_row_spec(HG_WIDTH), _row_spec(HG_K), _cache_spec(layer, mem_shape), _cache_spec(layer, mem_shape)],
        out_specs=_row_spec(D_MODEL),
        out_shape=jax.ShapeDtypeStruct((ns, 1, D_MODEL), F32),
        compiler_params=_cparams(1),
        name="sample_hgrn_gate",
    )(u.reshape(ns, 1, HG_WIDTH), mix.reshape(ns, 1, HG_K), cache_mk, cache_mv)
    return br.reshape(ns, D_MODEL)


def _hgrn_sample_kernel(q_ref, f_ref, iv_ref, lbl_ref, nw_ref, st_ref, mix_ref, so_ref, o_scr, *, layer, ns):
    lb = _lower_bound(lbl_ref[...], layer)
    fg = lb + (1.0 - lb) * jax.nn.sigmoid(f_ref[...])
    kk_t = (1.0 - fg).T
    fg_t = fg.T
    qs_t = _silu(q_ref[...]).T
    for r in range(ns):
        state = fg_t[:, r:r + 1] * st_ref[r] + kk_t[:, r:r + 1] * iv_ref[r:r + 1, :]
        so_ref[r] = state
        o_scr[r:r + 1, :] = jnp.sum(qs_t[:, r:r + 1] * state, axis=0, keepdims=True)
    o = o_scr[...]
    ms = jnp.mean(o * o, axis=-1, keepdims=True)
    mix_ref[...] = o * lax.rsqrt(ms + RMS_EPS) * nw_ref[...]


def _hgrn_sample_step(u_hm, state, j, lb_logits, norm_w, layer):
    ns = u_hm.shape[0]
    col = lambda off: pl.BlockSpec((ns, HG_D), lambda h: (0, 3 * h + off))
    kernel = functools.partial(_hgrn_sample_kernel, layer=layer, ns=ns)
    return pl.pallas_call(
        kernel,
        grid=(HG_HEADS,),
        in_specs=[col(0), col(1), col(2),
                  pl.BlockSpec((DEPTH, HG_D), lambda h: (0, h)),
                  pl.BlockSpec((1, HG_D), lambda h: (0, h)),
                  pl.BlockSpec((None, ns, None, HG_D, HG_D), lambda h: (j, 0, h, 0, 0))],
        out_specs=[pl.BlockSpec((ns, HG_D), lambda h: (0, h)),
                   pl.BlockSpec((ns, None, HG_D, HG_D), lambda h: (0, h, 0, 0))],
        out_shape=[jax.ShapeDtypeStruct((ns, HG_K), F32),
                   jax.ShapeDtypeStruct(state.shape[1:], F32)],
        scratch_shapes=[pltpu.VMEM((ns, HG_D), F32)],
        compiler_params=_cparams(1),
        name="hgrn_sample_step",
    )(u_hm, u_hm, u_hm, lb_logits, norm_w.reshape(1, HG_K), state)


def _outproj_ln_kernel(br_ref, h_ref, wout_ref, lnw_ref, lnb_ref, out_ref):
    y = jnp.dot(br_ref[...].astype(BF16), wout_ref[...], preferred_element_type=F32)
    out_ref[...] = _layer_norm(DEEPNORM_ALPHA * h_ref[...] + y, lnw_ref[...], lnb_ref[...])


def _outproj_ln(branch, h, w_out, ln_w, ln_b):
    ns, d = h.shape
    full = lambda shape: pl.BlockSpec(shape, lambda i: (0, 0))
    return pl.pallas_call(
        _outproj_ln_kernel,
        grid=(1,),
        in_specs=[full((ns, d)), full((ns, d)), full((d, d)), full((1, d)), full((1, d))],
        out_specs=full((ns, d)),
        out_shape=jax.ShapeDtypeStruct((ns, d), F32),
        compiler_params=_cparams(1),
        name="outproj_ln",
    )(branch, h, w_out, ln_w.reshape(1, d), ln_b.reshape(1, d))


def kernel(x_prompt, x_sample, cache_mem_k, cache_mem_v, cache_swa_k, cache_swa_v, state_hgrn, mem_prompt, rel_bias, swa_w_in, swa_sinks, hg_w_in, hg_lb_logits, hg_norm_w, w_mem_k, w_mem_v, w_out, ln_w, ln_b):
    bp, t, d = x_prompt.shape
    ns = x_sample.shape[0]
    assert d == D_MODEL and x_sample.shape[1] == 1 and t % HG_CHUNK == 0 and t >= WINDOW
    assert cache_swa_k.shape[2] == WINDOW <= PAST_LEN
    assert w_mem_k.shape[0] == DEPTH

    wo = w_out.astype(BF16)
    mem_k_prompt, mem_v_prompt, mk_bf, mv_bf = _mem_kv_proj(mem_prompt, w_mem_k, w_mem_v)
    cmk, cmv = _head_minor_rows(cache_mem_k), _head_minor_rows(cache_mem_v)
    csk, csv = _head_minor_rows(cache_swa_k), _head_minor_rows(cache_swa_v)

    hp = x_prompt
    hs = x_sample.reshape(ns, d)
    swa_kp, swa_vp, swa_ks, swa_vs, hg_sp, hg_ss = [], [], [], [], [], []
    for i in range(DEPTH):
        j = i // 2
        if i % 2 == 0:
            w_in = swa_w_in[j].astype(BF16)
            hp, kw, vw = _swa_prompt_layer(hp, w_in, wo[i], mk_bf, mv_bf, i, rel_bias, swa_sinks[j],
                                           ln_w[i], ln_b[i], rows=WINDOW)
            swa_kp.append(kw.reshape(bp, WINDOW, SWA_KVH, SWA_HD))
            swa_vp.append(vw.reshape(bp, WINDOW, SWA_KVH, SWA_HD))
            u = _proj(hs, w_in, "sample_swa_proj")
            br, ko, vo = _sample_swa_attend(u, csk, csv, j, cmk, cmv, i, rel_bias, swa_sinks[j])
            swa_ks.append(ko.reshape(ns, WINDOW, SWA_KVH, SWA_HD))
            swa_vs.append(vo.reshape(ns, WINDOW, SWA_KVH, SWA_HD))
        else:
            w_in = _hgrn_head_major(hg_w_in[j]).astype(BF16)
            hp, st = _hgrn_prompt_layer(hp, w_in, wo[i], mk_bf, mv_bf, i, hg_lb_logits, hg_norm_w[j],
                                        ln_w[i], ln_b[i])
            hg_sp.append(st)
            u = _proj(hs, w_in, "sample_hgrn_proj")
            mix, so = _hgrn_sample_step(u, state_hgrn, j, hg_lb_logits, hg_norm_w[j], i)
            hg_ss.append(so)
            br = _sample_hgrn_gate(u, mix, cmk, cmv, i)
        hs = _outproj_ln(br, hs, wo[i], ln_w[i], ln_b[i])

    return (hp, hs.reshape(ns, 1, d), mem_k_prompt, mem_v_prompt,
            jnp.stack(swa_kp), jnp.stack(swa_vp), jnp.stack(hg_sp),
            jnp.stack(swa_ks), jnp.stack(swa_vs), jnp.stack(hg_ss))
```

```python
import functools
import math

import numpy as np
import jax
import jax.numpy as jnp
from jax import lax
from jax.experimental import pallas as pl
from jax.experimental.pallas import tpu as pltpu

F32 = jnp.float32
BF16 = jnp.bfloat16

D_MODEL = 2048
DEPTH = 2
PAST_LEN = 8192
N_MEM = 256
MEM_HEADS = 4
MEM_HD = 128
MEM_Q = MEM_HEADS * MEM_HD
WINDOW = 128
SWA_HD = 64
SWA_KVH = 4
SWA_GROUP = 6
SWA_QH = SWA_KVH * SWA_GROUP
SWA_Q = SWA_QH * SWA_HD
SWA_KV = SWA_KVH * SWA_HD
N_BUCKETS = 32
HG_HEADS = 12
HG_D = 128
HG_K = HG_HEADS * HG_D
SWA_WIDTH = SWA_Q + 2 * SWA_KV + MEM_Q + D_MODEL
HG_WIDTH = 3 * HG_K + MEM_Q + D_MODEL
DEEPNORM_ALPHA = (2.0 * DEPTH) ** 0.25
LN_EPS = 1e-5
RMS_EPS = 1e-6
MASKED = -1e30

V7X_SUBLANES = 8
V7X_LANES = 128
V7X_VMEM_LIMIT = 60000 * 1024
PROJ_COLS = 512
GATE_CHUNK = 512
PROMPT_ROWS = 256
HG_ROWS = 128


def _t5_bucket_ranges():
    n = np.arange(WINDOW)
    max_exact = N_BUCKETS // 2
    nf = np.maximum(n, 1).astype(np.float64)
    large = max_exact + (np.log(nf / max_exact) / math.log(WINDOW / max_exact)
                         * (N_BUCKETS - max_exact)).astype(np.int64)
    bucket = np.where(n < max_exact, n, np.minimum(large, N_BUCKETS - 1))
    ranges = []
    for b in range(N_BUCKETS):
        idx = np.nonzero(bucket == b)[0]
        ranges.append((int(idx[0]), int(idx[-1])) if idx.size else None)
    return ranges


_BUCKET_RANGES = _t5_bucket_ranges()


def _dot(a, b):
    return jnp.dot(a.astype(BF16), b.astype(BF16), preferred_element_type=F32)


def _dot_nt(a, b):
    return lax.dot_general(a.astype(BF16), b.astype(BF16), (((1,), (1,)), ((), ())),
                           preferred_element_type=F32)


def _layer_norm(z, w, b):
    mu = jnp.mean(z, axis=-1, keepdims=True)
    zc = z - mu
    var = jnp.mean(zc * zc, axis=-1, keepdims=True)
    return zc * lax.rsqrt(var + LN_EPS) * w + b


def _silu(x):
    return x * jax.nn.sigmoid(x)


def _cparams(n_axes):
    return pltpu.CompilerParams(dimension_semantics=("arbitrary",) * n_axes,
                                vmem_limit_bytes=V7X_VMEM_LIMIT)


def _resident(shape):
    nd = len(shape)
    return pl.BlockSpec(shape, lambda *_: (0,) * nd, pipeline_mode=pl.Buffered(1))


_SMEM = pl.BlockSpec(memory_space=pltpu.SMEM)


def _mem_kv_kernel(x_ref, wk_ref, wv_ref, ko_ref, vo_ref, kb_ref, vb_ref):
    xb = x_ref[...].astype(BF16)
    for w_ref, o_ref, b_ref in ((wk_ref, ko_ref, kb_ref), (wv_ref, vo_ref, vb_ref)):
        y = jnp.dot(xb, w_ref[...].astype(BF16), preferred_element_type=F32)
        b_ref[...] = y.astype(BF16)
        for hm in range(MEM_HEADS):
            o_ref[:, hm, :] = y[:, hm * MEM_HD:(hm + 1) * MEM_HD]


def _mem_kv_proj(mem, w_k, w_v):
    bsz, n, d = mem.shape
    nl = w_k.shape[0]
    w_spec = pl.BlockSpec((None, d, MEM_Q), lambda l, b: (l, 0, 0))
    o_spec = pl.BlockSpec((None, None, n, MEM_HEADS, MEM_HD), lambda l, b: (l, b, 0, 0, 0))
    b_spec = pl.BlockSpec((None, n, MEM_Q), lambda l, b: (l, b, 0))
    o_shape = jax.ShapeDtypeStruct((nl, bsz, n, MEM_HEADS, MEM_HD), F32)
    b_shape = jax.ShapeDtypeStruct((nl, bsz * n, MEM_Q), BF16)
    return pl.pallas_call(
        _mem_kv_kernel,
        grid=(nl, bsz),
        in_specs=[pl.BlockSpec((None, n, d), lambda l, b: (b, 0, 0)), w_spec, w_spec],
        out_specs=[o_spec, o_spec, b_spec, b_spec],
        out_shape=[o_shape, o_shape, b_shape, b_shape],
        compiler_params=_cparams(2),
        name="mem_kv_proj",
    )(mem, w_k, w_v)


def _proj_kernel(x_ref, w_ref, o_ref):
    o_ref[...] = jnp.dot(x_ref[...].astype(BF16), w_ref[...], preferred_element_type=F32)


def _proj(x, w, name):
    m, k = x.shape
    n = w.shape[1]
    return pl.pallas_call(
        _proj_kernel,
        grid=(n // PROJ_COLS,),
        in_specs=[pl.BlockSpec((m, k), lambda j: (0, 0)),
                  pl.BlockSpec((k, PROJ_COLS), lambda j: (0, j))],
        out_specs=pl.BlockSpec((m, PROJ_COLS), lambda j: (0, j)),
        out_shape=jax.ShapeDtypeStruct((m, n), F32),
        compiler_params=_cparams(1),
        name=name,
    )(x, w)


def _mem_scores(mq_all, mk_ref):
    return [_dot_nt(mq_all[:, hm * MEM_HD:(hm + 1) * MEM_HD], mk_ref[:, hm * MEM_HD:(hm + 1) * MEM_HD])
            * (MEM_HD ** -0.5) for hm in range(MEM_HEADS)]


def _put_gated(bb_ref, gs_ref, col0, o):
    cols = slice(col0, col0 + o.shape[-1])
    bb_ref[:, cols] = (o * gs_ref[:, cols]).astype(BF16)


def _mem_values(weights, mv_ref, bb_ref, gs_ref, out_col0):
    for hm, (p, l) in enumerate(weights):
        lo = hm * MEM_HD
        _put_gated(bb_ref, gs_ref, out_col0 + lo, _dot(p, mv_ref[:, lo:lo + MEM_HD]) / l)


def _swa_prompt_kernel(relb_ref, sink_ref, h_ref, win_ref, wout_ref, mk_ref, mv_ref, lnw_ref, lnb_ref,
                       out_ref, kout_ref, vout_ref,
                       bias_scr, k_scr, v_scr, gs_scr, bb_scr, *, rows):
    b = pl.program_id(0)
    t = pl.program_id(1)
    w2 = 2 * WINDOW

    @pl.when(jnp.logical_and(b == 0, t == 0))
    def _build_bias():
        qi = lax.broadcasted_iota(jnp.int32, (WINDOW, w2), 0)
        kj = lax.broadcasted_iota(jnp.int32, (WINDOW, w2), 1)
        dist = qi + WINDOW - kj

        def per_head(hq, carry):
            acc = jnp.full((WINDOW, w2), MASKED, F32)
            for bk, rng in enumerate(_BUCKET_RANGES):
                if rng is not None:
                    hit = jnp.logical_and(dist >= rng[0], dist <= rng[1])
                    acc = jnp.where(hit, relb_ref[bk, hq], acc)
            bias_scr[0, hq] = acc
            bias_scr[1, hq] = jnp.where(kj < WINDOW, MASKED, acc)
            return carry

        lax.fori_loop(0, SWA_QH, per_head, 0)

    @pl.when(t == 0)
    def _reset_window():
        k_scr[0:WINDOW, :] = jnp.zeros((WINDOW, SWA_KV), BF16)
        v_scr[0:WINDOW, :] = jnp.zeros((WINDOW, SWA_KV), BF16)

    qkv_w = SWA_Q + 2 * SWA_KV
    n_blocks = rows // WINDOW
    n_gate = D_MODEL // GATE_CHUNK
    heads = [(g, j) for g in range(SWA_KVH) for j in range(SWA_GROUP)]
    xb, qkv, mq, scores, mem_s = {}, {}, {}, {}, {}

    def project(qb):
        r0 = qb * WINDOW
        xb[qb] = h_ref[0, r0:r0 + WINDOW, :].astype(BF16)
        qkv[qb] = jnp.dot(xb[qb], win_ref[:, :qkv_w], preferred_element_type=F32)
        mq[qb] = jnp.dot(xb[qb], win_ref[:, qkv_w:qkv_w + MEM_Q], preferred_element_type=F32)
        k = qkv[qb][:, SWA_Q:SWA_Q + SWA_KV]
        v = qkv[qb][:, SWA_Q + SWA_KV:]
        if qb == n_blocks - 1:
            kout_ref[0] = k
            vout_ref[0] = v
        k_scr[WINDOW + r0:w2 + r0, :] = k.astype(BF16)
        v_scr[WINDOW + r0:w2 + r0, :] = v.astype(BF16)

    def score(qb):
        r0 = qb * WINDOW
        table = jnp.where(t == 0, 1, 0) if qb == 0 else 0
        scores[qb] = []
        for idx, (g, j) in enumerate(heads):
            hq = g * SWA_GROUP + j
            q = qkv[qb][:, hq * SWA_HD:(hq + 1) * SWA_HD] * (SWA_HD ** -0.5)
            scores[qb].append(_dot_nt(q, k_scr[r0:r0 + w2, g * SWA_HD:(g + 1) * SWA_HD]) + bias_scr[table, hq])
            for chunk in range(idx * n_gate // len(heads), (idx + 1) * n_gate // len(heads)):
                lo = chunk * GATE_CHUNK
                gate = jnp.dot(xb[qb], win_ref[:, qkv_w + MEM_Q + lo:qkv_w + MEM_Q + lo + GATE_CHUNK],
                               preferred_element_type=F32)
                gs_scr[r0:r0 + WINDOW, lo:lo + GATE_CHUNK] = _silu(gate)
        mem_s[qb] = _mem_scores(mq[qb], mk_ref)

    def attend(qb):
        r0 = qb * WINDOW
        bb = bb_scr.at[r0:r0 + WINDOW, :]
        gs = gs_scr.at[r0:r0 + WINDOW, :]
        sinks = [sink_ref[g * SWA_GROUP + j] for g, j in heads]
        maxes = [jnp.maximum(jnp.max(s, axis=-1, keepdims=True), sk) for s, sk in zip(scores[qb], sinks)]
        mem_max = [jnp.max(s, axis=-1, keepdims=True) for s in mem_s[qb]]
        probs = [jnp.exp(s - m) for s, m in zip(scores[qb], maxes)]
        mem_p = [jnp.exp(s - m) for s, m in zip(mem_s[qb], mem_max)]
        weights = [(p, jnp.sum(p, axis=-1, keepdims=True) + jnp.exp(sk - m))
                   for p, m, sk in zip(probs, maxes, sinks)]
        mem_w = [(p, jnp.sum(p, axis=-1, keepdims=True)) for p in mem_p]
        for (g, j), (p, l) in zip(heads, weights):
            o = _dot(p, v_scr[r0:r0 + w2, g * SWA_HD:(g + 1) * SWA_HD]) / l
            _put_gated(bb, gs, (g * SWA_GROUP + j) * SWA_HD, o)
        _mem_values(mem_w, mv_ref, bb, gs, SWA_Q)

    def finish(qb):
        r0 = qb * WINDOW
        y = jnp.dot(bb_scr[r0:r0 + WINDOW, :], wout_ref[...], preferred_element_type=F32)
        out_ref[0, r0:r0 + WINDOW, :] = _layer_norm(DEEPNORM_ALPHA * h_ref[0, r0:r0 + WINDOW, :] + y,
                                                    lnw_ref[...], lnb_ref[...])

    project(0)
    score(0)
    for qb in range(n_blocks):
        if qb + 1 < n_blocks:
            project(qb + 1)
            score(qb + 1)
        attend(qb)
        finish(qb)

    k_scr[0:WINDOW, :] = k_scr[rows:rows + WINDOW, :]
    v_scr[0:WINDOW, :] = v_scr[rows:rows + WINDOW, :]


def _swa_prompt_layer(h, w_in, w_out, mk_bf, mv_bf, layer, rel_bias, sinks, ln_w, ln_b, rows):
    bsz, t, d = h.shape
    kernel = functools.partial(_swa_prompt_kernel, rows=rows)
    mem_spec = pl.BlockSpec((None, N_MEM, MEM_Q), lambda b, i: (layer, b, 0))
    return pl.pallas_call(
        kernel,
        grid=(bsz, t // rows),
        in_specs=[_SMEM, _SMEM,
                  pl.BlockSpec((1, rows, d), lambda b, i: (b, i, 0)),
                  _resident((d, SWA_WIDTH)),
                  _resident((d, d)),
                  mem_spec, mem_spec,
                  _resident((1, d)), _resident((1, d))],
        out_specs=[pl.BlockSpec((1, rows, d), lambda b, i: (b, i, 0)),
                   pl.BlockSpec((1, WINDOW, SWA_KV), lambda b, i: (b, 0, 0)),
                   pl.BlockSpec((1, WINDOW, SWA_KV), lambda b, i: (b, 0, 0))],
        out_shape=[jax.ShapeDtypeStruct((bsz, t, d), F32),
                   jax.ShapeDtypeStruct((bsz, WINDOW, SWA_KV), F32),
                   jax.ShapeDtypeStruct((bsz, WINDOW, SWA_KV), F32)],
        scratch_shapes=[pltpu.VMEM((2, SWA_QH, WINDOW, 2 * WINDOW), F32),
                        pltpu.VMEM((WINDOW + rows, SWA_KV), BF16),
                        pltpu.VMEM((WINDOW + rows, SWA_KV), BF16),
                        pltpu.VMEM((rows, d), F32),
                        pltpu.VMEM((rows, d), BF16)],
        compiler_params=_cparams(2),
        name="swa_prompt_layer",
    )(rel_bias, sinks, h, w_in, w_out, mk_bf, mv_bf, ln_w.reshape(1, d), ln_b.reshape(1, d))


HG_CHUNK = 128
HG_DIRECT = 8
HG_HEAD_COLS = 3 * HG_D


def _hgrn_head_major(w):
    d = w.shape[0]
    qfi = w[:, :3 * HG_K].reshape(d, 3, HG_HEADS, HG_D).transpose(0, 2, 1, 3).reshape(d, 3 * HG_K)
    return jnp.concatenate([qfi, w[:, 3 * HG_K:]], axis=1)


def _lower_bound(logits, layer):
    e = jnp.exp(logits - jnp.max(logits, axis=0, keepdims=True))
    sm = e / jnp.sum(e, axis=0, keepdims=True)
    return jnp.sum(sm[1:layer + 1, :], axis=0, keepdims=True)


def _cumsum_rows(tril_bf, g):
    g1 = g.astype(BF16)
    r1 = g - g1.astype(F32)
    g2 = r1.astype(BF16)
    g3 = (r1 - g2.astype(F32)).astype(BF16)
    d = jnp.dot(tril_bf, jnp.concatenate([g1, g2, g3], axis=1), preferred_element_type=F32)
    return (d[:, :HG_D] + d[:, HG_D:2 * HG_D]) + d[:, 2 * HG_D:]


def _hgrn_chunk_scores(qs, kk, gc, gk_ref):
    c = HG_CHUNK
    nb = c // HG_DIRECT
    q3 = qs.reshape(nb, HG_DIRECT, HG_D)
    g3 = gc.reshape(nb, HG_DIRECT, HG_D)
    gk_ref[0] = gc
    gk_ref[1] = kk

    def key_row(which, sl):
        return jnp.concatenate(
            [jnp.broadcast_to(gk_ref[which, HG_DIRECT * b + sl:HG_DIRECT * b + sl + 1, :], (HG_DIRECT, HG_D))
             for b in range(nb)], axis=0).reshape(nb, HG_DIRECT, HG_D)

    lane = lax.broadcasted_iota(jnp.int32, (nb, HG_DIRECT, c), 2)
    blk = lax.broadcasted_iota(jnp.int32, (nb, HG_DIRECT, c), 0)
    trow = lax.broadcasted_iota(jnp.int32, (nb, HG_DIRECT, c), 1)
    a3 = jnp.zeros((nb, HG_DIRECT, c), F32)
    for sl in range(HG_DIRECT):
        e = jnp.exp(g3 - key_row(0, sl))
        col = jnp.sum(e * q3 * key_row(1, sl), axis=-1, keepdims=True)
        hit = jnp.logical_and(lane == blk * HG_DIRECT + sl, trow >= sl)
        a3 = jnp.where(hit, col, a3)
    a = a3.reshape(c, c)

    ti = lax.broadcasted_iota(jnp.int32, (c, c), 0)
    si = lax.broadcasted_iota(jnp.int32, (c, c), 1)
    ri = lax.broadcasted_iota(jnp.int32, (c, 1), 0)
    half = HG_DIRECT
    while half < c:
        span = 2 * half
        mid = [(r0 // span) * span + half - 1 for r0 in range(0, c, V7X_SUBLANES)]
        gref = jnp.concatenate([jnp.broadcast_to(gk_ref[0, m:m + 1, :], (V7X_SUBLANES, HG_D)) for m in mid], axis=0)
        upper = (ri % span) >= half
        eq = jnp.where(upper, jnp.exp(gc - gref), 0.0)
        ek = jnp.where(upper, 0.0, jnp.exp(gref - gc))
        a = a + jnp.where((ti // span) == (si // span), _dot_nt(qs * eq, kk * ek), 0.0)
        half = span
    return a


def _hgrn_prompt_kernel(h_ref, win_ref, wout_ref, lbl_ref, nw_ref, mk_ref, mv_ref, lnw_ref, lnb_ref,
                        out_ref, st_ref,
                        st_scr, gs_scr, bb_scr, gk_scr, *, layer, rows):
    t = pl.program_id(1)
    c = HG_CHUNK
    n_blocks = rows // c
    n_gate = D_MODEL // GATE_CHUNK
    n_pairs = HG_HEADS // 2
    heads = range(HG_HEADS)

    @pl.when(t == 0)
    def _reset_state():
        st_scr[...] = jnp.zeros(st_scr.shape, F32)

    lb_all = _lower_bound(lbl_ref[...], layer)
    ri = lax.broadcasted_iota(jnp.int32, (c, c), 0)
    ci = lax.broadcasted_iota(jnp.int32, (c, c), 1)
    tril_bf = jnp.where(ri >= ci, 1.0, 0.0).astype(BF16)

    xb, qs, kk, iv, gc, a, mem_s, y_cols = ({} for _ in range(8))

    def rows_of(blk):
        return slice(blk * c, (blk + 1) * c)

    def load(blk):
        xb[blk] = h_ref[0, rows_of(blk), :].astype(BF16)

    def project_pair(blk, pair):
        up = jnp.dot(xb[blk], win_ref[:, 2 * pair * HG_HEAD_COLS:2 * (pair + 1) * HG_HEAD_COLS],
                     preferred_element_type=F32)
        for hh in (2 * pair, 2 * pair + 1):
            uh = up[:, (hh - 2 * pair) * HG_HEAD_COLS:(hh - 2 * pair + 1) * HG_HEAD_COLS]
            lb = lb_all[:, hh * HG_D:(hh + 1) * HG_D]
            fg = lb + (1.0 - lb) * jax.nn.sigmoid(uh[:, HG_D:2 * HG_D])
            qs[blk, hh] = _silu(uh[:, :HG_D])
            iv[blk, hh] = uh[:, 2 * HG_D:]
            kk[blk, hh] = 1.0 - fg
            gc[blk, hh] = _cumsum_rows(tril_bf, jnp.log(fg))

    def project_mem(blk):
        mq = jnp.dot(xb[blk], win_ref[:, 3 * HG_K:3 * HG_K + MEM_Q], preferred_element_type=F32)
        mem_s[blk] = _mem_scores(mq, mk_ref)

    def project_gate(blk, chunk):
        lo = chunk * GATE_CHUNK
        gate = jnp.dot(xb[blk], win_ref[:, 3 * HG_K + MEM_Q + lo:3 * HG_K + MEM_Q + lo + GATE_CHUNK],
                       preferred_element_type=F32)
        gs_scr[rows_of(blk), lo:lo + GATE_CHUNK] = _silu(gate)

    def scores(blk, fillers):
        for hh in heads:
            a[blk, hh] = _hgrn_chunk_scores(qs[blk, hh], kk[blk, hh], gc[blk, hh], gk_scr.at[hh])
            for job in fillers[hh * len(fillers) // HG_HEADS:(hh + 1) * len(fillers) // HG_HEADS]:
                job()

    def recur(blk):
        bb = bb_scr.at[rows_of(blk), :]
        gs = gs_scr.at[rows_of(blk), :]
        iv_t = [iv[blk, hh].T for hh in heads]
        for hh in heads:
            lhs = jnp.concatenate([a[blk, hh], qs[blk, hh] * jnp.exp(gc[blk, hh])], axis=1)
            rhs = jnp.concatenate([iv_t[hh], st_scr[hh]], axis=1)
            o = _dot_nt(lhs, rhs)
            ms = jnp.mean(o * o, axis=-1, keepdims=True)
            _put_gated(bb, gs, hh * HG_D, o * lax.rsqrt(ms + RMS_EPS) * nw_ref[:, hh * HG_D:(hh + 1) * HG_D])
        for hh in heads:
            g_last = gc[blk, hh][c - 1:c, :]
            kdec = kk[blk, hh] * jnp.exp(g_last - gc[blk, hh])
            st_scr[hh] = st_scr[hh] * jnp.exp(g_last) + _dot(iv_t[hh], kdec)
        mem_p = [jnp.exp(s - jnp.max(s, axis=-1, keepdims=True)) for s in mem_s[blk]]
        _mem_values([(p, jnp.sum(p, axis=-1, keepdims=True)) for p in mem_p], mv_ref, bb, gs, HG_K)

    def project_out(blk, chunk):
        lo = chunk * GATE_CHUNK
        y_cols[blk, chunk] = jnp.dot(bb_scr[rows_of(blk), :], wout_ref[:, lo:lo + GATE_CHUNK],
                                     preferred_element_type=F32)

    def finish(blk):
        y = jnp.concatenate([y_cols[blk, chunk] for chunk in range(n_gate)], axis=1)
        out_ref[0, rows_of(blk), :] = _layer_norm(DEEPNORM_ALPHA * h_ref[0, rows_of(blk), :] + y,
                                                  lnw_ref[...], lnb_ref[...])

    load(0)
    for pair in range(n_pairs):
        project_pair(0, pair)
    project_mem(0)
    for blk in range(n_blocks):
        fillers = []
        if blk + 1 < n_blocks:
            load(blk + 1)
            fillers += [functools.partial(project_pair, blk + 1, pair) for pair in range(n_pairs)]
            fillers.append(functools.partial(project_mem, blk + 1))
        if blk > 0:
            fillers += [functools.partial(project_out, blk - 1, chunk) for chunk in range(n_gate)]
        fillers += [functools.partial(project_gate, blk, chunk) for chunk in range(n_gate)]
        scores(blk, fillers)
        if blk > 0:
            finish(blk - 1)
        recur(blk)
    last = n_blocks - 1
    for chunk in range(n_gate):
        project_out(last, chunk)
    finish(last)

    @pl.when(t == pl.num_programs(1) - 1)
    def _emit_state():
        for hh in heads:
            st_ref[0, hh] = st_scr[hh].T


def _hgrn_prompt_layer(h, w_in_hm, w_out, mk_bf, mv_bf, layer, lb_logits, norm_w, ln_w, ln_b, rows):
    bsz, t, d = h.shape
    kernel = functools.partial(_hgrn_prompt_kernel, layer=layer, rows=rows)
    mem_spec = pl.BlockSpec((None, N_MEM, MEM_Q), lambda b, i: (layer, b, 0))
    return pl.pallas_call(
        kernel,
        grid=(bsz, t // rows),
        in_specs=[pl.BlockSpec((1, rows, d), lambda b, i: (b, i, 0)),
                  _resident((d, HG_WIDTH)),
                  _resident((d, d)),
                  _resident((DEPTH, HG_K)),
                  _resident((1, HG_K)),
                  mem_spec, mem_spec,
                  _resident((1, d)), _resident((1, d))],
        out_specs=[pl.BlockSpec((1, rows, d), lambda b, i: (b, i, 0)),
                   pl.BlockSpec((1, HG_HEADS, HG_D, HG_D), lambda b, i: (b, 0, 0, 0))],
        out_shape=[jax.ShapeDtypeStruct((bsz, t, d), F32),
                   jax.ShapeDtypeStruct((bsz, HG_HEADS, HG_D, HG_D), F32)],
        scratch_shapes=[pltpu.VMEM((HG_HEADS, HG_D, HG_D), F32),
                        pltpu.VMEM((rows, d), F32),
                        pltpu.VMEM((rows, d), BF16),
                        pltpu.VMEM((HG_HEADS, 2, HG_CHUNK, HG_D), F32)],
        compiler_params=_cparams(2),
        name="hgrn_prompt_layer",
    )(h, w_in_hm, w_out, lb_logits, norm_w.reshape(1, HG_K), mk_bf, mv_bf, ln_w.reshape(1, d), ln_b.reshape(1, d))


REQS_PER_STEP = 4


def _heads_to_sublanes(row, col0, n, width):
    return jnp.concatenate([row[:, col0 + i * width:col0 + (i + 1) * width] for i in range(n)], axis=0)


def _sample_mem_attend(urow, col0, cmk_ref, cmv_ref, r):
    pieces = []
    for hm in range(MEM_HEADS):
        lo = col0 + hm * MEM_HD
        rows = pl.ds(hm, N_MEM, stride=MEM_HEADS)
        s = jnp.sum(cmk_ref[r, rows, :] * urow[:, lo:lo + MEM_HD], axis=-1, keepdims=True) * (MEM_HD ** -0.5)
        s = jnp.broadcast_to(s, (N_MEM, MEM_HD))
        p = jnp.exp(s - jnp.max(s, axis=0, keepdims=True))
        pieces.append(jnp.sum(p * cmv_ref[r, rows, :], axis=0, keepdims=True) / jnp.sum(p, axis=0, keepdims=True))
    return jnp.concatenate(pieces, axis=1)


def _sample_swa_kernel(relb_ref, sink_ref, u_ref, ck_ref, cv_ref, cmk_ref, cmv_ref,
                       br_ref, ko_ref, vo_ref, bias_scr, bias0_scr, sink_scr):
    nbuf = WINDOW

    @pl.when(pl.program_id(0) == 0)
    def _build_tables():
        dist = nbuf - lax.broadcasted_iota(jnp.int32, (V7X_SUBLANES, nbuf), 1)
        rid = lax.broadcasted_iota(jnp.int32, (V7X_SUBLANES, nbuf), 0)
        for g in range(SWA_KVH):
            acc = jnp.where(rid < SWA_GROUP, MASKED, 0.0).astype(F32)
            acc0 = jnp.zeros((V7X_SUBLANES, nbuf), F32)
            sk = jnp.zeros((V7X_SUBLANES, nbuf), F32)
            for j in range(SWA_GROUP):
                hq = g * SWA_GROUP + j
                sk = jnp.where(rid == j, sink_ref[hq], sk)
                acc0 = jnp.where(rid == j, relb_ref[0, hq], acc0)
                for bk, rng in enumerate(_BUCKET_RANGES):
                    if rng is not None:
                        hit = jnp.logical_and(rid == j, jnp.logical_and(dist >= rng[0], dist <= rng[1]))
                        acc = jnp.where(hit, relb_ref[bk, hq], acc)
            bias_scr[g] = acc
            bias0_scr[g] = acc0
            sink_scr[g] = sk

    g0 = SWA_Q + 2 * SWA_KV + MEM_Q
    pad = jnp.zeros((V7X_SUBLANES - SWA_GROUP, SWA_HD), F32)
    pairs = [(r, g) for r in range(REQS_PER_STEP) for g in range(SWA_KVH)]
    urows = [u_ref[r] for r in range(REQS_PER_STEP)]

    scores = []
    for r, g in pairs:
        kg = ck_ref[r, pl.ds(g, nbuf, stride=SWA_KVH), :]
        k_new = urows[r][:, SWA_Q + g * SWA_HD:SWA_Q + (g + 1) * SWA_HD]
        v_new = urows[r][:, SWA_Q + SWA_KV + g * SWA_HD:SWA_Q + SWA_KV + (g + 1) * SWA_HD]
        older = pl.ds(SWA_KVH + g, nbuf - 1, stride=SWA_KVH)
        kept = pl.ds(g, nbuf - 1, stride=SWA_KVH)
        ko_ref[r, kept, :] = ck_ref[r, older, :]
        vo_ref[r, kept, :] = cv_ref[r, older, :]
        last = (nbuf - 1) * SWA_KVH + g
        ko_ref[r, last:last + 1, :] = k_new
        vo_ref[r, last:last + 1, :] = v_new
        qg = jnp.concatenate(
            [urows[r][:, (g * SWA_GROUP + j) * SWA_HD:(g * SWA_GROUP + j + 1) * SWA_HD] for j in range(SWA_GROUP)]
            + [pad], axis=0) * (SWA_HD ** -0.5)
        s = _dot_nt(qg, kg) + bias_scr[g]
        s_new = jnp.sum(qg * k_new, axis=-1, keepdims=True) + bias0_scr[g][:, 0:1]
        scores.append((s, s_new, v_new))

    weights = []
    for (r, g), (s, s_new, v_new) in zip(pairs, scores):
        sink = sink_scr[g][:, 0:1]
        m = jnp.maximum(jnp.max(s, axis=-1, keepdims=True), jnp.maximum(s_new, sink))
        p = jnp.exp(s - m)
        p_new = jnp.exp(s_new - m)
        l = jnp.sum(p, axis=-1, keepdims=True) + p_new + jnp.exp(sink - m)
        weights.append((p, p_new * v_new, l))

    pieces = [[] for _ in range(REQS_PER_STEP)]
    for (r, g), (p, o_new, l) in zip(pairs, weights):
        o = (_dot(p, cv_ref[r, pl.ds(g, nbuf, stride=SWA_KVH), :]) + o_new) / l
        pieces[r].extend(o[j:j + 1, :] for j in range(SWA_GROUP))

    for r in range(REQS_PER_STEP):
        mem_o = _sample_mem_attend(urows[r], SWA_Q + 2 * SWA_KV, cmk_ref, cmv_ref, r)
        br_ref[r] = jnp.concatenate(pieces[r] + [mem_o], axis=1) * _silu(urows[r][:, g0:g0 + D_MODEL])


def _cache_spec(layer, shape):
    return pl.BlockSpec((None, REQS_PER_STEP) + shape, lambda i: (layer, i) + (0,) * len(shape))


def _row_spec(width):
    return pl.BlockSpec((REQS_PER_STEP, 1, width), lambda i: (i, 0, 0))


def _head_minor_rows(cache):
    nl, ns, rows, heads, hd = cache.shape
    return cache.reshape(nl, ns, rows * heads, hd)


def _sample_swa_attend(u, cache_k, cache_v, j, cache_mk, cache_mv, layer, rel_bias, sinks):
    ns = u.shape[0]
    nbuf = cache_k.shape[2] // SWA_KVH
    swa_shape = (nbuf * SWA_KVH, SWA_HD)
    mem_shape = (N_MEM * MEM_HEADS, MEM_HD)
    new_cache = pl.BlockSpec((REQS_PER_STEP,) + swa_shape, lambda i: (i, 0, 0))
    br, ko, vo = pl.pallas_call(
        _sample_swa_kernel,
        grid=(ns // REQS_PER_STEP,),
        in_specs=[_SMEM, _SMEM, _row_spec(SWA_WIDTH), _cache_spec(j, swa_shape), _cache_spec(j, swa_shape),
                  _cache_spec(layer, mem_shape), _cache_spec(layer, mem_shape)],
        out_specs=[_row_spec(D_MODEL), new_cache, new_cache],
        out_shape=[jax.ShapeDtypeStruct((ns, 1, D_MODEL), F32),
                   jax.ShapeDtypeStruct((ns,) + swa_shape, F32),
                   jax.ShapeDtypeStruct((ns,) + swa_shape, F32)],
        scratch_shapes=[pltpu.VMEM((SWA_KVH, V7X_SUBLANES, nbuf), F32)] * 3,
        compiler_params=_cparams(1),
        name="sample_swa_attend",
    )(rel_bias, sinks, u.reshape(ns, 1, SWA_WIDTH), cache_k, cache_v, cache_mk, cache_mv)
    return br.reshape(ns, D_MODEL), ko, vo


def _sample_hgrn_gate_kernel(u_ref, mix_ref, cmk_ref, cmv_ref, br_ref):
    g0 = 3 * HG_K + MEM_Q
    for r in range(REQS_PER_STEP):
        urow = u_ref[r]
        mem_o = _sample_mem_attend(urow, 3 * HG_K, cmk_ref, cmv_ref, r)
        br_ref[r] = jnp.concatenate([mix_ref[r], mem_o], axis=1) * _silu(urow[:, g0:g0 + D_MODEL])


def _sample_hgrn_gate(u, mix, cache_mk, cache_mv, layer):
    ns = u.shape[0]
    mem_shape = (N_MEM * MEM_HEADS, MEM_HD)
    br = pl.pallas_call(
        _sample_hgrn_gate_kernel,
        grid=(ns // REQS_PER_STEP,),
        in_specs=[_row_spec(HG_WIDTH), _row_spec(HG_K), _cache_spec(layer, mem_shape), _cache_spec(layer, mem_shape)],
        out_specs=_row_spec(D_MODEL),
        out_shape=jax.ShapeDtypeStruct((ns, 1, D_MODEL), F32),
        compiler_params=_cparams(1),
        name="sample_hgrn_gate",
    )(u.reshape(ns, 1, HG_WIDTH), mix.reshape(ns, 1, HG_K), cache_mk, cache_mv)
    return br.reshape(ns, D_MODEL)


def _hgrn_sample_kernel(q_ref, f_ref, iv_ref, lbl_ref, nw_ref, st_ref, mix_ref, so_ref, o_scr, *, layer, ns):
    lb = _lower_bound(lbl_ref[...], layer)
    fg = lb + (1.0 - lb) * jax.nn.sigmoid(f_ref[...])
    kk_t = (1.0 - fg).T
    fg_t = fg.T
    qs_t = _silu(q_ref[...]).T
    for r in range(ns):
        state = fg_t[:, r:r + 1] * st_ref[r] + kk_t[:, r:r + 1] * iv_ref[r:r + 1, :]
        so_ref[r] = state
        o_scr[r:r + 1, :] = jnp.sum(qs_t[:, r:r + 1] * state, axis=0, keepdims=True)
    o = o_scr[...]
    ms = jnp.mean(o * o, axis=-1, keepdims=True)
    mix_ref[...] = o * lax.rsqrt(ms + RMS_EPS) * nw_ref[...]


def _hgrn_sample_step(u_hm, state, j, lb_logits, norm_w, layer):
    ns = u_hm.shape[0]
    col = lambda off: pl.BlockSpec((ns, HG_D), lambda h: (0, 3 * h + off))
    kernel = functools.partial(_hgrn_sample_kernel, layer=layer, ns=ns)
    return pl.pallas_call(
        kernel,
        grid=(HG_HEADS,),
        in_specs=[col(0), col(1), col(2),
                  pl.BlockSpec((DEPTH, HG_D), lambda h: (0, h)),
                  pl.BlockSpec((1, HG_D), lambda h: (0, h)),
                  pl.BlockSpec((None, ns, None, HG_D, HG_D), lambda h: (j, 0, h, 0, 0))],
        out_specs=[pl.BlockSpec((ns, HG_D), lambda h: (0, h)),
                   pl.BlockSpec((ns, None, HG_D, HG_D), lambda h: (0, h, 0, 0))],
        out_shape=[jax.ShapeDtypeStruct((ns, HG_K), F32),
                   jax.ShapeDtypeStruct(state.shape[1:], F32)],
        scratch_shapes=[pltpu.VMEM((ns, HG_D), F32)],
        compiler_params=_cparams(1),
        name="hgrn_sample_step",
    )(u_hm, u_hm, u_hm, lb_logits, norm_w.reshape(1, HG_K), state)


def _outproj_ln_kernel(br_ref, h_ref, wout_ref, lnw_ref, lnb_ref, out_ref):
    y = jnp.dot(br_ref[...].astype(BF16), wout_ref[...], preferred_element_type=F32)
    out_ref[...] = _layer_norm(DEEPNORM_ALPHA * h_ref[...] + y, lnw_ref[...], lnb_ref[...])


def _outproj_ln(branch, h, w_out, ln_w, ln_b):
    ns, d = h.shape
    full = lambda shape: pl.BlockSpec(shape, lambda i: (0, 0))
    return pl.pallas_call(
        _outproj_ln_kernel,
        grid=(1,),
        in_specs=[full((ns, d)), full((ns, d)), full((d, d)), full((1, d)), full((1, d))],
        out_specs=full((ns, d)),
        out_shape=jax.ShapeDtypeStruct((ns, d), F32),
        compiler_params=_cparams(1),
        name="outproj_ln",
    )(branch, h, w_out, ln_w.reshape(1, d), ln_b.reshape(1, d))


def kernel(x_prompt, x_sample, cache_mem_k, cache_mem_v, cache_swa_k, cache_swa_v, state_hgrn, mem_prompt, rel_bias, swa_w_in, swa_sinks, hg_w_in, hg_lb_logits, hg_norm_w, w_mem_k, w_mem_v, w_out, ln_w, ln_b):
    bp, t, d = x_prompt.shape
    ns = x_sample.shape[0]
    assert d == D_MODEL and x_sample.shape[1] == 1 and t % PROMPT_ROWS == 0
    assert cache_swa_k.shape[2] == WINDOW <= PAST_LEN
    assert w_mem_k.shape[0] == DEPTH

    wo = w_out.astype(BF16)
    mem_k_prompt, mem_v_prompt, mk_bf, mv_bf = _mem_kv_proj(mem_prompt, w_mem_k, w_mem_v)
    cmk, cmv = _head_minor_rows(cache_mem_k), _head_minor_rows(cache_mem_v)
    csk, csv = _head_minor_rows(cache_swa_k), _head_minor_rows(cache_swa_v)

    hp = x_prompt
    hs = x_sample.reshape(ns, d)
    swa_kp, swa_vp, swa_ks, swa_vs, hg_sp, hg_ss = [], [], [], [], [], []
    for i in range(DEPTH):
        j = i // 2
        if i % 2 == 0:
            w_in = swa_w_in[j].astype(BF16)
            hp, kw, vw = _swa_prompt_layer(hp, w_in, wo[i], mk_bf, mv_bf, i, rel_bias, swa_sinks[j],
                                           ln_w[i], ln_b[i], rows=PROMPT_ROWS)
            swa_kp.append(kw.reshape(bp, WINDOW, SWA_KVH, SWA_HD))
            swa_vp.append(vw.reshape(bp, WINDOW, SWA_KVH, SWA_HD))
            u = _proj(hs, w_in, "sample_swa_proj")
            br, ko, vo = _sample_swa_attend(u, csk, csv, j, cmk, cmv, i, rel_bias, swa_sinks[j])
            swa_ks.append(ko.reshape(ns, WINDOW, SWA_KVH, SWA_HD))
            swa_vs.append(vo.reshape(ns, WINDOW, SWA_KVH, SWA_HD))
        else:
            w_in = _hgrn_head_major(hg_w_in[j]).astype(BF16)
            hp, st = _hgrn_prompt_layer(hp, w_in, wo[i], mk_bf, mv_bf, i, hg_lb_logits, hg_norm_w[j],
                                        ln_w[i], ln_b[i], rows=HG_ROWS)
            hg_sp.append(st)
            u = _proj(hs, w_in, "sample_hgrn_proj")
            mix, so = _hgrn_sample_step(u, state_hgrn, j, hg_lb_logits, hg_norm_w[j], i)
            hg_ss.append(so)
            br = _sample_hgrn_gate(u, mix, cmk, cmv, i)
        hs = _outproj_ln(br, hs, wo[i], ln_w[i], ln_b[i])

    return (hp, hs.reshape(ns, 1, d), mem_k_prompt, mem_v_prompt,
            jnp.stack(swa_kp), jnp.stack(swa_vp), jnp.stack(hg_sp),
            jnp.stack(swa_ks), jnp.stack(swa_vs), jnp.stack(hg_ss))
```

```python
import functools
import math

import numpy as np
import jax
import jax.numpy as jnp
from jax import lax
from jax.experimental import pallas as pl
from jax.experimental.pallas import tpu as pltpu

F32 = jnp.float32
BF16 = jnp.bfloat16

D_MODEL = 2048
DEPTH = 2
PAST_LEN = 8192
N_MEM = 256
MEM_HEADS = 4
MEM_HD = 128
MEM_Q = MEM_HEADS * MEM_HD
WINDOW = 128
SWA_HD = 64
SWA_KVH = 4
SWA_GROUP = 6
SWA_QH = SWA_KVH * SWA_GROUP
SWA_Q = SWA_QH * SWA_HD
SWA_KV = SWA_KVH * SWA_HD
N_BUCKETS = 32
HG_HEADS = 12
HG_D = 128
HG_K = HG_HEADS * HG_D
SWA_WIDTH = SWA_Q + 2 * SWA_KV + MEM_Q + D_MODEL
HG_WIDTH = 3 * HG_K + MEM_Q + D_MODEL
DEEPNORM_ALPHA = (2.0 * DEPTH) ** 0.25
LN_EPS = 1e-5
RMS_EPS = 1e-6
MASKED = -1e30

V7X_SUBLANES = 8
V7X_LANES = 128
V7X_MXU_COLS = 256
V7X_VMEM_LIMIT = 60000 * 1024
PROJ_COLS = 512
GATE_CHUNK = 512
PROMPT_ROWS = 256
HG_ROWS = 128


def _t5_bucket_ranges():
    n = np.arange(WINDOW)
    max_exact = N_BUCKETS // 2
    nf = np.maximum(n, 1).astype(np.float64)
    large = max_exact + (np.log(nf / max_exact) / math.log(WINDOW / max_exact)
                         * (N_BUCKETS - max_exact)).astype(np.int64)
    bucket = np.where(n < max_exact, n, np.minimum(large, N_BUCKETS - 1))
    ranges = []
    for b in range(N_BUCKETS):
        idx = np.nonzero(bucket == b)[0]
        ranges.append((int(idx[0]), int(idx[-1])) if idx.size else None)
    return ranges


_BUCKET_RANGES = _t5_bucket_ranges()


def _dot(a, b):
    return jnp.dot(a.astype(BF16), b.astype(BF16), preferred_element_type=F32)


def _dot_nt(a, b):
    return lax.dot_general(a.astype(BF16), b.astype(BF16), (((1,), (1,)), ((), ())),
                           preferred_element_type=F32)


def _layer_norm(z, w, b):
    mu = jnp.mean(z, axis=-1, keepdims=True)
    zc = z - mu
    var = jnp.mean(zc * zc, axis=-1, keepdims=True)
    return zc * lax.rsqrt(var + LN_EPS) * w + b


def _silu(x):
    return x * jax.nn.sigmoid(x)


def _cparams(n_axes):
    return pltpu.CompilerParams(dimension_semantics=("arbitrary",) * n_axes,
                                vmem_limit_bytes=V7X_VMEM_LIMIT)


def _resident(shape):
    nd = len(shape)
    return pl.BlockSpec(shape, lambda *_: (0,) * nd, pipeline_mode=pl.Buffered(1))


_SMEM = pl.BlockSpec(memory_space=pltpu.SMEM)


def _mem_kv_kernel(x_ref, wk_ref, wv_ref, ko_ref, vo_ref, kb_ref, vb_ref):
    xb = x_ref[...].astype(BF16)
    for w_ref, o_ref, b_ref in ((wk_ref, ko_ref, kb_ref), (wv_ref, vo_ref, vb_ref)):
        y = jnp.dot(xb, w_ref[...].astype(BF16), preferred_element_type=F32)
        b_ref[...] = y.astype(BF16)
        for hm in range(MEM_HEADS):
            o_ref[:, hm, :] = y[:, hm * MEM_HD:(hm + 1) * MEM_HD]


def _mem_kv_proj(mem, w_k, w_v):
    bsz, n, d = mem.shape
    nl = w_k.shape[0]
    w_spec = pl.BlockSpec((None, d, MEM_Q), lambda l, b: (l, 0, 0))
    o_spec = pl.BlockSpec((None, None, n, MEM_HEADS, MEM_HD), lambda l, b: (l, b, 0, 0, 0))
    b_spec = pl.BlockSpec((None, n, MEM_Q), lambda l, b: (l, b, 0))
    o_shape = jax.ShapeDtypeStruct((nl, bsz, n, MEM_HEADS, MEM_HD), F32)
    b_shape = jax.ShapeDtypeStruct((nl, bsz * n, MEM_Q), BF16)
    return pl.pallas_call(
        _mem_kv_kernel,
        grid=(nl, bsz),
        in_specs=[pl.BlockSpec((None, n, d), lambda l, b: (b, 0, 0)), w_spec, w_spec],
        out_specs=[o_spec, o_spec, b_spec, b_spec],
        out_shape=[o_shape, o_shape, b_shape, b_shape],
        compiler_params=_cparams(2),
        name="mem_kv_proj",
    )(mem, w_k, w_v)


def _proj_kernel(x_ref, w_ref, o_ref):
    o_ref[...] = jnp.dot(x_ref[...].astype(BF16), w_ref[...], preferred_element_type=F32)


def _pad_cols(w):
    if w.shape[-1] % (4 * V7X_MXU_COLS) != 0:
        return w
    return jnp.pad(w, [(0, 0)] * (w.ndim - 1) + [(0, V7X_MXU_COLS)])


def _proj(x, w, n, name):
    m, k = x.shape
    return pl.pallas_call(
        _proj_kernel,
        grid=(n // PROJ_COLS,),
        in_specs=[pl.BlockSpec((m, k), lambda j: (0, 0)),
                  pl.BlockSpec((k, PROJ_COLS), lambda j: (0, j))],
        out_specs=pl.BlockSpec((m, PROJ_COLS), lambda j: (0, j)),
        out_shape=jax.ShapeDtypeStruct((m, n), F32),
        compiler_params=_cparams(1),
        name=name,
    )(x, w)


def _mem_scores(mq_all, mk_ref):
    return [_dot_nt(mq_all[:, hm * MEM_HD:(hm + 1) * MEM_HD], mk_ref[:, hm * MEM_HD:(hm + 1) * MEM_HD])
            * (MEM_HD ** -0.5) for hm in range(MEM_HEADS)]


def _put_gated(bb_ref, gs_ref, col0, o):
    cols = slice(col0, col0 + o.shape[-1])
    bb_ref[:, cols] = (o * gs_ref[:, cols]).astype(BF16)


def _mem_values(weights, mv_ref, bb_ref, gs_ref, out_col0):
    for hm, (p, l) in enumerate(weights):
        lo = hm * MEM_HD
        _put_gated(bb_ref, gs_ref, out_col0 + lo, _dot(p, mv_ref[:, lo:lo + MEM_HD]) / l)


def _swa_prompt_kernel(relb_ref, sink_ref, h_ref, win_ref, wout_ref, mk_ref, mv_ref, lnw_ref, lnb_ref,
                       out_ref, kout_ref, vout_ref,
                       bias_scr, k_scr, v_scr, gs_scr, bb_scr, *, rows):
    b = pl.program_id(0)
    t = pl.program_id(1)
    w2 = 2 * WINDOW

    @pl.when(jnp.logical_and(b == 0, t == 0))
    def _build_bias():
        qi = lax.broadcasted_iota(jnp.int32, (WINDOW, w2), 0)
        kj = lax.broadcasted_iota(jnp.int32, (WINDOW, w2), 1)
        dist = qi + WINDOW - kj

        def per_head(hq, carry):
            acc = jnp.full((WINDOW, w2), MASKED, F32)
            for bk, rng in enumerate(_BUCKET_RANGES):
                if rng is not None:
                    hit = jnp.logical_and(dist >= rng[0], dist <= rng[1])
                    acc = jnp.where(hit, relb_ref[bk, hq], acc)
            bias_scr[0, hq] = acc
            bias_scr[1, hq] = jnp.where(kj < WINDOW, MASKED, acc)
            return carry

        lax.fori_loop(0, SWA_QH, per_head, 0)

    @pl.when(t == 0)
    def _reset_window():
        k_scr[0:WINDOW, :] = jnp.zeros((WINDOW, SWA_KV), BF16)
        v_scr[0:WINDOW, :] = jnp.zeros((WINDOW, SWA_KV), BF16)

    qkv_w = SWA_Q + 2 * SWA_KV
    n_blocks = rows // WINDOW
    n_gate = D_MODEL // GATE_CHUNK
    heads = [(g, j) for g in range(SWA_KVH) for j in range(SWA_GROUP)]
    xb, qkv, mq, scores, mem_s = {}, {}, {}, {}, {}

    def project(qb):
        r0 = qb * WINDOW
        xb[qb] = h_ref[0, r0:r0 + WINDOW, :].astype(BF16)
        qkv[qb] = jnp.dot(xb[qb], win_ref[:, :qkv_w], preferred_element_type=F32)
        mq[qb] = jnp.dot(xb[qb], win_ref[:, qkv_w:qkv_w + MEM_Q], preferred_element_type=F32)
        k = qkv[qb][:, SWA_Q:SWA_Q + SWA_KV]
        v = qkv[qb][:, SWA_Q + SWA_KV:]
        if qb == n_blocks - 1:
            kout_ref[0] = k
            vout_ref[0] = v
        k_scr[WINDOW + r0:w2 + r0, :] = k.astype(BF16)
        v_scr[WINDOW + r0:w2 + r0, :] = v.astype(BF16)

    def score(qb):
        r0 = qb * WINDOW
        table = jnp.where(t == 0, 1, 0) if qb == 0 else 0
        scores[qb] = []
        for idx, (g, j) in enumerate(heads):
            hq = g * SWA_GROUP + j
            q = qkv[qb][:, hq * SWA_HD:(hq + 1) * SWA_HD] * (SWA_HD ** -0.5)
            scores[qb].append(_dot_nt(q, k_scr[r0:r0 + w2, g * SWA_HD:(g + 1) * SWA_HD]) + bias_scr[table, hq])
            for chunk in range(idx * n_gate // len(heads), (idx + 1) * n_gate // len(heads)):
                lo = chunk * GATE_CHUNK
                gate = jnp.dot(xb[qb], win_ref[:, qkv_w + MEM_Q + lo:qkv_w + MEM_Q + lo + GATE_CHUNK],
                               preferred_element_type=F32)
                gs_scr[r0:r0 + WINDOW, lo:lo + GATE_CHUNK] = _silu(gate)
        mem_s[qb] = _mem_scores(mq[qb], mk_ref)

    def attend(qb):
        r0 = qb * WINDOW
        bb = bb_scr.at[r0:r0 + WINDOW, :]
        gs = gs_scr.at[r0:r0 + WINDOW, :]
        sinks = [sink_ref[g * SWA_GROUP + j] for g, j in heads]
        maxes = [jnp.maximum(jnp.max(s, axis=-1, keepdims=True), sk) for s, sk in zip(scores[qb], sinks)]
        mem_max = [jnp.max(s, axis=-1, keepdims=True) for s in mem_s[qb]]
        probs = [jnp.exp(s - m) for s, m in zip(scores[qb], maxes)]
        mem_p = [jnp.exp(s - m) for s, m in zip(mem_s[qb], mem_max)]
        weights = [(p, jnp.sum(p, axis=-1, keepdims=True) + jnp.exp(sk - m))
                   for p, m, sk in zip(probs, maxes, sinks)]
        mem_w = [(p, jnp.sum(p, axis=-1, keepdims=True)) for p in mem_p]
        for (g, j), (p, l) in zip(heads, weights):
            o = _dot(p, v_scr[r0:r0 + w2, g * SWA_HD:(g + 1) * SWA_HD]) / l
            _put_gated(bb, gs, (g * SWA_GROUP + j) * SWA_HD, o)
        _mem_values(mem_w, mv_ref, bb, gs, SWA_Q)

    def finish(qb):
        r0 = qb * WINDOW
        y = jnp.dot(bb_scr[r0:r0 + WINDOW, :], wout_ref[:, :D_MODEL], preferred_element_type=F32)
        out_ref[0, r0:r0 + WINDOW, :] = _layer_norm(DEEPNORM_ALPHA * h_ref[0, r0:r0 + WINDOW, :] + y,
                                                    lnw_ref[...], lnb_ref[...])

    project(0)
    score(0)
    for qb in range(n_blocks):
        if qb + 1 < n_blocks:
            project(qb + 1)
            score(qb + 1)
        attend(qb)
        finish(qb)

    k_scr[0:WINDOW, :] = k_scr[rows:rows + WINDOW, :]
    v_scr[0:WINDOW, :] = v_scr[rows:rows + WINDOW, :]


def _swa_prompt_layer(h, w_in, w_out, mk_bf, mv_bf, layer, rel_bias, sinks, ln_w, ln_b, rows):
    bsz, t, d = h.shape
    kernel = functools.partial(_swa_prompt_kernel, rows=rows)
    mem_spec = pl.BlockSpec((None, N_MEM, MEM_Q), lambda b, i: (layer, b, 0))
    return pl.pallas_call(
        kernel,
        grid=(bsz, t // rows),
        in_specs=[_SMEM, _SMEM,
                  pl.BlockSpec((1, rows, d), lambda b, i: (b, i, 0)),
                  _resident(w_in.shape),
                  _resident(w_out.shape),
                  mem_spec, mem_spec,
                  _resident((1, d)), _resident((1, d))],
        out_specs=[pl.BlockSpec((1, rows, d), lambda b, i: (b, i, 0)),
                   pl.BlockSpec((1, WINDOW, SWA_KV), lambda b, i: (b, 0, 0)),
                   pl.BlockSpec((1, WINDOW, SWA_KV), lambda b, i: (b, 0, 0))],
        out_shape=[jax.ShapeDtypeStruct((bsz, t, d), F32),
                   jax.ShapeDtypeStruct((bsz, WINDOW, SWA_KV), F32),
                   jax.ShapeDtypeStruct((bsz, WINDOW, SWA_KV), F32)],
        scratch_shapes=[pltpu.VMEM((2, SWA_QH, WINDOW, 2 * WINDOW), F32),
                        pltpu.VMEM((WINDOW + rows, SWA_KV), BF16),
                        pltpu.VMEM((WINDOW + rows, SWA_KV), BF16),
                        pltpu.VMEM((rows, d), F32),
                        pltpu.VMEM((rows, d), BF16)],
        compiler_params=_cparams(2),
        name="swa_prompt_layer",
    )(rel_bias, sinks, h, w_in, w_out, mk_bf, mv_bf, ln_w.reshape(1, d), ln_b.reshape(1, d))


HG_CHUNK = 128
HG_DIRECT = 8
HG_HEAD_COLS = 3 * HG_D


def _hgrn_head_major(w):
    d = w.shape[0]
    qfi = w[:, :3 * HG_K].reshape(d, 3, HG_HEADS, HG_D).transpose(0, 2, 1, 3).reshape(d, 3 * HG_K)
    return jnp.concatenate([qfi, w[:, 3 * HG_K:]], axis=1)


def _lower_bound(logits, layer):
    e = jnp.exp(logits - jnp.max(logits, axis=0, keepdims=True))
    sm = e / jnp.sum(e, axis=0, keepdims=True)
    return jnp.sum(sm[1:layer + 1, :], axis=0, keepdims=True)


def _cumsum_rows(tril_bf, g):
    g1 = g.astype(BF16)
    r1 = g - g1.astype(F32)
    g2 = r1.astype(BF16)
    g3 = (r1 - g2.astype(F32)).astype(BF16)
    d = jnp.dot(tril_bf, jnp.concatenate([g1, g2, g3], axis=1), preferred_element_type=F32)
    return (d[:, :HG_D] + d[:, HG_D:2 * HG_D]) + d[:, 2 * HG_D:]


def _hgrn_chunk_scores(qs, kk, gc):
    c = HG_CHUNK
    nb = c // HG_DIRECT
    q3 = qs.reshape(nb, HG_DIRECT, HG_D)
    g3 = gc.reshape(nb, HG_DIRECT, HG_D)
    k3 = kk.reshape(nb, HG_DIRECT, HG_D)
    lane = lax.broadcasted_iota(jnp.int32, (nb, HG_DIRECT, c), 2)
    blk = lax.broadcasted_iota(jnp.int32, (nb, HG_DIRECT, c), 0)
    trow = lax.broadcasted_iota(jnp.int32, (nb, HG_DIRECT, c), 1)
    a3 = jnp.zeros((nb, HG_DIRECT, c), F32)
    for sl in range(HG_DIRECT):
        e = jnp.exp(g3 - g3[:, sl:sl + 1, :])
        col = jnp.sum(e * q3 * k3[:, sl:sl + 1, :], axis=-1, keepdims=True)
        hit = jnp.logical_and(lane == blk * HG_DIRECT + sl, trow >= sl)
        a3 = jnp.where(hit, col, a3)
    a = a3.reshape(c, c)

    ti = lax.broadcasted_iota(jnp.int32, (c, c), 0)
    si = lax.broadcasted_iota(jnp.int32, (c, c), 1)
    ri = lax.broadcasted_iota(jnp.int32, (c, 1), 0)
    half = HG_DIRECT
    while half < c:
        span = 2 * half
        gsp = gc.reshape(c // span, span, HG_D)
        gref = jnp.broadcast_to(gsp[:, half - 1:half, :], gsp.shape).reshape(c, HG_D)
        upper = (ri % span) >= half
        eq = jnp.where(upper, jnp.exp(gc - gref), 0.0)
        ek = jnp.where(upper, 0.0, jnp.exp(gref - gc))
        a = a + jnp.where((ti // span) == (si // span), _dot_nt(qs * eq, kk * ek), 0.0)
        half = span
    return a


def _hgrn_prompt_kernel(h_ref, win_ref, wout_ref, lbl_ref, nw_ref, mk_ref, mv_ref, lnw_ref, lnb_ref,
                        out_ref, st_ref,
                        st_scr, gs_scr, bb_scr, *, layer, rows):
    t = pl.program_id(1)
    c = HG_CHUNK
    n_blocks = rows // c
    n_gate = D_MODEL // GATE_CHUNK
    n_pairs = HG_HEADS // 2
    heads = range(HG_HEADS)

    @pl.when(t == 0)
    def _reset_state():
        st_scr[...] = jnp.zeros(st_scr.shape, F32)

    lb_all = _lower_bound(lbl_ref[...], layer)
    ri = lax.broadcasted_iota(jnp.int32, (c, c), 0)
    ci = lax.broadcasted_iota(jnp.int32, (c, c), 1)
    tril_bf = jnp.where(ri >= ci, 1.0, 0.0).astype(BF16)

    xb, qs, kk, iv, gc, a, mem_s, y_cols = ({} for _ in range(8))

    def rows_of(blk):
        return slice(blk * c, (blk + 1) * c)

    def load(blk):
        xb[blk] = h_ref[0, rows_of(blk), :].astype(BF16)

    def project_pair(blk, pair):
        up = jnp.dot(xb[blk], win_ref[:, 2 * pair * HG_HEAD_COLS:2 * (pair + 1) * HG_HEAD_COLS],
                     preferred_element_type=F32)
        for hh in (2 * pair, 2 * pair + 1):
            uh = up[:, (hh - 2 * pair) * HG_HEAD_COLS:(hh - 2 * pair + 1) * HG_HEAD_COLS]
            lb = lb_all[:, hh * HG_D:(hh + 1) * HG_D]
            fg = lb + (1.0 - lb) * jax.nn.sigmoid(uh[:, HG_D:2 * HG_D])
            qs[blk, hh] = _silu(uh[:, :HG_D])
            iv[blk, hh] = uh[:, 2 * HG_D:]
            kk[blk, hh] = 1.0 - fg
            gc[blk, hh] = _cumsum_rows(tril_bf, jnp.log(fg))

    def project_mem(blk):
        mq = jnp.dot(xb[blk], win_ref[:, 3 * HG_K:3 * HG_K + MEM_Q], preferred_element_type=F32)
        mem_s[blk] = _mem_scores(mq, mk_ref)

    def project_gate(blk, chunk):
        lo = chunk * GATE_CHUNK
        gate = jnp.dot(xb[blk], win_ref[:, 3 * HG_K + MEM_Q + lo:3 * HG_K + MEM_Q + lo + GATE_CHUNK],
                       preferred_element_type=F32)
        gs_scr[rows_of(blk), lo:lo + GATE_CHUNK] = _silu(gate)

    def scores(blk, fillers):
        for hh in heads:
            a[blk, hh] = _hgrn_chunk_scores(qs[blk, hh], kk[blk, hh], gc[blk, hh])
            for job in fillers[hh * len(fillers) // HG_HEADS:(hh + 1) * len(fillers) // HG_HEADS]:
                job()

    def recur(blk):
        bb = bb_scr.at[rows_of(blk), :]
        gs = gs_scr.at[rows_of(blk), :]
        iv_t = [iv[blk, hh].T for hh in heads]
        for hh in heads:
            lhs = jnp.concatenate([a[blk, hh], qs[blk, hh] * jnp.exp(gc[blk, hh])], axis=1)
            rhs = jnp.concatenate([iv_t[hh], st_scr[hh]], axis=1)
            o = _dot_nt(lhs, rhs)
            ms = jnp.mean(o * o, axis=-1, keepdims=True)
            _put_gated(bb, gs, hh * HG_D, o * lax.rsqrt(ms + RMS_EPS) * nw_ref[:, hh * HG_D:(hh + 1) * HG_D])
        for hh in heads:
            g_last = gc[blk, hh][c - 1:c, :]
            kdec = kk[blk, hh] * jnp.exp(g_last - gc[blk, hh])
            st_scr[hh] = st_scr[hh] * jnp.exp(g_last) + _dot(iv_t[hh], kdec)
        mem_p = [jnp.exp(s - jnp.max(s, axis=-1, keepdims=True)) for s in mem_s[blk]]
        _mem_values([(p, jnp.sum(p, axis=-1, keepdims=True)) for p in mem_p], mv_ref, bb, gs, HG_K)

    def project_out(blk, chunk):
        lo = chunk * GATE_CHUNK
        y_cols[blk, chunk] = jnp.dot(bb_scr[rows_of(blk), :], wout_ref[:, lo:lo + GATE_CHUNK],
                                     preferred_element_type=F32)

    def finish(blk):
        y = jnp.concatenate([y_cols[blk, chunk] for chunk in range(n_gate)], axis=1)
        out_ref[0, rows_of(blk), :] = _layer_norm(DEEPNORM_ALPHA * h_ref[0, rows_of(blk), :] + y,
                                                  lnw_ref[...], lnb_ref[...])

    load(0)
    for pair in range(n_pairs):
        project_pair(0, pair)
    project_mem(0)
    for blk in range(n_blocks):
        fillers = []
        if blk + 1 < n_blocks:
            load(blk + 1)
            fillers += [functools.partial(project_pair, blk + 1, pair) for pair in range(n_pairs)]
            fillers.append(functools.partial(project_mem, blk + 1))
        if blk > 0:
            fillers += [functools.partial(project_out, blk - 1, chunk) for chunk in range(n_gate)]
        fillers += [functools.partial(project_gate, blk, chunk) for chunk in range(n_gate)]
        scores(blk, fillers)
        if blk > 0:
            finish(blk - 1)
        recur(blk)
    last = n_blocks - 1
    for chunk in range(n_gate):
        project_out(last, chunk)
    finish(last)

    @pl.when(t == pl.num_programs(1) - 1)
    def _emit_state():
        for hh in heads:
            st_ref[0, hh] = st_scr[hh].T


def _hgrn_prompt_layer(h, w_in_hm, w_out, mk_bf, mv_bf, layer, lb_logits, norm_w, ln_w, ln_b, rows):
    bsz, t, d = h.shape
    kernel = functools.partial(_hgrn_prompt_kernel, layer=layer, rows=rows)
    mem_spec = pl.BlockSpec((None, N_MEM, MEM_Q), lambda b, i: (layer, b, 0))
    return pl.pallas_call(
        kernel,
        grid=(bsz, t // rows),
        in_specs=[pl.BlockSpec((1, rows, d), lambda b, i: (b, i, 0)),
                  _resident(w_in_hm.shape),
                  _resident(w_out.shape),
                  _resident((DEPTH, HG_K)),
                  _resident((1, HG_K)),
                  mem_spec, mem_spec,
                  _resident((1, d)), _resident((1, d))],
        out_specs=[pl.BlockSpec((1, rows, d), lambda b, i: (b, i, 0)),
                   pl.BlockSpec((1, HG_HEADS, HG_D, HG_D), lambda b, i: (b, 0, 0, 0))],
        out_shape=[jax.ShapeDtypeStruct((bsz, t, d), F32),
                   jax.ShapeDtypeStruct((bsz, HG_HEADS, HG_D, HG_D), F32)],
        scratch_shapes=[pltpu.VMEM((HG_HEADS, HG_D, HG_D), F32),
                        pltpu.VMEM((rows, d), F32),
                        pltpu.VMEM((rows, d), BF16)],
        compiler_params=_cparams(2),
        name="hgrn_prompt_layer",
    )(h, w_in_hm, w_out, lb_logits, norm_w.reshape(1, HG_K), mk_bf, mv_bf, ln_w.reshape(1, d), ln_b.reshape(1, d))


REQS_PER_STEP = 4


def _heads_to_sublanes(row, col0, n, width):
    return jnp.concatenate([row[:, col0 + i * width:col0 + (i + 1) * width] for i in range(n)], axis=0)


def _sample_mem_attend(urow, col0, cmk_ref, cmv_ref, r):
    pieces = []
    for hm in range(MEM_HEADS):
        lo = col0 + hm * MEM_HD
        rows = pl.ds(hm, N_MEM, stride=MEM_HEADS)
        s = jnp.sum(cmk_ref[r, rows, :] * urow[:, lo:lo + MEM_HD], axis=-1, keepdims=True) * (MEM_HD ** -0.5)
        s = jnp.broadcast_to(s, (N_MEM, MEM_HD))
        p = jnp.exp(s - jnp.max(s, axis=0, keepdims=True))
        pieces.append(jnp.sum(p * cmv_ref[r, rows, :], axis=0, keepdims=True) / jnp.sum(p, axis=0, keepdims=True))
    return jnp.concatenate(pieces, axis=1)


def _sample_swa_kernel(relb_ref, sink_ref, u_ref, ck_ref, cv_ref, cmk_ref, cmv_ref,
                       br_ref, ko_ref, vo_ref, bias_scr, bias0_scr, sink_scr):
    nbuf = WINDOW

    @pl.when(pl.program_id(0) == 0)
    def _build_tables():
        dist = nbuf - lax.broadcasted_iota(jnp.int32, (V7X_SUBLANES, nbuf), 1)
        rid = lax.broadcasted_iota(jnp.int32, (V7X_SUBLANES, nbuf), 0)
        for g in range(SWA_KVH):
            acc = jnp.where(rid < SWA_GROUP, MASKED, 0.0).astype(F32)
            acc0 = jnp.zeros((V7X_SUBLANES, nbuf), F32)
            sk = jnp.zeros((V7X_SUBLANES, nbuf), F32)
            for j in range(SWA_GROUP):
                hq = g * SWA_GROUP + j
                sk = jnp.where(rid == j, sink_ref[hq], sk)
                acc0 = jnp.where(rid == j, relb_ref[0, hq], acc0)
                for bk, rng in enumerate(_BUCKET_RANGES):
                    if rng is not None:
                        hit = jnp.logical_and(rid == j, jnp.logical_and(dist >= rng[0], dist <= rng[1]))
                        acc = jnp.where(hit, relb_ref[bk, hq], acc)
            bias_scr[g] = acc
            bias0_scr[g] = acc0
            sink_scr[g] = sk

    g0 = SWA_Q + 2 * SWA_KV + MEM_Q
    pad = jnp.zeros((V7X_SUBLANES - SWA_GROUP, SWA_HD), F32)
    pairs = [(r, g) for r in range(REQS_PER_STEP) for g in range(SWA_KVH)]
    urows = [u_ref[r] for r in range(REQS_PER_STEP)]

    scores = []
    for r, g in pairs:
        kg = ck_ref[r, pl.ds(g, nbuf, stride=SWA_KVH), :]
        k_new = urows[r][:, SWA_Q + g * SWA_HD:SWA_Q + (g + 1) * SWA_HD]
        v_new = urows[r][:, SWA_Q + SWA_KV + g * SWA_HD:SWA_Q + SWA_KV + (g + 1) * SWA_HD]
        older = pl.ds(SWA_KVH + g, nbuf - 1, stride=SWA_KVH)
        kept = pl.ds(g, nbuf - 1, stride=SWA_KVH)
        ko_ref[r, kept, :] = ck_ref[r, older, :]
        vo_ref[r, kept, :] = cv_ref[r, older, :]
        last = (nbuf - 1) * SWA_KVH + g
        ko_ref[r, last:last + 1, :] = k_new
        vo_ref[r, last:last + 1, :] = v_new
        qg = jnp.concatenate(
            [urows[r][:, (g * SWA_GROUP + j) * SWA_HD:(g * SWA_GROUP + j + 1) * SWA_HD] for j in range(SWA_GROUP)]
            + [pad], axis=0) * (SWA_HD ** -0.5)
        s = _dot_nt(qg, kg) + bias_scr[g]
        s_new = jnp.sum(qg * k_new, axis=-1, keepdims=True) + bias0_scr[g][:, 0:1]
        scores.append((s, s_new, v_new))

    weights = []
    for (r, g), (s, s_new, v_new) in zip(pairs, scores):
        sink = sink_scr[g][:, 0:1]
        m = jnp.maximum(jnp.max(s, axis=-1, keepdims=True), jnp.maximum(s_new, sink))
        p = jnp.exp(s - m)
        p_new = jnp.exp(s_new - m)
        l = jnp.sum(p, axis=-1, keepdims=True) + p_new + jnp.exp(sink - m)
        weights.append((p, p_new * v_new, l))

    pieces = [[] for _ in range(REQS_PER_STEP)]
    for (r, g), (p, o_new, l) in zip(pairs, weights):
        o = (_dot(p, cv_ref[r, pl.ds(g, nbuf, stride=SWA_KVH), :]) + o_new) / l
        pieces[r].extend(o[j:j + 1, :] for j in range(SWA_GROUP))

    for r in range(REQS_PER_STEP):
        mem_o = _sample_mem_attend(urows[r], SWA_Q + 2 * SWA_KV, cmk_ref, cmv_ref, r)
        br_ref[r] = jnp.concatenate(pieces[r] + [mem_o], axis=1) * _silu(urows[r][:, g0:g0 + D_MODEL])


def _cache_spec(layer, shape):
    return pl.BlockSpec((None, REQS_PER_STEP) + shape, lambda i: (layer, i) + (0,) * len(shape))


def _row_spec(width):
    return pl.BlockSpec((REQS_PER_STEP, 1, width), lambda i: (i, 0, 0))


def _head_minor_rows(cache):
    nl, ns, rows, heads, hd = cache.shape
    return cache.reshape(nl, ns, rows * heads, hd)


def _sample_swa_attend(u, cache_k, cache_v, j, cache_mk, cache_mv, layer, rel_bias, sinks):
    ns = u.shape[0]
    nbuf = cache_k.shape[2] // SWA_KVH
    swa_shape = (nbuf * SWA_KVH, SWA_HD)
    mem_shape = (N_MEM * MEM_HEADS, MEM_HD)
    new_cache = pl.BlockSpec((REQS_PER_STEP,) + swa_shape, lambda i: (i, 0, 0))
    br, ko, vo = pl.pallas_call(
        _sample_swa_kernel,
        grid=(ns // REQS_PER_STEP,),
        in_specs=[_SMEM, _SMEM, _row_spec(SWA_WIDTH), _cache_spec(j, swa_shape), _cache_spec(j, swa_shape),
                  _cache_spec(layer, mem_shape), _cache_spec(layer, mem_shape)],
        out_specs=[_row_spec(D_MODEL), new_cache, new_cache],
        out_shape=[jax.ShapeDtypeStruct((ns, 1, D_MODEL), F32),
                   jax.ShapeDtypeStruct((ns,) + swa_shape, F32),
                   jax.ShapeDtypeStruct((ns,) + swa_shape, F32)],
        scratch_shapes=[pltpu.VMEM((SWA_KVH, V7X_SUBLANES, nbuf), F32)] * 3,
        compiler_params=_cparams(1),
        name="sample_swa_attend",
    )(rel_bias, sinks, u.reshape(ns, 1, SWA_WIDTH), cache_k, cache_v, cache_mk, cache_mv)
    return br.reshape(ns, D_MODEL), ko, vo


def _sample_hgrn_gate_kernel(u_ref, mix_ref, cmk_ref, cmv_ref, br_ref):
    g0 = 3 * HG_K + MEM_Q
    for r in range(REQS_PER_STEP):
        urow = u_ref[r]
        mem_o = _sample_mem_attend(urow, 3 * HG_K, cmk_ref, cmv_ref, r)
        br_ref[r] = jnp.concatenate([mix_ref[r], mem_o], axis=1) * _silu(urow[:, g0:g0 + D_MODEL])


def _sample_hgrn_gate(u, mix, cache_mk, cache_mv, layer):
    ns = u.shape[0]
    mem_shape = (N_MEM * MEM_HEADS, MEM_HD)
    br = pl.pallas_call(
        _sample_hgrn_gate_kernel,
        grid=(ns // REQS_PER_STEP,),
        in_specs=[_row_spec(HG_WIDTH), _row_spec(HG_K), _cache_spec(layer, mem_shape), _cache_spec(layer, mem_shape)],
        out_specs=_row_spec(D_MODEL),
        out_shape=jax.ShapeDtypeStruct((ns, 1, D_MODEL), F32),
        compiler_params=_cparams(1),
        name="sample_hgrn_gate",
    )(u.reshape(ns, 1, HG_WIDTH), mix.reshape(ns, 1, HG_K), cache_mk, cache_mv)
    return br.reshape(ns, D_MODEL)


def _hgrn_sample_kernel(q_ref, f_ref, iv_ref, lbl_ref, nw_ref, st_ref, mix_ref, so_ref, o_scr, *, layer, ns):
    lb = _lower_bound(lbl_ref[...], layer)
    fg = lb + (1.0 - lb) * jax.nn.sigmoid(f_ref[...])
    kk_t = (1.0 - fg).T
    fg_t = fg.T
    qs_t = _silu(q_ref[...]).T
    for r in range(ns):
        state = fg_t[:, r:r + 1] * st_ref[r] + kk_t[:, r:r + 1] * iv_ref[r:r + 1, :]
        so_ref[r] = state
        o_scr[r:r + 1, :] = jnp.sum(qs_t[:, r:r + 1] * state, axis=0, keepdims=True)
    o = o_scr[...]
    ms = jnp.mean(o * o, axis=-1, keepdims=True)
    mix_ref[...] = o * lax.rsqrt(ms + RMS_EPS) * nw_ref[...]


def _hgrn_sample_step(u_hm, state, j, lb_logits, norm_w, layer):
    ns = u_hm.shape[0]
    col = lambda off: pl.BlockSpec((ns, HG_D), lambda h: (0, 3 * h + off))
    kernel = functools.partial(_hgrn_sample_kernel, layer=layer, ns=ns)
    return pl.pallas_call(
        kernel,
        grid=(HG_HEADS,),
        in_specs=[col(0), col(1), col(2),
                  pl.BlockSpec((DEPTH, HG_D), lambda h: (0, h)),
                  pl.BlockSpec((1, HG_D), lambda h: (0, h)),
                  pl.BlockSpec((None, ns, None, HG_D, HG_D), lambda h: (j, 0, h, 0, 0))],
        out_specs=[pl.BlockSpec((ns, HG_D), lambda h: (0, h)),
                   pl.BlockSpec((ns, None, HG_D, HG_D), lambda h: (0, h, 0, 0))],
        out_shape=[jax.ShapeDtypeStruct((ns, HG_K), F32),
                   jax.ShapeDtypeStruct(state.shape[1:], F32)],
        scratch_shapes=[pltpu.VMEM((ns, HG_D), F32)],
        compiler_params=_cparams(1),
        name="hgrn_sample_step",
    )(u_hm, u_hm, u_hm, lb_logits, norm_w.reshape(1, HG_K), state)


def _outproj_ln_kernel(br_ref, h_ref, wout_ref, lnw_ref, lnb_ref, out_ref):
    y = jnp.dot(br_ref[...].astype(BF16), wout_ref[:, :D_MODEL], preferred_element_type=F32)
    out_ref[...] = _layer_norm(DEEPNORM_ALPHA * h_ref[...] + y, lnw_ref[...], lnb_ref[...])


def _outproj_ln(branch, h, w_out, ln_w, ln_b):
    ns, d = h.shape
    full = lambda shape: pl.BlockSpec(shape, lambda i: (0, 0))
    return pl.pallas_call(
        _outproj_ln_kernel,
        grid=(1,),
        in_specs=[full((ns, d)), full((ns, d)), full(w_out.shape), full((1, d)), full((1, d))],
        out_specs=full((ns, d)),
        out_shape=jax.ShapeDtypeStruct((ns, d), F32),
        compiler_params=_cparams(1),
        name="outproj_ln",
    )(branch, h, w_out, ln_w.reshape(1, d), ln_b.reshape(1, d))


def kernel(x_prompt, x_sample, cache_mem_k, cache_mem_v, cache_swa_k, cache_swa_v, state_hgrn, mem_prompt, rel_bias, swa_w_in, swa_sinks, hg_w_in, hg_lb_logits, hg_norm_w, w_mem_k, w_mem_v, w_out, ln_w, ln_b):
    bp, t, d = x_prompt.shape
    ns = x_sample.shape[0]
    assert d == D_MODEL and x_sample.shape[1] == 1 and t % PROMPT_ROWS == 0
    assert cache_swa_k.shape[2] == WINDOW <= PAST_LEN
    assert w_mem_k.shape[0] == DEPTH

    wo = [(_pad_cols(w_out[i]) if i % 2 else w_out[i]).astype(BF16) for i in range(DEPTH)]
    mem_k_prompt, mem_v_prompt, mk_bf, mv_bf = _mem_kv_proj(mem_prompt, w_mem_k, w_mem_v)
    cmk, cmv = _head_minor_rows(cache_mem_k), _head_minor_rows(cache_mem_v)
    csk, csv = _head_minor_rows(cache_swa_k), _head_minor_rows(cache_swa_v)

    hp = x_prompt
    hs = x_sample.reshape(ns, d)
    swa_kp, swa_vp, swa_ks, swa_vs, hg_sp, hg_ss = [], [], [], [], [], []
    for i in range(DEPTH):
        j = i // 2
        if i % 2 == 0:
            w_in = _pad_cols(swa_w_in[j]).astype(BF16)
            hp, kw, vw = _swa_prompt_layer(hp, w_in, wo[i], mk_bf, mv_bf, i, rel_bias, swa_sinks[j],
                                           ln_w[i], ln_b[i], rows=PROMPT_ROWS)
            swa_kp.append(kw.reshape(bp, WINDOW, SWA_KVH, SWA_HD))
            swa_vp.append(vw.reshape(bp, WINDOW, SWA_KVH, SWA_HD))
            u = _proj(hs, w_in, SWA_WIDTH, "sample_swa_proj")
            br, ko, vo = _sample_swa_attend(u, csk, csv, j, cmk, cmv, i, rel_bias, swa_sinks[j])
            swa_ks.append(ko.reshape(ns, WINDOW, SWA_KVH, SWA_HD))
            swa_vs.append(vo.reshape(ns, WINDOW, SWA_KVH, SWA_HD))
        else:
            w_in = _pad_cols(_hgrn_head_major(hg_w_in[j])).astype(BF16)
            hp, st = _hgrn_prompt_layer(hp, w_in, wo[i], mk_bf, mv_bf, i, hg_lb_logits, hg_norm_w[j],
                                        ln_w[i], ln_b[i], rows=HG_ROWS)
            hg_sp.append(st)
            u = _proj(hs, w_in, HG_WIDTH, "sample_hgrn_proj")
            mix, so = _hgrn_sample_step(u, state_hgrn, j, hg_lb_logits, hg_norm_w[j], i)
            hg_ss.append(so)
            br = _sample_hgrn_gate(u, mix, cmk, cmv, i)
        hs = _outproj_ln(br, hs, wo[i], ln_w[i], ln_b[i])

    return (hp, hs.reshape(ns, 1, d), mem_k_prompt, mem_v_prompt,
            jnp.stack(swa_kp), jnp.stack(swa_vp), jnp.stack(hg_sp),
            jnp.stack(swa_ks), jnp.stack(swa_vs), jnp.stack(hg_ss))
```

```python
import functools
import math

import numpy as np
import jax
import jax.numpy as jnp
from jax import lax
from jax.experimental import pallas as pl
from jax.experimental.pallas import tpu as pltpu

F32 = jnp.float32
BF16 = jnp.bfloat16

D_MODEL = 2048
DEPTH = 2
PAST_LEN = 8192
N_MEM = 256
MEM_HEADS = 4
MEM_HD = 128
MEM_Q = MEM_HEADS * MEM_HD
WINDOW = 128
SWA_HD = 64
SWA_KVH = 4
SWA_GROUP = 6
SWA_QH = SWA_KVH * SWA_GROUP
SWA_Q = SWA_QH * SWA_HD
SWA_KV = SWA_KVH * SWA_HD
N_BUCKETS = 32
HG_HEADS = 12
HG_D = 128
HG_K = HG_HEADS * HG_D
SWA_WIDTH = SWA_Q + 2 * SWA_KV + MEM_Q + D_MODEL
HG_WIDTH = 3 * HG_K + MEM_Q + D_MODEL
DEEPNORM_ALPHA = (2.0 * DEPTH) ** 0.25
LN_EPS = 1e-5
RMS_EPS = 1e-6
MASKED = -1e30

V7X_SUBLANES = 8
V7X_LANES = 128
V7X_MXU_COLS = 256
V7X_VMEM_LIMIT = 60000 * 1024
PROJ_COLS = 512
GATE_CHUNK = 512
PROMPT_ROWS = 256
HG_ROWS = 128


def _t5_bucket_ranges():
    n = np.arange(WINDOW)
    max_exact = N_BUCKETS // 2
    nf = np.maximum(n, 1).astype(np.float64)
    large = max_exact + (np.log(nf / max_exact) / math.log(WINDOW / max_exact)
                         * (N_BUCKETS - max_exact)).astype(np.int64)
    bucket = np.where(n < max_exact, n, np.minimum(large, N_BUCKETS - 1))
    ranges = []
    for b in range(N_BUCKETS):
        idx = np.nonzero(bucket == b)[0]
        ranges.append((int(idx[0]), int(idx[-1])) if idx.size else None)
    return ranges


_BUCKET_RANGES = _t5_bucket_ranges()


def _dot(a, b):
    return jnp.dot(a.astype(BF16), b.astype(BF16), preferred_element_type=F32)


def _dot_nt(a, b):
    return lax.dot_general(a.astype(BF16), b.astype(BF16), (((1,), (1,)), ((), ())),
                           preferred_element_type=F32)


def _layer_norm(z, w, b):
    mu = jnp.mean(z, axis=-1, keepdims=True)
    zc = z - mu
    var = jnp.mean(zc * zc, axis=-1, keepdims=True)
    return zc * lax.rsqrt(var + LN_EPS) * w + b


def _silu(x):
    return x * jax.nn.sigmoid(x)


def _cparams(n_axes):
    return pltpu.CompilerParams(dimension_semantics=("arbitrary",) * n_axes,
                                vmem_limit_bytes=V7X_VMEM_LIMIT)


def _resident(shape):
    nd = len(shape)
    return pl.BlockSpec(shape, lambda *_: (0,) * nd, pipeline_mode=pl.Buffered(1))


_SMEM = pl.BlockSpec(memory_space=pltpu.SMEM)


def _mem_kv_kernel(x_ref, wk_ref, wv_ref, ko_ref, vo_ref, kb_ref, vb_ref):
    xb = x_ref[...].astype(BF16)
    for w_ref, o_ref, b_ref in ((wk_ref, ko_ref, kb_ref), (wv_ref, vo_ref, vb_ref)):
        y = jnp.dot(xb, w_ref[...].astype(BF16), preferred_element_type=F32)
        b_ref[...] = y.astype(BF16)
        for hm in range(MEM_HEADS):
            o_ref[:, hm, :] = y[:, hm * MEM_HD:(hm + 1) * MEM_HD]


def _mem_kv_proj(mem, w_k, w_v):
    bsz, n, d = mem.shape
    nl = w_k.shape[0]
    w_spec = pl.BlockSpec((None, d, MEM_Q), lambda l, b: (l, 0, 0))
    o_spec = pl.BlockSpec((None, None, n, MEM_HEADS, MEM_HD), lambda l, b: (l, b, 0, 0, 0))
    b_spec = pl.BlockSpec((None, n, MEM_Q), lambda l, b: (l, b, 0))
    o_shape = jax.ShapeDtypeStruct((nl, bsz, n, MEM_HEADS, MEM_HD), F32)
    b_shape = jax.ShapeDtypeStruct((nl, bsz * n, MEM_Q), BF16)
    return pl.pallas_call(
        _mem_kv_kernel,
        grid=(nl, bsz),
        in_specs=[pl.BlockSpec((None, n, d), lambda l, b: (b, 0, 0)), w_spec, w_spec],
        out_specs=[o_spec, o_spec, b_spec, b_spec],
        out_shape=[o_shape, o_shape, b_shape, b_shape],
        compiler_params=_cparams(2),
        name="mem_kv_proj",
    )(mem, w_k, w_v)


def _proj_kernel(x_ref, w_ref, o_ref):
    o_ref[...] = jnp.dot(x_ref[...].astype(BF16), w_ref[...], preferred_element_type=F32)


def _pad_cols(w):
    if w.shape[-1] % (4 * V7X_MXU_COLS) != 0:
        return w
    return jnp.pad(w, [(0, 0)] * (w.ndim - 1) + [(0, V7X_MXU_COLS)])


def _proj(x, w, n, name):
    m, k = x.shape
    return pl.pallas_call(
        _proj_kernel,
        grid=(n // PROJ_COLS,),
        in_specs=[pl.BlockSpec((m, k), lambda j: (0, 0)),
                  pl.BlockSpec((k, PROJ_COLS), lambda j: (0, j))],
        out_specs=pl.BlockSpec((m, PROJ_COLS), lambda j: (0, j)),
        out_shape=jax.ShapeDtypeStruct((m, n), F32),
        compiler_params=_cparams(1),
        name=name,
    )(x, w)


def _mem_scores(mq_all, mk_ref):
    return [_dot_nt(mq_all[:, hm * MEM_HD:(hm + 1) * MEM_HD], mk_ref[:, hm * MEM_HD:(hm + 1) * MEM_HD])
            * (MEM_HD ** -0.5) for hm in range(MEM_HEADS)]


def _put_gated(bb_ref, gs_ref, col0, o):
    cols = slice(col0, col0 + o.shape[-1])
    bb_ref[:, cols] = (o * gs_ref[:, cols]).astype(BF16)


def _mem_values(weights, mv_ref, bb_ref, gs_ref, out_col0):
    for hm, (p, l) in enumerate(weights):
        lo = hm * MEM_HD
        _put_gated(bb_ref, gs_ref, out_col0 + lo, _dot(p, mv_ref[:, lo:lo + MEM_HD]) / l)


def _swa_prompt_kernel(relb_ref, sink_ref, h_ref, win_ref, wout_ref, mk_ref, mv_ref, lnw_ref, lnb_ref,
                       out_ref, kout_ref, vout_ref,
                       bias_scr, k_scr, v_scr, gs_scr, bb_scr, *, rows):
    b = pl.program_id(0)
    t = pl.program_id(1)
    w2 = 2 * WINDOW

    @pl.when(jnp.logical_and(b == 0, t == 0))
    def _build_bias():
        qi = lax.broadcasted_iota(jnp.int32, (WINDOW, w2), 0)
        kj = lax.broadcasted_iota(jnp.int32, (WINDOW, w2), 1)
        dist = qi + WINDOW - kj

        def per_head(hq, carry):
            acc = jnp.full((WINDOW, w2), MASKED, F32)
            for bk, rng in enumerate(_BUCKET_RANGES):
                if rng is not None:
                    hit = jnp.logical_and(dist >= rng[0], dist <= rng[1])
                    acc = jnp.where(hit, relb_ref[bk, hq], acc)
            bias_scr[0, hq] = acc
            bias_scr[1, hq] = jnp.where(kj < WINDOW, MASKED, acc)
            return carry

        lax.fori_loop(0, SWA_QH, per_head, 0)

    @pl.when(t == 0)
    def _reset_window():
        k_scr[0:WINDOW, :] = jnp.zeros((WINDOW, SWA_KV), BF16)
        v_scr[0:WINDOW, :] = jnp.zeros((WINDOW, SWA_KV), BF16)

    qkv_w = SWA_Q + 2 * SWA_KV
    n_blocks = rows // WINDOW
    n_gate = D_MODEL // GATE_CHUNK
    heads = [(g, j) for g in range(SWA_KVH) for j in range(SWA_GROUP)]
    xb, qkv, mq, scores, mem_s = {}, {}, {}, {}, {}

    def project(qb):
        r0 = qb * WINDOW
        xb[qb] = h_ref[0, r0:r0 + WINDOW, :].astype(BF16)
        qkv[qb] = jnp.dot(xb[qb], win_ref[:, :qkv_w], preferred_element_type=F32)
        mq[qb] = jnp.dot(xb[qb], win_ref[:, qkv_w:qkv_w + MEM_Q], preferred_element_type=F32)
        k = qkv[qb][:, SWA_Q:SWA_Q + SWA_KV]
        v = qkv[qb][:, SWA_Q + SWA_KV:]
        if qb == n_blocks - 1:
            kout_ref[0] = k
            vout_ref[0] = v
        k_scr[WINDOW + r0:w2 + r0, :] = k.astype(BF16)
        v_scr[WINDOW + r0:w2 + r0, :] = v.astype(BF16)

    def score(qb):
        r0 = qb * WINDOW
        table = jnp.where(t == 0, 1, 0) if qb == 0 else 0
        scores[qb] = []
        for idx, (g, j) in enumerate(heads):
            hq = g * SWA_GROUP + j
            q = qkv[qb][:, hq * SWA_HD:(hq + 1) * SWA_HD] * (SWA_HD ** -0.5)
            scores[qb].append(_dot_nt(q, k_scr[r0:r0 + w2, g * SWA_HD:(g + 1) * SWA_HD]) + bias_scr[table, hq])
            for chunk in range(idx * n_gate // len(heads), (idx + 1) * n_gate // len(heads)):
                lo = chunk * GATE_CHUNK
                gate = jnp.dot(xb[qb], win_ref[:, qkv_w + MEM_Q + lo:qkv_w + MEM_Q + lo + GATE_CHUNK],
                               preferred_element_type=F32)
                gs_scr[r0:r0 + WINDOW, lo:lo + GATE_CHUNK] = _silu(gate)
        mem_s[qb] = _mem_scores(mq[qb], mk_ref)

    def attend(qb):
        r0 = qb * WINDOW
        bb = bb_scr.at[r0:r0 + WINDOW, :]
        gs = gs_scr.at[r0:r0 + WINDOW, :]
        sinks = [sink_ref[g * SWA_GROUP + j] for g, j in heads]
        maxes = [jnp.maximum(jnp.max(s, axis=-1, keepdims=True), sk) for s, sk in zip(scores[qb], sinks)]
        mem_max = [jnp.max(s, axis=-1, keepdims=True) for s in mem_s[qb]]
        probs = [jnp.exp(s - m) for s, m in zip(scores[qb], maxes)]
        mem_p = [jnp.exp(s - m) for s, m in zip(mem_s[qb], mem_max)]
        weights = [(p, jnp.sum(p, axis=-1, keepdims=True) + jnp.exp(sk - m))
                   for p, m, sk in zip(probs, maxes, sinks)]
        mem_w = [(p, jnp.sum(p, axis=-1, keepdims=True)) for p in mem_p]
        for (g, j), (p, l) in zip(heads, weights):
            o = _dot(p, v_scr[r0:r0 + w2, g * SWA_HD:(g + 1) * SWA_HD]) / l
            _put_gated(bb, gs, (g * SWA_GROUP + j) * SWA_HD, o)
        _mem_values(mem_w, mv_ref, bb, gs, SWA_Q)

    def finish(qb):
        r0 = qb * WINDOW
        y = jnp.dot(bb_scr[r0:r0 + WINDOW, :], wout_ref[:, :D_MODEL], preferred_element_type=F32)
        out_ref[0, r0:r0 + WINDOW, :] = _layer_norm(DEEPNORM_ALPHA * h_ref[0, r0:r0 + WINDOW, :] + y,
                                                    lnw_ref[...], lnb_ref[...])

    project(0)
    score(0)
    for qb in range(n_blocks):
        if qb + 1 < n_blocks:
            project(qb + 1)
            score(qb + 1)
        attend(qb)
        finish(qb)

    k_scr[0:WINDOW, :] = k_scr[rows:rows + WINDOW, :]
    v_scr[0:WINDOW, :] = v_scr[rows:rows + WINDOW, :]


def _swa_prompt_layer(h, w_in, w_out, mk_bf, mv_bf, layer, rel_bias, sinks, ln_w, ln_b, rows):
    bsz, t, d = h.shape
    kernel = functools.partial(_swa_prompt_kernel, rows=rows)
    mem_spec = pl.BlockSpec((None, N_MEM, MEM_Q), lambda b, i: (layer, b, 0))
    return pl.pallas_call(
        kernel,
        grid=(bsz, t // rows),
        in_specs=[_SMEM, _SMEM,
                  pl.BlockSpec((1, rows, d), lambda b, i: (b, i, 0)),
                  _resident(w_in.shape),
                  _resident(w_out.shape),
                  mem_spec, mem_spec,
                  _resident((1, d)), _resident((1, d))],
        out_specs=[pl.BlockSpec((1, rows, d), lambda b, i: (b, i, 0)),
                   pl.BlockSpec((1, WINDOW, SWA_KV), lambda b, i: (b, 0, 0)),
                   pl.BlockSpec((1, WINDOW, SWA_KV), lambda b, i: (b, 0, 0))],
        out_shape=[jax.ShapeDtypeStruct((bsz, t, d), F32),
                   jax.ShapeDtypeStruct((bsz, WINDOW, SWA_KV), F32),
                   jax.ShapeDtypeStruct((bsz, WINDOW, SWA_KV), F32)],
        scratch_shapes=[pltpu.VMEM((2, SWA_QH, WINDOW, 2 * WINDOW), F32),
                        pltpu.VMEM((WINDOW + rows, SWA_KV), BF16),
                        pltpu.VMEM((WINDOW + rows, SWA_KV), BF16),
                        pltpu.VMEM((rows, d), F32),
                        pltpu.VMEM((rows, d), BF16)],
        compiler_params=_cparams(2),
        name="swa_prompt_layer",
    )(rel_bias, sinks, h, w_in, w_out, mk_bf, mv_bf, ln_w.reshape(1, d), ln_b.reshape(1, d))


HG_CHUNK = 128
HG_DIRECT = 8
HG_HEAD_COLS = 3 * HG_D


def _hgrn_head_major(w):
    d = w.shape[0]
    qfi = w[:, :3 * HG_K].reshape(d, 3, HG_HEADS, HG_D).transpose(0, 2, 1, 3).reshape(d, 3 * HG_K)
    return jnp.concatenate([qfi, w[:, 3 * HG_K:]], axis=1)


def _lower_bound(logits, layer):
    e = jnp.exp(logits - jnp.max(logits, axis=0, keepdims=True))
    sm = e / jnp.sum(e, axis=0, keepdims=True)
    return jnp.sum(sm[1:layer + 1, :], axis=0, keepdims=True)


def _cumsum_rows(tril_bf, g):
    g1 = g.astype(BF16)
    r1 = g - g1.astype(F32)
    g2 = r1.astype(BF16)
    g3 = (r1 - g2.astype(F32)).astype(BF16)
    d = jnp.dot(tril_bf, jnp.concatenate([g1, g2, g3], axis=1), preferred_element_type=F32)
    return (d[:, :HG_D] + d[:, HG_D:2 * HG_D]) + d[:, 2 * HG_D:]


def _hgrn_chunk_scores(qs, kk, gc):
    c = HG_CHUNK
    nb = c // HG_DIRECT
    q3 = qs.reshape(nb, HG_DIRECT, HG_D)
    g3 = gc.reshape(nb, HG_DIRECT, HG_D)
    k3 = kk.reshape(nb, HG_DIRECT, HG_D)
    lane = lax.broadcasted_iota(jnp.int32, (nb, HG_DIRECT, c), 2)
    blk = lax.broadcasted_iota(jnp.int32, (nb, HG_DIRECT, c), 0)
    trow = lax.broadcasted_iota(jnp.int32, (nb, HG_DIRECT, c), 1)
    a3 = jnp.zeros((nb, HG_DIRECT, c), F32)
    for sl in range(HG_DIRECT):
        e = jnp.exp(g3 - g3[:, sl:sl + 1, :])
        col = jnp.sum(e * q3 * k3[:, sl:sl + 1, :], axis=-1, keepdims=True)
        hit = jnp.logical_and(lane == blk * HG_DIRECT + sl, trow >= sl)
        a3 = jnp.where(hit, col, a3)
    a = a3.reshape(c, c)

    ti = lax.broadcasted_iota(jnp.int32, (c, c), 0)
    si = lax.broadcasted_iota(jnp.int32, (c, c), 1)
    ri = lax.broadcasted_iota(jnp.int32, (c, 1), 0)
    half = HG_DIRECT
    while half < c:
        span = 2 * half
        gsp = gc.reshape(c // span, span, HG_D)
        gref = jnp.broadcast_to(gsp[:, half - 1:half, :], gsp.shape).reshape(c, HG_D)
        upper = (ri % span) >= half
        eq = jnp.where(upper, jnp.exp(gc - gref), 0.0)
        ek = jnp.where(upper, 0.0, jnp.exp(gref - gc))
        a = a + jnp.where((ti // span) == (si // span), _dot_nt(qs * eq, kk * ek), 0.0)
        half = span
    return a


def _hgrn_prompt_kernel(h_ref, win_ref, wout_ref, lbl_ref, nw_ref, mk_ref, mv_ref, lnw_ref, lnb_ref,
                        out_ref, st_ref,
                        st_scr, gs_scr, bb_scr, *, layer, rows):
    t = pl.program_id(1)
    c = HG_CHUNK
    n_blocks = rows // c
    n_gate = D_MODEL // GATE_CHUNK
    n_pairs = HG_HEADS // 2
    heads = range(HG_HEADS)

    @pl.when(t == 0)
    def _reset_state():
        st_scr[...] = jnp.zeros(st_scr.shape, F32)

    lb_all = _lower_bound(lbl_ref[...], layer)
    ri = lax.broadcasted_iota(jnp.int32, (c, c), 0)
    ci = lax.broadcasted_iota(jnp.int32, (c, c), 1)
    tril_bf = jnp.where(ri >= ci, 1.0, 0.0).astype(BF16)

    xb, qs, kk, iv, gc, a, mem_s, y_cols = ({} for _ in range(8))

    def rows_of(blk):
        return slice(blk * c, (blk + 1) * c)

    def load(blk):
        xb[blk] = h_ref[0, rows_of(blk), :].astype(BF16)

    def project_pair(blk, pair):
        up = jnp.dot(xb[blk], win_ref[:, 2 * pair * HG_HEAD_COLS:2 * (pair + 1) * HG_HEAD_COLS],
                     preferred_element_type=F32)
        for hh in (2 * pair, 2 * pair + 1):
            uh = up[:, (hh - 2 * pair) * HG_HEAD_COLS:(hh - 2 * pair + 1) * HG_HEAD_COLS]
            lb = lb_all[:, hh * HG_D:(hh + 1) * HG_D]
            fg = lb + (1.0 - lb) * jax.nn.sigmoid(uh[:, HG_D:2 * HG_D])
            qs[blk, hh] = _silu(uh[:, :HG_D])
            iv[blk, hh] = uh[:, 2 * HG_D:]
            kk[blk, hh] = 1.0 - fg
            gc[blk, hh] = _cumsum_rows(tril_bf, jnp.log(fg))

    def project_mem(blk):
        mq = jnp.dot(xb[blk], win_ref[:, 3 * HG_K:3 * HG_K + MEM_Q], preferred_element_type=F32)
        mem_s[blk] = _mem_scores(mq, mk_ref)

    def project_gate(blk, chunk):
        lo = chunk * GATE_CHUNK
        gate = jnp.dot(xb[blk], win_ref[:, 3 * HG_K + MEM_Q + lo:3 * HG_K + MEM_Q + lo + GATE_CHUNK],
                       preferred_element_type=F32)
        gs_scr[rows_of(blk), lo:lo + GATE_CHUNK] = _silu(gate)

    def scores(blk, fillers):
        for hh in heads:
            a[blk, hh] = _hgrn_chunk_scores(qs[blk, hh], kk[blk, hh], gc[blk, hh])
            for job in fillers[hh * len(fillers) // HG_HEADS:(hh + 1) * len(fillers) // HG_HEADS]:
                job()

    def recur(blk):
        bb = bb_scr.at[rows_of(blk), :]
        gs = gs_scr.at[rows_of(blk), :]
        iv_t = [iv[blk, hh].T for hh in heads]
        for hh in heads:
            o = _dot(a[blk, hh], iv[blk, hh]) + _dot_nt(qs[blk, hh] * jnp.exp(gc[blk, hh]), st_scr[hh])
            ms = jnp.mean(o * o, axis=-1, keepdims=True)
            _put_gated(bb, gs, hh * HG_D, o * lax.rsqrt(ms + RMS_EPS) * nw_ref[:, hh * HG_D:(hh + 1) * HG_D])
        for hh in heads:
            g_last = gc[blk, hh][c - 1:c, :]
            kdec = kk[blk, hh] * jnp.exp(g_last - gc[blk, hh])
            st_scr[hh] = st_scr[hh] * jnp.exp(g_last) + _dot(iv_t[hh], kdec)
        mem_p = [jnp.exp(s - jnp.max(s, axis=-1, keepdims=True)) for s in mem_s[blk]]
        _mem_values([(p, jnp.sum(p, axis=-1, keepdims=True)) for p in mem_p], mv_ref, bb, gs, HG_K)

    def project_out(blk, chunk):
        lo = chunk * GATE_CHUNK
        y_cols[blk, chunk] = jnp.dot(bb_scr[rows_of(blk), :], wout_ref[:, lo:lo + GATE_CHUNK],
                                     preferred_element_type=F32)

    def finish(blk):
        y = jnp.concatenate([y_cols[blk, chunk] for chunk in range(n_gate)], axis=1)
        out_ref[0, rows_of(blk), :] = _layer_norm(DEEPNORM_ALPHA * h_ref[0, rows_of(blk), :] + y,
                                                  lnw_ref[...], lnb_ref[...])

    load(0)
    for pair in range(n_pairs):
        project_pair(0, pair)
    project_mem(0)
    for blk in range(n_blocks):
        fillers = []
        if blk + 1 < n_blocks:
            load(blk + 1)
            fillers += [functools.partial(project_pair, blk + 1, pair) for pair in range(n_pairs)]
            fillers.append(functools.partial(project_mem, blk + 1))
        if blk > 0:
            fillers += [functools.partial(project_out, blk - 1, chunk) for chunk in range(n_gate)]
        fillers += [functools.partial(project_gate, blk, chunk) for chunk in range(n_gate)]
        scores(blk, fillers)
        if blk > 0:
            finish(blk - 1)
        recur(blk)
    last = n_blocks - 1
    for chunk in range(n_gate):
        project_out(last, chunk)
    finish(last)

    @pl.when(t == pl.num_programs(1) - 1)
    def _emit_state():
        for hh in heads:
            st_ref[0, hh] = st_scr[hh].T


def _hgrn_prompt_layer(h, w_in_hm, w_out, mk_bf, mv_bf, layer, lb_logits, norm_w, ln_w, ln_b, rows):
    bsz, t, d = h.shape
    kernel = functools.partial(_hgrn_prompt_kernel, layer=layer, rows=rows)
    mem_spec = pl.BlockSpec((None, N_MEM, MEM_Q), lambda b, i: (layer, b, 0))
    return pl.pallas_call(
        kernel,
        grid=(bsz, t // rows),
        in_specs=[pl.BlockSpec((1, rows, d), lambda b, i: (b, i, 0)),
                  _resident(w_in_hm.shape),
                  _resident(w_out.shape),
                  _resident((DEPTH, HG_K)),
                  _resident((1, HG_K)),
                  mem_spec, mem_spec,
                  _resident((1, d)), _resident((1, d))],
        out_specs=[pl.BlockSpec((1, rows, d), lambda b, i: (b, i, 0)),
                   pl.BlockSpec((1, HG_HEADS, HG_D, HG_D), lambda b, i: (b, 0, 0, 0))],
        out_shape=[jax.ShapeDtypeStruct((bsz, t, d), F32),
                   jax.ShapeDtypeStruct((bsz, HG_HEADS, HG_D, HG_D), F32)],
        scratch_shapes=[pltpu.VMEM((HG_HEADS, HG_D, HG_D), F32),
                        pltpu.VMEM((rows, d), F32),
                        pltpu.VMEM((rows, d), BF16)],
        compiler_params=_cparams(2),
        name="hgrn_prompt_layer",
    )(h, w_in_hm, w_out, lb_logits, norm_w.reshape(1, HG_K), mk_bf, mv_bf, ln_w.reshape(1, d), ln_b.reshape(1, d))


REQS_PER_STEP = 4


def _heads_to_sublanes(row, col0, n, width):
    return jnp.concatenate([row[:, col0 + i * width:col0 + (i + 1) * width] for i in range(n)], axis=0)


def _sample_mem_attend(urow, col0, cmk_ref, cmv_ref, r):
    pieces = []
    for hm in range(MEM_HEADS):
        lo = col0 + hm * MEM_HD
        rows = pl.ds(hm, N_MEM, stride=MEM_HEADS)
        s = jnp.sum(cmk_ref[r, rows, :] * urow[:, lo:lo + MEM_HD], axis=-1, keepdims=True) * (MEM_HD ** -0.5)
        s = jnp.broadcast_to(s, (N_MEM, MEM_HD))
        p = jnp.exp(s - jnp.max(s, axis=0, keepdims=True))
        pieces.append(jnp.sum(p * cmv_ref[r, rows, :], axis=0, keepdims=True) / jnp.sum(p, axis=0, keepdims=True))
    return jnp.concatenate(pieces, axis=1)


def _sample_swa_kernel(relb_ref, sink_ref, u_ref, ck_ref, cv_ref, cmk_ref, cmv_ref,
                       br_ref, ko_ref, vo_ref, bias_scr, bias0_scr, sink_scr):
    nbuf = WINDOW

    @pl.when(pl.program_id(0) == 0)
    def _build_tables():
        dist = nbuf - lax.broadcasted_iota(jnp.int32, (V7X_SUBLANES, nbuf), 1)
        rid = lax.broadcasted_iota(jnp.int32, (V7X_SUBLANES, nbuf), 0)
        for g in range(SWA_KVH):
            acc = jnp.where(rid < SWA_GROUP, MASKED, 0.0).astype(F32)
            acc0 = jnp.zeros((V7X_SUBLANES, nbuf), F32)
            sk = jnp.zeros((V7X_SUBLANES, nbuf), F32)
            for j in range(SWA_GROUP):
                hq = g * SWA_GROUP + j
                sk = jnp.where(rid == j, sink_ref[hq], sk)
                acc0 = jnp.where(rid == j, relb_ref[0, hq], acc0)
                for bk, rng in enumerate(_BUCKET_RANGES):
                    if rng is not None:
                        hit = jnp.logical_and(rid == j, jnp.logical_and(dist >= rng[0], dist <= rng[1]))
                        acc = jnp.where(hit, relb_ref[bk, hq], acc)
            bias_scr[g] = acc
            bias0_scr[g] = acc0
            sink_scr[g] = sk

    g0 = SWA_Q + 2 * SWA_KV + MEM_Q
    pad = jnp.zeros((V7X_SUBLANES - SWA_GROUP, SWA_HD), F32)
    pairs = [(r, g) for r in range(REQS_PER_STEP) for g in range(SWA_KVH)]
    urows = [u_ref[r] for r in range(REQS_PER_STEP)]

    diag = (lax.broadcasted_iota(jnp.int32, (SWA_HD, SWA_HD), 0)
            == lax.broadcasted_iota(jnp.int32, (SWA_HD, SWA_HD), 1))
    newest = lax.broadcasted_iota(jnp.int32, (SWA_HD, nbuf), 1) == nbuf - 1

    def rolled(cache_t, new_row):
        new_col = jnp.sum(jnp.where(diag, new_row, 0.0), axis=-1, keepdims=True)
        return jnp.where(newest, new_col, pltpu.roll(cache_t, nbuf - 1, 1))

    scores = []
    for r, g in pairs:
        k_t = ck_ref[r, g]
        k_new = urows[r][:, SWA_Q + g * SWA_HD:SWA_Q + (g + 1) * SWA_HD]
        v_new = urows[r][:, SWA_Q + SWA_KV + g * SWA_HD:SWA_Q + SWA_KV + (g + 1) * SWA_HD]
        qg = jnp.concatenate(
            [urows[r][:, (g * SWA_GROUP + j) * SWA_HD:(g * SWA_GROUP + j + 1) * SWA_HD] for j in range(SWA_GROUP)]
            + [pad], axis=0) * (SWA_HD ** -0.5)
        s = _dot(qg, k_t) + bias_scr[g]
        s_new = jnp.sum(qg * k_new, axis=-1, keepdims=True) + bias0_scr[g][:, 0:1]
        scores.append((s, s_new, k_new, v_new))

    weights = []
    for (r, g), (s, s_new, _, v_new) in zip(pairs, scores):
        sink = sink_scr[g][:, 0:1]
        m = jnp.maximum(jnp.max(s, axis=-1, keepdims=True), jnp.maximum(s_new, sink))
        p = jnp.exp(s - m)
        p_new = jnp.exp(s_new - m)
        l = jnp.sum(p, axis=-1, keepdims=True) + p_new + jnp.exp(sink - m)
        weights.append((p, p_new * v_new, l))

    pieces = [[] for _ in range(REQS_PER_STEP)]
    for (r, g), (p, o_new, l) in zip(pairs, weights):
        o = (_dot_nt(p, cv_ref[r, g]) + o_new) / l
        pieces[r].extend(o[j:j + 1, :] for j in range(SWA_GROUP))

    for (r, g), (_, _, k_new, v_new) in zip(pairs, scores):
        ko_ref[r, g] = rolled(ck_ref[r, g], k_new)
        vo_ref[r, g] = rolled(cv_ref[r, g], v_new)

    for r in range(REQS_PER_STEP):
        mem_o = _sample_mem_attend(urows[r], SWA_Q + 2 * SWA_KV, cmk_ref, cmv_ref, r)
        br_ref[r] = jnp.concatenate(pieces[r] + [mem_o], axis=1) * _silu(urows[r][:, g0:g0 + D_MODEL])


def _cache_spec(layer, shape):
    return pl.BlockSpec((None, REQS_PER_STEP) + shape, lambda i: (layer, i) + (0,) * len(shape))


def _row_spec(width):
    return pl.BlockSpec((REQS_PER_STEP, 1, width), lambda i: (i, 0, 0))


def _head_minor_rows(cache):
    nl, ns, rows, heads, hd = cache.shape
    return cache.reshape(nl, ns, rows * heads, hd)


def _rows_minor(cache):
    return jnp.moveaxis(cache, -3, -1)


def _sample_swa_attend(u, cache_k, cache_v, j, cache_mk, cache_mv, layer, rel_bias, sinks):
    ns = u.shape[0]
    nbuf = cache_k.shape[-1]
    swa_shape = (SWA_KVH, SWA_HD, nbuf)
    mem_shape = (N_MEM * MEM_HEADS, MEM_HD)
    new_cache = pl.BlockSpec((REQS_PER_STEP,) + swa_shape, lambda i: (i, 0, 0, 0))
    br, ko, vo = pl.pallas_call(
        _sample_swa_kernel,
        grid=(ns // REQS_PER_STEP,),
        in_specs=[_SMEM, _SMEM, _row_spec(SWA_WIDTH), _cache_spec(j, swa_shape), _cache_spec(j, swa_shape),
                  _cache_spec(layer, mem_shape), _cache_spec(layer, mem_shape)],
        out_specs=[_row_spec(D_MODEL), new_cache, new_cache],
        out_shape=[jax.ShapeDtypeStruct((ns, 1, D_MODEL), F32),
                   jax.ShapeDtypeStruct((ns,) + swa_shape, F32),
                   jax.ShapeDtypeStruct((ns,) + swa_shape, F32)],
        scratch_shapes=[pltpu.VMEM((SWA_KVH, V7X_SUBLANES, nbuf), F32)] * 3,
        compiler_params=_cparams(1),
        name="sample_swa_attend",
    )(rel_bias, sinks, u.reshape(ns, 1, SWA_WIDTH), cache_k, cache_v, cache_mk, cache_mv)
    return br.reshape(ns, D_MODEL), ko, vo


def _sample_hgrn_gate_kernel(u_ref, mix_ref, cmk_ref, cmv_ref, br_ref):
    g0 = 3 * HG_K + MEM_Q
    for r in range(REQS_PER_STEP):
        urow = u_ref[r]
        mem_o = _sample_mem_attend(urow, 3 * HG_K, cmk_ref, cmv_ref, r)
        br_ref[r] = jnp.concatenate([mix_ref[r], mem_o], axis=1) * _silu(urow[:, g0:g0 + D_MODEL])


def _sample_hgrn_gate(u, mix, cache_mk, cache_mv, layer):
    ns = u.shape[0]
    mem_shape = (N_MEM * MEM_HEADS, MEM_HD)
    br = pl.pallas_call(
        _sample_hgrn_gate_kernel,
        grid=(ns // REQS_PER_STEP,),
        in_specs=[_row_spec(HG_WIDTH), _row_spec(HG_K), _cache_spec(layer, mem_shape), _cache_spec(layer, mem_shape)],
        out_specs=_row_spec(D_MODEL),
        out_shape=jax.ShapeDtypeStruct((ns, 1, D_MODEL), F32),
        compiler_params=_cparams(1),
        name="sample_hgrn_gate",
    )(u.reshape(ns, 1, HG_WIDTH), mix.reshape(ns, 1, HG_K), cache_mk, cache_mv)
    return br.reshape(ns, D_MODEL)


def _hgrn_sample_kernel(q_ref, f_ref, iv_ref, lbl_ref, nw_ref, st_ref, mix_ref, so_ref, o_scr, *, layer, ns):
    lb = _lower_bound(lbl_ref[...], layer)
    fg = lb + (1.0 - lb) * jax.nn.sigmoid(f_ref[...])
    kk_t = (1.0 - fg).T
    fg_t = fg.T
    qs_t = _silu(q_ref[...]).T
    for r in range(ns):
        state = fg_t[:, r:r + 1] * st_ref[r] + kk_t[:, r:r + 1] * iv_ref[r:r + 1, :]
        so_ref[r] = state
        o_scr[r:r + 1, :] = jnp.sum(qs_t[:, r:r + 1] * state, axis=0, keepdims=True)
    o = o_scr[...]
    ms = jnp.mean(o * o, axis=-1, keepdims=True)
    mix_ref[...] = o * lax.rsqrt(ms + RMS_EPS) * nw_ref[...]


def _hgrn_sample_step(u_hm, state, j, lb_logits, norm_w, layer):
    ns = u_hm.shape[0]
    col = lambda off: pl.BlockSpec((ns, HG_D), lambda h: (0, 3 * h + off))
    kernel = functools.partial(_hgrn_sample_kernel, layer=layer, ns=ns)
    return pl.pallas_call(
        kernel,
        grid=(HG_HEADS,),
        in_specs=[col(0), col(1), col(2),
                  pl.BlockSpec((DEPTH, HG_D), lambda h: (0, h)),
                  pl.BlockSpec((1, HG_D), lambda h: (0, h)),
                  pl.BlockSpec((None, ns, None, HG_D, HG_D), lambda h: (j, 0, h, 0, 0))],
        out_specs=[pl.BlockSpec((ns, HG_D), lambda h: (0, h)),
                   pl.BlockSpec((ns, None, HG_D, HG_D), lambda h: (0, h, 0, 0))],
        out_shape=[jax.ShapeDtypeStruct((ns, HG_K), F32),
                   jax.ShapeDtypeStruct(state.shape[1:], F32)],
        scratch_shapes=[pltpu.VMEM((ns, HG_D), F32)],
        compiler_params=_cparams(1),
        name="hgrn_sample_step",
    )(u_hm, u_hm, u_hm, lb_logits, norm_w.reshape(1, HG_K), state)


def _outproj_ln_kernel(br_ref, h_ref, wout_ref, lnw_ref, lnb_ref, out_ref):
    y = jnp.dot(br_ref[...].astype(BF16), wout_ref[:, :D_MODEL], preferred_element_type=F32)
    out_ref[...] = _layer_norm(DEEPNORM_ALPHA * h_ref[...] + y, lnw_ref[...], lnb_ref[...])


def _outproj_ln(branch, h, w_out, ln_w, ln_b):
    ns, d = h.shape
    full = lambda shape: pl.BlockSpec(shape, lambda i: (0, 0))
    return pl.pallas_call(
        _outproj_ln_kernel,
        grid=(1,),
        in_specs=[full((ns, d)), full((ns, d)), full(w_out.shape), full((1, d)), full((1, d))],
        out_specs=full((ns, d)),
        out_shape=jax.ShapeDtypeStruct((ns, d), F32),
        compiler_params=_cparams(1),
        name="outproj_ln",
    )(branch, h, w_out, ln_w.reshape(1, d), ln_b.reshape(1, d))


def kernel(x_prompt, x_sample, cache_mem_k, cache_mem_v, cache_swa_k, cache_swa_v, state_hgrn, mem_prompt, rel_bias, swa_w_in, swa_sinks, hg_w_in, hg_lb_logits, hg_norm_w, w_mem_k, w_mem_v, w_out, ln_w, ln_b):
    bp, t, d = x_prompt.shape
    ns = x_sample.shape[0]
    assert d == D_MODEL and x_sample.shape[1] == 1 and t % PROMPT_ROWS == 0
    assert cache_swa_k.shape[2] == WINDOW <= PAST_LEN
    assert w_mem_k.shape[0] == DEPTH

    wo = [(_pad_cols(w_out[i]) if i % 2 else w_out[i]).astype(BF16) for i in range(DEPTH)]
    mem_k_prompt, mem_v_prompt, mk_bf, mv_bf = _mem_kv_proj(mem_prompt, w_mem_k, w_mem_v)
    cmk, cmv = _head_minor_rows(cache_mem_k), _head_minor_rows(cache_mem_v)
    csk, csv = _rows_minor(cache_swa_k), _rows_minor(cache_swa_v)

    hp = x_prompt
    hs = x_sample.reshape(ns, d)
    swa_kp, swa_vp, swa_ks, swa_vs, hg_sp, hg_ss = [], [], [], [], [], []
    for i in range(DEPTH):
        j = i // 2
        if i % 2 == 0:
            w_in = _pad_cols(swa_w_in[j]).astype(BF16)
            hp, kw, vw = _swa_prompt_layer(hp, w_in, wo[i], mk_bf, mv_bf, i, rel_bias, swa_sinks[j],
                                           ln_w[i], ln_b[i], rows=PROMPT_ROWS)
            swa_kp.append(kw.reshape(bp, WINDOW, SWA_KVH, SWA_HD))
            swa_vp.append(vw.reshape(bp, WINDOW, SWA_KVH, SWA_HD))
            u = _proj(hs, w_in, SWA_WIDTH, "sample_swa_proj")
            br, ko, vo = _sample_swa_attend(u, csk, csv, j, cmk, cmv, i, rel_bias, swa_sinks[j])
            swa_ks.append(jnp.moveaxis(ko, -1, -3))
            swa_vs.append(jnp.moveaxis(vo, -1, -3))
        else:
            w_in = _pad_cols(_hgrn_head_major(hg_w_in[j])).astype(BF16)
            hp, st = _hgrn_prompt_layer(hp, w_in, wo[i], mk_bf, mv_bf, i, hg_lb_logits, hg_norm_w[j],
                                        ln_w[i], ln_b[i], rows=HG_ROWS)
            hg_sp.append(st)
            u = _proj(hs, w_in, HG_WIDTH, "sample_hgrn_proj")
            mix, so = _hgrn_sample_step(u, state_hgrn, j, hg_lb_logits, hg_norm_w[j], i)
            hg_ss.append(so)
            br = _sample_hgrn_gate(u, mix, cmk, cmv, i)
        hs = _outproj_ln(br, hs, wo[i], ln_w[i], ln_b[i])

    return (hp, hs.reshape(ns, 1, d), mem_k_prompt, mem_v_prompt,
            jnp.stack(swa_kp), jnp.stack(swa_vp), jnp.stack(hg_sp),
            jnp.stack(swa_ks), jnp.stack(swa_vs), jnp.stack(hg_ss))
```

```python
import functools
import math

import numpy as np
import jax
import jax.numpy as jnp
from jax import lax
from jax.experimental import pallas as pl
from jax.experimental.pallas import tpu as pltpu

F32 = jnp.float32
BF16 = jnp.bfloat16

D_MODEL = 2048
DEPTH = 2
PAST_LEN = 8192
N_MEM = 256
MEM_HEADS = 4
MEM_HD = 128
MEM_Q = MEM_HEADS * MEM_HD
WINDOW = 128
SWA_HD = 64
SWA_KVH = 4
SWA_GROUP = 6
SWA_QH = SWA_KVH * SWA_GROUP
SWA_Q = SWA_QH * SWA_HD
SWA_KV = SWA_KVH * SWA_HD
N_BUCKETS = 32
HG_HEADS = 12
HG_D = 128
HG_K = HG_HEADS * HG_D
SWA_WIDTH = SWA_Q + 2 * SWA_KV + MEM_Q + D_MODEL
HG_WIDTH = 3 * HG_K + MEM_Q + D_MODEL
DEEPNORM_ALPHA = (2.0 * DEPTH) ** 0.25
LN_EPS = 1e-5
RMS_EPS = 1e-6
MASKED = -1e30

V7X_SUBLANES = 8
V7X_LANES = 128
V7X_MXU_COLS = 256
V7X_VMEM_LIMIT = 60000 * 1024
PROJ_COLS = 512
GATE_CHUNK = 512
PROMPT_ROWS = 256


def _t5_bucket_ranges():
    n = np.arange(WINDOW)
    max_exact = N_BUCKETS // 2
    nf = np.maximum(n, 1).astype(np.float64)
    large = max_exact + (np.log(nf / max_exact) / math.log(WINDOW / max_exact)
                         * (N_BUCKETS - max_exact)).astype(np.int64)
    bucket = np.where(n < max_exact, n, np.minimum(large, N_BUCKETS - 1))
    ranges = []
    for b in range(N_BUCKETS):
        idx = np.nonzero(bucket == b)[0]
        ranges.append((int(idx[0]), int(idx[-1])) if idx.size else None)
    return ranges


_BUCKET_RANGES = _t5_bucket_ranges()


def _dot(a, b):
    return jnp.dot(a.astype(BF16), b.astype(BF16), preferred_element_type=F32)


def _dot_nt(a, b):
    return lax.dot_general(a.astype(BF16), b.astype(BF16), (((1,), (1,)), ((), ())),
                           preferred_element_type=F32)


def _layer_norm(z, w, b):
    mu = jnp.mean(z, axis=-1, keepdims=True)
    zc = z - mu
    var = jnp.mean(zc * zc, axis=-1, keepdims=True)
    return zc * lax.rsqrt(var + LN_EPS) * w + b


def _silu(x):
    return x * jax.nn.sigmoid(x)


def _cparams(n_axes):
    return pltpu.CompilerParams(dimension_semantics=("arbitrary",) * n_axes,
                                vmem_limit_bytes=V7X_VMEM_LIMIT)


def _resident(shape):
    nd = len(shape)
    return pl.BlockSpec(shape, lambda *_: (0,) * nd, pipeline_mode=pl.Buffered(1))


_SMEM = pl.BlockSpec(memory_space=pltpu.SMEM)


def _mem_kv_kernel(x_ref, wk_ref, wv_ref, ko_ref, vo_ref, kb_ref, vb_ref):
    xb = x_ref[...].astype(BF16)
    for w_ref, o_ref, b_ref in ((wk_ref, ko_ref, kb_ref), (wv_ref, vo_ref, vb_ref)):
        y = jnp.dot(xb, w_ref[...].astype(BF16), preferred_element_type=F32)
        b_ref[...] = y.astype(BF16)
        for hm in range(MEM_HEADS):
            o_ref[:, hm, :] = y[:, hm * MEM_HD:(hm + 1) * MEM_HD]


def _mem_kv_proj(mem, w_k, w_v):
    bsz, n, d = mem.shape
    nl = w_k.shape[0]
    w_spec = pl.BlockSpec((None, d, MEM_Q), lambda l, b: (l, 0, 0))
    o_spec = pl.BlockSpec((None, None, n, MEM_HEADS, MEM_HD), lambda l, b: (l, b, 0, 0, 0))
    b_spec = pl.BlockSpec((None, n, MEM_Q), lambda l, b: (l, b, 0))
    o_shape = jax.ShapeDtypeStruct((nl, bsz, n, MEM_HEADS, MEM_HD), F32)
    b_shape = jax.ShapeDtypeStruct((nl, bsz * n, MEM_Q), BF16)
    return pl.pallas_call(
        _mem_kv_kernel,
        grid=(nl, bsz),
        in_specs=[pl.BlockSpec((None, n, d), lambda l, b: (b, 0, 0)), w_spec, w_spec],
        out_specs=[o_spec, o_spec, b_spec, b_spec],
        out_shape=[o_shape, o_shape, b_shape, b_shape],
        compiler_params=_cparams(2),
        name="mem_kv_proj",
    )(mem, w_k, w_v)


def _proj_kernel(x_ref, w_ref, o_ref):
    o_ref[...] = jnp.dot(x_ref[...].astype(BF16), w_ref[...], preferred_element_type=F32)


def _pad_cols(w):
    if w.shape[-1] % (4 * V7X_MXU_COLS) != 0:
        return w
    return jnp.pad(w, [(0, 0)] * (w.ndim - 1) + [(0, V7X_MXU_COLS)])


def _proj(x, w, n, name):
    m, k = x.shape
    return pl.pallas_call(
        _proj_kernel,
        grid=(n // PROJ_COLS,),
        in_specs=[pl.BlockSpec((m, k), lambda j: (0, 0)),
                  pl.BlockSpec((k, PROJ_COLS), lambda j: (0, j))],
        out_specs=pl.BlockSpec((m, PROJ_COLS), lambda j: (0, j)),
        out_shape=jax.ShapeDtypeStruct((m, n), F32),
        compiler_params=_cparams(1),
        name=name,
    )(x, w)


def _mem_scores(mq_all, mk_ref):
    return [_dot_nt(mq_all[:, hm * MEM_HD:(hm + 1) * MEM_HD], mk_ref[:, hm * MEM_HD:(hm + 1) * MEM_HD])
            * (MEM_HD ** -0.5) for hm in range(MEM_HEADS)]


def _put_gated(bb_ref, gs_ref, col0, o):
    cols = slice(col0, col0 + o.shape[-1])
    bb_ref[:, cols] = (o * gs_ref[:, cols]).astype(BF16)


def _mem_values(weights, mv_ref, bb_ref, gs_ref, out_col0):
    for hm, (p, l) in enumerate(weights):
        lo = hm * MEM_HD
        _put_gated(bb_ref, gs_ref, out_col0 + lo, _dot(p, mv_ref[:, lo:lo + MEM_HD]) / l)


def _swa_prompt_kernel(relb_ref, sink_ref, h_ref, win_ref, wout_ref, mk_ref, mv_ref, lnw_ref, lnb_ref,
                       out_ref, kout_ref, vout_ref,
                       bias_scr, k_scr, v_scr, gs_scr, bb_scr, *, rows):
    b = pl.program_id(0)
    t = pl.program_id(1)
    w2 = 2 * WINDOW

    @pl.when(jnp.logical_and(b == 0, t == 0))
    def _build_bias():
        qi = lax.broadcasted_iota(jnp.int32, (WINDOW, w2), 0)
        kj = lax.broadcasted_iota(jnp.int32, (WINDOW, w2), 1)
        dist = qi + WINDOW - kj

        def per_head(hq, carry):
            acc = jnp.full((WINDOW, w2), MASKED, F32)
            for bk, rng in enumerate(_BUCKET_RANGES):
                if rng is not None:
                    hit = jnp.logical_and(dist >= rng[0], dist <= rng[1])
                    acc = jnp.where(hit, relb_ref[bk, hq], acc)
            bias_scr[0, hq] = acc
            bias_scr[1, hq] = jnp.where(kj < WINDOW, MASKED, acc)
            return carry

        lax.fori_loop(0, SWA_QH, per_head, 0)

    @pl.when(t == 0)
    def _reset_window():
        k_scr[0:WINDOW, :] = jnp.zeros((WINDOW, SWA_KV), BF16)
        v_scr[0:WINDOW, :] = jnp.zeros((WINDOW, SWA_KV), BF16)

    qkv_w = SWA_Q + 2 * SWA_KV
    n_blocks = rows // WINDOW
    n_gate = D_MODEL // GATE_CHUNK
    heads = [(g, j) for g in range(SWA_KVH) for j in range(SWA_GROUP)]
    xb, qkv, mq, scores, mem_s = {}, {}, {}, {}, {}

    def project(qb):
        r0 = qb * WINDOW
        xb[qb] = h_ref[0, r0:r0 + WINDOW, :].astype(BF16)
        qkv[qb] = jnp.dot(xb[qb], win_ref[:, :qkv_w], preferred_element_type=F32)
        mq[qb] = jnp.dot(xb[qb], win_ref[:, qkv_w:qkv_w + MEM_Q], preferred_element_type=F32)
        k = qkv[qb][:, SWA_Q:SWA_Q + SWA_KV]
        v = qkv[qb][:, SWA_Q + SWA_KV:]
        if qb == n_blocks - 1:
            kout_ref[0] = k
            vout_ref[0] = v
        k_scr[WINDOW + r0:w2 + r0, :] = k.astype(BF16)
        v_scr[WINDOW + r0:w2 + r0, :] = v.astype(BF16)

    def score(qb):
        r0 = qb * WINDOW
        table = jnp.where(t == 0, 1, 0) if qb == 0 else 0
        scores[qb] = []
        for idx, (g, j) in enumerate(heads):
            hq = g * SWA_GROUP + j
            q = qkv[qb][:, hq * SWA_HD:(hq + 1) * SWA_HD] * (SWA_HD ** -0.5)
            scores[qb].append(_dot_nt(q, k_scr[r0:r0 + w2, g * SWA_HD:(g + 1) * SWA_HD]) + bias_scr[table, hq])
            for chunk in range(idx * n_gate // len(heads), (idx + 1) * n_gate // len(heads)):
                lo = chunk * GATE_CHUNK
                gate = jnp.dot(xb[qb], win_ref[:, qkv_w + MEM_Q + lo:qkv_w + MEM_Q + lo + GATE_CHUNK],
                               preferred_element_type=F32)
                gs_scr[r0:r0 + WINDOW, lo:lo + GATE_CHUNK] = _silu(gate)
        mem_s[qb] = _mem_scores(mq[qb], mk_ref)

    def attend(qb):
        r0 = qb * WINDOW
        bb = bb_scr.at[r0:r0 + WINDOW, :]
        gs = gs_scr.at[r0:r0 + WINDOW, :]
        sinks = [sink_ref[g * SWA_GROUP + j] for g, j in heads]
        maxes = [jnp.maximum(jnp.max(s, axis=-1, keepdims=True), sk) for s, sk in zip(scores[qb], sinks)]
        mem_max = [jnp.max(s, axis=-1, keepdims=True) for s in mem_s[qb]]
        probs = [jnp.exp(s - m) for s, m in zip(scores[qb], maxes)]
        mem_p = [jnp.exp(s - m) for s, m in zip(mem_s[qb], mem_max)]
        weights = [(p, jnp.sum(p, axis=-1, keepdims=True) + jnp.exp(sk - m))
                   for p, m, sk in zip(probs, maxes, sinks)]
        mem_w = [(p, jnp.sum(p, axis=-1, keepdims=True)) for p in mem_p]
        for (g, j), (p, l) in zip(heads, weights):
            o = _dot(p, v_scr[r0:r0 + w2, g * SWA_HD:(g + 1) * SWA_HD]) / l
            _put_gated(bb, gs, (g * SWA_GROUP + j) * SWA_HD, o)
        _mem_values(mem_w, mv_ref, bb, gs, SWA_Q)

    def finish(qb):
        r0 = qb * WINDOW
        y = jnp.dot(bb_scr[r0:r0 + WINDOW, :], wout_ref[:, :D_MODEL], preferred_element_type=F32)
        out_ref[0, r0:r0 + WINDOW, :] = _layer_norm(DEEPNORM_ALPHA * h_ref[0, r0:r0 + WINDOW, :] + y,
                                                    lnw_ref[...], lnb_ref[...])

    project(0)
    score(0)
    for qb in range(n_blocks):
        if qb + 1 < n_blocks:
            project(qb + 1)
            score(qb + 1)
        attend(qb)
        finish(qb)

    k_scr[0:WINDOW, :] = k_scr[rows:rows + WINDOW, :]
    v_scr[0:WINDOW, :] = v_scr[rows:rows + WINDOW, :]


def _swa_prompt_layer(h, w_in, w_out, mk_bf, mv_bf, layer, rel_bias, sinks, ln_w, ln_b, rows):
    bsz, t, d = h.shape
    kernel = functools.partial(_swa_prompt_kernel, rows=rows)
    mem_spec = pl.BlockSpec((None, N_MEM, MEM_Q), lambda b, i: (layer, b, 0))
    return pl.pallas_call(
        kernel,
        grid=(bsz, t // rows),
        in_specs=[_SMEM, _SMEM,
                  pl.BlockSpec((1, rows, d), lambda b, i: (b, i, 0)),
                  _resident(w_in.shape),
                  _resident(w_out.shape),
                  mem_spec, mem_spec,
                  _resident((1, d)), _resident((1, d))],
        out_specs=[pl.BlockSpec((1, rows, d), lambda b, i: (b, i, 0)),
                   pl.BlockSpec((1, WINDOW, SWA_KV), lambda b, i: (b, 0, 0)),
                   pl.BlockSpec((1, WINDOW, SWA_KV), lambda b, i: (b, 0, 0))],
        out_shape=[jax.ShapeDtypeStruct((bsz, t, d), F32),
                   jax.ShapeDtypeStruct((bsz, WINDOW, SWA_KV), F32),
                   jax.ShapeDtypeStruct((bsz, WINDOW, SWA_KV), F32)],
        scratch_shapes=[pltpu.VMEM((2, SWA_QH, WINDOW, 2 * WINDOW), F32),
                        pltpu.VMEM((WINDOW + rows, SWA_KV), BF16),
                        pltpu.VMEM((WINDOW + rows, SWA_KV), BF16),
                        pltpu.VMEM((rows, d), F32),
                        pltpu.VMEM((rows, d), BF16)],
        compiler_params=_cparams(2),
        name="swa_prompt_layer",
    )(rel_bias, sinks, h, w_in, w_out, mk_bf, mv_bf, ln_w.reshape(1, d), ln_b.reshape(1, d))


HG_CHUNK = 128
HG_DIRECT = 8
HG_HEAD_COLS = 3 * HG_D


def _hgrn_head_major(w):
    d = w.shape[0]
    qfi = w[:, :3 * HG_K].reshape(d, 3, HG_HEADS, HG_D).transpose(0, 2, 1, 3).reshape(d, 3 * HG_K)
    return jnp.concatenate([qfi, w[:, 3 * HG_K:]], axis=1)


def _lower_bound(logits, layer):
    e = jnp.exp(logits - jnp.max(logits, axis=0, keepdims=True))
    sm = e / jnp.sum(e, axis=0, keepdims=True)
    return jnp.sum(sm[1:layer + 1, :], axis=0, keepdims=True)


def _cumsum_rows(tril_bf, g):
    g1 = g.astype(BF16)
    r1 = g - g1.astype(F32)
    g2 = r1.astype(BF16)
    g3 = (r1 - g2.astype(F32)).astype(BF16)
    d = jnp.dot(tril_bf, jnp.concatenate([g1, g2, g3], axis=1), preferred_element_type=F32)
    return (d[:, :HG_D] + d[:, HG_D:2 * HG_D]) + d[:, 2 * HG_D:]


def _hgrn_chunk_scores(qs, kk, gc):
    c = HG_CHUNK
    nb = c // HG_DIRECT
    q3 = qs.reshape(nb, HG_DIRECT, HG_D)
    g3 = gc.reshape(nb, HG_DIRECT, HG_D)
    k3 = kk.reshape(nb, HG_DIRECT, HG_D)
    lane = lax.broadcasted_iota(jnp.int32, (nb, HG_DIRECT, c), 2)
    blk = lax.broadcasted_iota(jnp.int32, (nb, HG_DIRECT, c), 0)
    trow = lax.broadcasted_iota(jnp.int32, (nb, HG_DIRECT, c), 1)
    a3 = jnp.zeros((nb, HG_DIRECT, c), F32)
    for sl in range(HG_DIRECT):
        e = jnp.exp(g3 - g3[:, sl:sl + 1, :])
        col = jnp.sum(e * q3 * k3[:, sl:sl + 1, :], axis=-1, keepdims=True)
        hit = jnp.logical_and(lane == blk * HG_DIRECT + sl, trow >= sl)
        a3 = jnp.where(hit, col, a3)
    a = a3.reshape(c, c)

    ti = lax.broadcasted_iota(jnp.int32, (c, c), 0)
    si = lax.broadcasted_iota(jnp.int32, (c, c), 1)
    ri = lax.broadcasted_iota(jnp.int32, (c, 1), 0)
    half = HG_DIRECT
    while half < c:
        span = 2 * half
        gsp = gc.reshape(c // span, span, HG_D)
        gref = jnp.broadcast_to(gsp[:, half - 1:half, :], gsp.shape).reshape(c, HG_D)
        upper = (ri % span) >= half
        eq = jnp.where(upper, jnp.exp(gc - gref), 0.0)
        ek = jnp.where(upper, 0.0, jnp.exp(gref - gc))
        a = a + jnp.where((ti // span) == (si // span), _dot_nt(qs * eq, kk * ek), 0.0)
        half = span
    return a


def _hgrn_prompt_kernel(h_ref, win_ref, wout_ref, lbl_ref, nw_ref, mk_ref, mv_ref, lnw_ref, lnb_ref,
                        out_ref, st_ref,
                        st_scr, br_scr, *, layer):
    t = pl.program_id(1)
    c = HG_CHUNK
    heads = range(HG_HEADS)

    @pl.when(t == 0)
    def _reset_state():
        st_scr[...] = jnp.zeros(st_scr.shape, F32)

    lb_all = _lower_bound(lbl_ref[...], layer)
    ri = lax.broadcasted_iota(jnp.int32, (c, c), 0)
    ci = lax.broadcasted_iota(jnp.int32, (c, c), 1)
    tril_bf = jnp.where(ri >= ci, 1.0, 0.0).astype(BF16)

    x = h_ref[0]
    xb = x.astype(BF16)
    uh = [jnp.dot(xb, win_ref[:, hh * HG_HEAD_COLS:(hh + 1) * HG_HEAD_COLS], preferred_element_type=F32)
          for hh in heads]
    ug = jnp.dot(xb, win_ref[:, 3 * HG_K:HG_WIDTH], preferred_element_type=F32)
    mem_s = _mem_scores(ug[:, :MEM_Q], mk_ref)

    qs, kk, iv, gc = [], [], [], []
    for hh in heads:
        lb = lb_all[:, hh * HG_D:(hh + 1) * HG_D]
        fg = lb + (1.0 - lb) * jax.nn.sigmoid(uh[hh][:, HG_D:2 * HG_D])
        qs.append(_silu(uh[hh][:, :HG_D]))
        iv.append(uh[hh][:, 2 * HG_D:])
        kk.append(1.0 - fg)
        gc.append(_cumsum_rows(tril_bf, jnp.log(fg)))
    mem_p = [jnp.exp(s - jnp.max(s, axis=-1, keepdims=True)) for s in mem_s]
    for hm, p in enumerate(mem_p):
        lo = hm * MEM_HD
        br_scr[:, HG_K + lo:HG_K + lo + MEM_HD] = (_dot(p, mv_ref[:, lo:lo + MEM_HD])
                                                   / jnp.sum(p, axis=-1, keepdims=True))

    a = [_hgrn_chunk_scores(qs[hh], kk[hh], gc[hh]) for hh in heads]

    iv_t = [iv[hh].T for hh in heads]
    for hh in heads:
        lhs = jnp.concatenate([a[hh], qs[hh] * jnp.exp(gc[hh])], axis=1)
        rhs = jnp.concatenate([iv_t[hh], st_scr[hh]], axis=1)
        o = _dot_nt(lhs, rhs)
        ms = jnp.mean(o * o, axis=-1, keepdims=True)
        br_scr[:, hh * HG_D:(hh + 1) * HG_D] = o * lax.rsqrt(ms + RMS_EPS) * nw_ref[:, hh * HG_D:(hh + 1) * HG_D]

    for hh in heads:
        g_last = gc[hh][c - 1:c, :]
        kdec = kk[hh] * jnp.exp(g_last - gc[hh])
        st_scr[hh] = st_scr[hh] * jnp.exp(g_last) + _dot(iv_t[hh], kdec)

    @pl.when(t == pl.num_programs(1) - 1)
    def _emit_state():
        for hh in heads:
            st_ref[0, hh] = st_scr[hh].T

    branch = br_scr[...] * _silu(ug[:, MEM_Q:])
    y = jnp.dot(branch.astype(BF16), wout_ref[:, :D_MODEL], preferred_element_type=F32)
    out_ref[0] = _layer_norm(DEEPNORM_ALPHA * x + y, lnw_ref[...], lnb_ref[...])


def _hgrn_prompt_layer(h, w_in_hm, w_out, mk_bf, mv_bf, layer, lb_logits, norm_w, ln_w, ln_b):
    bsz, t, d = h.shape
    rows = HG_CHUNK
    kernel = functools.partial(_hgrn_prompt_kernel, layer=layer)
    mem_spec = pl.BlockSpec((None, N_MEM, MEM_Q), lambda b, i: (layer, b, 0))
    return pl.pallas_call(
        kernel,
        grid=(bsz, t // rows),
        in_specs=[pl.BlockSpec((1, rows, d), lambda b, i: (b, i, 0)),
                  _resident(w_in_hm.shape),
                  _resident(w_out.shape),
                  _resident((DEPTH, HG_K)),
                  _resident((1, HG_K)),
                  mem_spec, mem_spec,
                  _resident((1, d)), _resident((1, d))],
        out_specs=[pl.BlockSpec((1, rows, d), lambda b, i: (b, i, 0)),
                   pl.BlockSpec((1, HG_HEADS, HG_D, HG_D), lambda b, i: (b, 0, 0, 0))],
        out_shape=[jax.ShapeDtypeStruct((bsz, t, d), F32),
                   jax.ShapeDtypeStruct((bsz, HG_HEADS, HG_D, HG_D), F32)],
        scratch_shapes=[pltpu.VMEM((HG_HEADS, HG_D, HG_D), F32),
                        pltpu.VMEM((rows, d), F32)],
        compiler_params=_cparams(2),
        name="hgrn_prompt_layer",
    )(h, w_in_hm, w_out, lb_logits, norm_w.reshape(1, HG_K), mk_bf, mv_bf, ln_w.reshape(1, d), ln_b.reshape(1, d))


REQS_PER_STEP = 4


def _heads_to_sublanes(row, col0, n, width):
    return jnp.concatenate([row[:, col0 + i * width:col0 + (i + 1) * width] for i in range(n)], axis=0)


def _sample_mem_attend(urow, col0, cmk_ref, cmv_ref, r):
    pieces = []
    for hm in range(MEM_HEADS):
        lo = col0 + hm * MEM_HD
        rows = pl.ds(hm, N_MEM, stride=MEM_HEADS)
        s = jnp.sum(cmk_ref[r, rows, :] * urow[:, lo:lo + MEM_HD], axis=-1, keepdims=True) * (MEM_HD ** -0.5)
        s = jnp.broadcast_to(s, (N_MEM, MEM_HD))
        p = jnp.exp(s - jnp.max(s, axis=0, keepdims=True))
        pieces.append(jnp.sum(p * cmv_ref[r, rows, :], axis=0, keepdims=True) / jnp.sum(p, axis=0, keepdims=True))
    return jnp.concatenate(pieces, axis=1)


def _sample_swa_kernel(relb_ref, sink_ref, u_ref, ck_ref, cv_ref, cmk_ref, cmv_ref,
                       br_ref, ko_ref, vo_ref, bias_scr, bias0_scr, sink_scr):
    nbuf = WINDOW

    @pl.when(pl.program_id(0) == 0)
    def _build_tables():
        dist = nbuf - lax.broadcasted_iota(jnp.int32, (V7X_SUBLANES, nbuf), 1)
        rid = lax.broadcasted_iota(jnp.int32, (V7X_SUBLANES, nbuf), 0)
        for g in range(SWA_KVH):
            acc = jnp.where(rid < SWA_GROUP, MASKED, 0.0).astype(F32)
            acc0 = jnp.zeros((V7X_SUBLANES, nbuf), F32)
            sk = jnp.zeros((V7X_SUBLANES, nbuf), F32)
            for j in range(SWA_GROUP):
                hq = g * SWA_GROUP + j
                sk = jnp.where(rid == j, sink_ref[hq], sk)
                acc0 = jnp.where(rid == j, relb_ref[0, hq], acc0)
                for bk, rng in enumerate(_BUCKET_RANGES):
                    if rng is not None:
                        hit = jnp.logical_and(rid == j, jnp.logical_and(dist >= rng[0], dist <= rng[1]))
                        acc = jnp.where(hit, relb_ref[bk, hq], acc)
            bias_scr[g] = acc
            bias0_scr[g] = acc0
            sink_scr[g] = sk

    g0 = SWA_Q + 2 * SWA_KV + MEM_Q
    pad = jnp.zeros((V7X_SUBLANES - SWA_GROUP, SWA_HD), F32)
    pairs = [(r, g) for r in range(REQS_PER_STEP) for g in range(SWA_KVH)]
    urows = [u_ref[r] for r in range(REQS_PER_STEP)]

    diag = (lax.broadcasted_iota(jnp.int32, (SWA_HD, SWA_HD), 0)
            == lax.broadcasted_iota(jnp.int32, (SWA_HD, SWA_HD), 1))
    newest = lax.broadcasted_iota(jnp.int32, (SWA_HD, nbuf), 1) == nbuf - 1

    def rolled(cache_t, new_row):
        new_col = jnp.sum(jnp.where(diag, new_row, 0.0), axis=-1, keepdims=True)
        return jnp.where(newest, new_col, pltpu.roll(cache_t, nbuf - 1, 1))

    scores = []
    for r, g in pairs:
        k_t = ck_ref[r, g]
        k_new = urows[r][:, SWA_Q + g * SWA_HD:SWA_Q + (g + 1) * SWA_HD]
        v_new = urows[r][:, SWA_Q + SWA_KV + g * SWA_HD:SWA_Q + SWA_KV + (g + 1) * SWA_HD]
        qg = jnp.concatenate(
            [urows[r][:, (g * SWA_GROUP + j) * SWA_HD:(g * SWA_GROUP + j + 1) * SWA_HD] for j in range(SWA_GROUP)]
            + [pad], axis=0) * (SWA_HD ** -0.5)
        s = _dot(qg, k_t) + bias_scr[g]
        s_new = jnp.sum(qg * k_new, axis=-1, keepdims=True) + bias0_scr[g][:, 0:1]
        scores.append((s, s_new, k_new, v_new))

    weights = []
    for (r, g), (s, s_new, _, v_new) in zip(pairs, scores):
        sink = sink_scr[g][:, 0:1]
        m = jnp.maximum(jnp.max(s, axis=-1, keepdims=True), jnp.maximum(s_new, sink))
        p = jnp.exp(s - m)
        p_new = jnp.exp(s_new - m)
        l = jnp.sum(p, axis=-1, keepdims=True) + p_new + jnp.exp(sink - m)
        weights.append((p, p_new * v_new, l))

    pieces = [[] for _ in range(REQS_PER_STEP)]
    for (r, g), (p, o_new, l) in zip(pairs, weights):
        o = (_dot_nt(p, cv_ref[r, g]) + o_new) / l
        pieces[r].extend(o[j:j + 1, :] for j in range(SWA_GROUP))

    for (r, g), (_, _, k_new, v_new) in zip(pairs, scores):
        ko_ref[r, g] = rolled(ck_ref[r, g], k_new)
        vo_ref[r, g] = rolled(cv_ref[r, g], v_new)

    for r in range(REQS_PER_STEP):
        mem_o = _sample_mem_attend(urows[r], SWA_Q + 2 * SWA_KV, cmk_ref, cmv_ref, r)
        br_ref[r] = jnp.concatenate(pieces[r] + [mem_o], axis=1) * _silu(urows[r][:, g0:g0 + D_MODEL])


def _cache_spec(layer, shape):
    return pl.BlockSpec((None, REQS_PER_STEP) + shape, lambda i: (layer, i) + (0,) * len(shape))


def _row_spec(width):
    return pl.BlockSpec((REQS_PER_STEP, 1, width), lambda i: (i, 0, 0))


def _head_minor_rows(cache):
    nl, ns, rows, heads, hd = cache.shape
    return cache.reshape(nl, ns, rows * heads, hd)


def _rows_minor(cache):
    return jnp.moveaxis(cache, -3, -1)


def _sample_swa_attend(u, cache_k, cache_v, j, cache_mk, cache_mv, layer, rel_bias, sinks):
    ns = u.shape[0]
    nbuf = cache_k.shape[-1]
    swa_shape = (SWA_KVH, SWA_HD, nbuf)
    mem_shape = (N_MEM * MEM_HEADS, MEM_HD)
    new_cache = pl.BlockSpec((REQS_PER_STEP,) + swa_shape, lambda i: (i, 0, 0, 0))
    br, ko, vo = pl.pallas_call(
        _sample_swa_kernel,
        grid=(ns // REQS_PER_STEP,),
        in_specs=[_SMEM, _SMEM, _row_spec(SWA_WIDTH), _cache_spec(j, swa_shape), _cache_spec(j, swa_shape),
                  _cache_spec(layer, mem_shape), _cache_spec(layer, mem_shape)],
        out_specs=[_row_spec(D_MODEL), new_cache, new_cache],
        out_shape=[jax.ShapeDtypeStruct((ns, 1, D_MODEL), F32),
                   jax.ShapeDtypeStruct((ns,) + swa_shape, F32),
                   jax.ShapeDtypeStruct((ns,) + swa_shape, F32)],
        scratch_shapes=[pltpu.VMEM((SWA_KVH, V7X_SUBLANES, nbuf), F32)] * 3,
        compiler_params=_cparams(1),
        name="sample_swa_attend",
    )(rel_bias, sinks, u.reshape(ns, 1, SWA_WIDTH), cache_k, cache_v, cache_mk, cache_mv)
    return br.reshape(ns, D_MODEL), ko, vo


def _sample_hgrn_gate_kernel(u_ref, mix_ref, cmk_ref, cmv_ref, br_ref):
    g0 = 3 * HG_K + MEM_Q
    for r in range(REQS_PER_STEP):
        urow = u_ref[r]
        mem_o = _sample_mem_attend(urow, 3 * HG_K, cmk_ref, cmv_ref, r)
        br_ref[r] = jnp.concatenate([mix_ref[r], mem_o], axis=1) * _silu(urow[:, g0:g0 + D_MODEL])


def _sample_hgrn_gate(u, mix, cache_mk, cache_mv, layer):
    ns = u.shape[0]
    mem_shape = (N_MEM * MEM_HEADS, MEM_HD)
    br = pl.pallas_call(
        _sample_hgrn_gate_kernel,
        grid=(ns // REQS_PER_STEP,),
        in_specs=[_row_spec(HG_WIDTH), _row_spec(HG_K), _cache_spec(layer, mem_shape), _cache_spec(layer, mem_shape)],
        out_specs=_row_spec(D_MODEL),
        out_shape=jax.ShapeDtypeStruct((ns, 1, D_MODEL), F32),
        compiler_params=_cparams(1),
        name="sample_hgrn_gate",
    )(u.reshape(ns, 1, HG_WIDTH), mix.reshape(ns, 1, HG_K), cache_mk, cache_mv)
    return br.reshape(ns, D_MODEL)


def _hgrn_sample_kernel(q_ref, f_ref, iv_ref, lbl_ref, nw_ref, st_ref, mix_ref, so_ref, o_scr, *, layer, ns):
    lb = _lower_bound(lbl_ref[...], layer)
    fg = lb + (1.0 - lb) * jax.nn.sigmoid(f_ref[...])
    kk_t = (1.0 - fg).T
    fg_t = fg.T
    qs_t = _silu(q_ref[...]).T
    for r in range(ns):
        state = fg_t[:, r:r + 1] * st_ref[r] + kk_t[:, r:r + 1] * iv_ref[r:r + 1, :]
        so_ref[r] = state
        o_scr[r:r + 1, :] = jnp.sum(qs_t[:, r:r + 1] * state, axis=0, keepdims=True)
    o = o_scr[...]
    ms = jnp.mean(o * o, axis=-1, keepdims=True)
    mix_ref[...] = o * lax.rsqrt(ms + RMS_EPS) * nw_ref[...]


def _hgrn_sample_step(u_hm, state, j, lb_logits, norm_w, layer):
    ns = u_hm.shape[0]
    col = lambda off: pl.BlockSpec((ns, HG_D), lambda h: (0, 3 * h + off))
    kernel = functools.partial(_hgrn_sample_kernel, layer=layer, ns=ns)
    return pl.pallas_call(
        kernel,
        grid=(HG_HEADS,),
        in_specs=[col(0), col(1), col(2),
                  pl.BlockSpec((DEPTH, HG_D), lambda h: (0, h)),
                  pl.BlockSpec((1, HG_D), lambda h: (0, h)),
                  pl.BlockSpec((None, ns, None, HG_D, HG_D), lambda h: (j, 0, h, 0, 0))],
        out_specs=[pl.BlockSpec((ns, HG_D), lambda h: (0, h)),
                   pl.BlockSpec((ns, None, HG_D, HG_D), lambda h: (0, h, 0, 0))],
        out_shape=[jax.ShapeDtypeStruct((ns, HG_K), F32),
                   jax.ShapeDtypeStruct(state.shape[1:], F32)],
        scratch_shapes=[pltpu.VMEM((ns, HG_D), F32)],
        compiler_params=_cparams(1),
        name="hgrn_sample_step",
    )(u_hm, u_hm, u_hm, lb_logits, norm_w.reshape(1, HG_K), state)


def _outproj_ln_kernel(br_ref, h_ref, wout_ref, lnw_ref, lnb_ref, out_ref):
    y = jnp.dot(br_ref[...].astype(BF16), wout_ref[:, :D_MODEL], preferred_element_type=F32)
    out_ref[...] = _layer_norm(DEEPNORM_ALPHA * h_ref[...] + y, lnw_ref[...], lnb_ref[...])


def _outproj_ln(branch, h, w_out, ln_w, ln_b):
    ns, d = h.shape
    full = lambda shape: pl.BlockSpec(shape, lambda i: (0, 0))
    return pl.pallas_call(
        _outproj_ln_kernel,
        grid=(1,),
        in_specs=[full((ns, d)), full((ns, d)), full(w_out.shape), full((1, d)), full((1, d))],
        out_specs=full((ns, d)),
        out_shape=jax.ShapeDtypeStruct((ns, d), F32),
        compiler_params=_cparams(1),
        name="outproj_ln",
    )(branch, h, w_out, ln_w.reshape(1, d), ln_b.reshape(1, d))


def kernel(x_prompt, x_sample, cache_mem_k, cache_mem_v, cache_swa_k, cache_swa_v, state_hgrn, mem_prompt, rel_bias, swa_w_in, swa_sinks, hg_w_in, hg_lb_logits, hg_norm_w, w_mem_k, w_mem_v, w_out, ln_w, ln_b):
    bp, t, d = x_prompt.shape
    ns = x_sample.shape[0]
    assert d == D_MODEL and x_sample.shape[1] == 1 and t % PROMPT_ROWS == 0
    assert cache_swa_k.shape[2] == WINDOW <= PAST_LEN
    assert w_mem_k.shape[0] == DEPTH

    wo = [(_pad_cols(w_out[i]) if i % 2 else w_out[i]).astype(BF16) for i in range(DEPTH)]
    mem_k_prompt, mem_v_prompt, mk_bf, mv_bf = _mem_kv_proj(mem_prompt, w_mem_k, w_mem_v)
    cmk, cmv = _head_minor_rows(cache_mem_k), _head_minor_rows(cache_mem_v)
    csk, csv = _rows_minor(cache_swa_k), _rows_minor(cache_swa_v)

    hp = x_prompt
    hs = x_sample.reshape(ns, d)
    swa_kp, swa_vp, swa_ks, swa_vs, hg_sp, hg_ss = [], [], [], [], [], []
    for i in range(DEPTH):
        j = i // 2
        if i % 2 == 0:
            w_in = _pad_cols(swa_w_in[j]).astype(BF16)
            hp, kw, vw = _swa_prompt_layer(hp, w_in, wo[i], mk_bf, mv_bf, i, rel_bias, swa_sinks[j],
                                           ln_w[i], ln_b[i], rows=PROMPT_ROWS)
            swa_kp.append(kw.reshape(bp, WINDOW, SWA_KVH, SWA_HD))
            swa_vp.append(vw.reshape(bp, WINDOW, SWA_KVH, SWA_HD))
            u = _proj(hs, w_in, SWA_WIDTH, "sample_swa_proj")
            br, ko, vo = _sample_swa_attend(u, csk, csv, j, cmk, cmv, i, rel_bias, swa_sinks[j])
            swa_ks.append(jnp.moveaxis(ko, -1, -3))
            swa_vs.append(jnp.moveaxis(vo, -1, -3))
        else:
            w_in = _pad_cols(_hgrn_head_major(hg_w_in[j])).astype(BF16)
            hp, st = _hgrn_prompt_layer(hp, w_in, wo[i], mk_bf, mv_bf, i, hg_lb_logits, hg_norm_w[j],
                                        ln_w[i], ln_b[i])
            hg_sp.append(st)
            u = _proj(hs, w_in, HG_WIDTH, "sample_hgrn_proj")
            mix, so = _hgrn_sample_step(u, state_hgrn, j, hg_lb_logits, hg_norm_w[j], i)
            hg_ss.append(so)
            br = _sample_hgrn_gate(u, mix, cmk, cmv, i)
        hs = _outproj_ln(br, hs, wo[i], ln_w[i], ln_b[i])

    return (hp, hs.reshape(ns, 1, d), mem_k_prompt, mem_v_prompt,
            jnp.stack(swa_kp), jnp.stack(swa_vp), jnp.stack(hg_sp),
            jnp.stack(swa_ks), jnp.stack(swa_vs), jnp.stack(hg_ss))
```

```python
import functools
import math

import numpy as np
import jax
import jax.numpy as jnp
from jax import lax
from jax.experimental import pallas as pl
from jax.experimental.pallas import tpu as pltpu

F32 = jnp.float32
BF16 = jnp.bfloat16

D_MODEL = 2048
DEPTH = 2
PAST_LEN = 8192
N_MEM = 256
MEM_HEADS = 4
MEM_HD = 128
MEM_Q = MEM_HEADS * MEM_HD
WINDOW = 128
SWA_HD = 64
SWA_KVH = 4
SWA_GROUP = 6
SWA_QH = SWA_KVH * SWA_GROUP
SWA_Q = SWA_QH * SWA_HD
SWA_KV = SWA_KVH * SWA_HD
N_BUCKETS = 32
HG_HEADS = 12
HG_D = 128
HG_K = HG_HEADS * HG_D
SWA_WIDTH = SWA_Q + 2 * SWA_KV + MEM_Q + D_MODEL
HG_WIDTH = 3 * HG_K + MEM_Q + D_MODEL
DEEPNORM_ALPHA = (2.0 * DEPTH) ** 0.25
LN_EPS = 1e-5
RMS_EPS = 1e-6
MASKED = -1e30

V7X_SUBLANES = 8
V7X_LANES = 128
V7X_MXU_COLS = 256
V7X_VMEM_LIMIT = 60000 * 1024
PROJ_COLS = 512
GATE_CHUNK = 512
PROMPT_ROWS = 256


def _t5_bucket_ranges():
    n = np.arange(WINDOW)
    max_exact = N_BUCKETS // 2
    nf = np.maximum(n, 1).astype(np.float64)
    large = max_exact + (np.log(nf / max_exact) / math.log(WINDOW / max_exact)
                         * (N_BUCKETS - max_exact)).astype(np.int64)
    bucket = np.where(n < max_exact, n, np.minimum(large, N_BUCKETS - 1))
    ranges = []
    for b in range(N_BUCKETS):
        idx = np.nonzero(bucket == b)[0]
        ranges.append((int(idx[0]), int(idx[-1])) if idx.size else None)
    return ranges


_BUCKET_RANGES = _t5_bucket_ranges()


def _dot(a, b):
    return jnp.dot(a.astype(BF16), b.astype(BF16), preferred_element_type=F32)


def _dot_nt(a, b):
    return lax.dot_general(a.astype(BF16), b.astype(BF16), (((1,), (1,)), ((), ())),
                           preferred_element_type=F32)


def _layer_norm(z, w, b):
    mu = jnp.mean(z, axis=-1, keepdims=True)
    zc = z - mu
    var = jnp.mean(zc * zc, axis=-1, keepdims=True)
    return zc * lax.rsqrt(var + LN_EPS) * w + b


def _silu(x):
    return x * jax.nn.sigmoid(x)


def _cparams(n_axes):
    return pltpu.CompilerParams(dimension_semantics=("arbitrary",) * n_axes,
                                vmem_limit_bytes=V7X_VMEM_LIMIT)


def _resident(shape):
    nd = len(shape)
    return pl.BlockSpec(shape, lambda *_: (0,) * nd, pipeline_mode=pl.Buffered(1))


_SMEM = pl.BlockSpec(memory_space=pltpu.SMEM)


def _mem_kv_kernel(x_ref, wk_ref, wv_ref, ko_ref, vo_ref, kb_ref, vb_ref):
    xb = x_ref[...].astype(BF16)
    for w_ref, o_ref, b_ref in ((wk_ref, ko_ref, kb_ref), (wv_ref, vo_ref, vb_ref)):
        y = jnp.dot(xb, w_ref[...].astype(BF16), preferred_element_type=F32)
        b_ref[...] = y.astype(BF16)
        for hm in range(MEM_HEADS):
            o_ref[:, hm, :] = y[:, hm * MEM_HD:(hm + 1) * MEM_HD]


def _mem_kv_proj(mem, w_k, w_v):
    bsz, n, d = mem.shape
    nl = w_k.shape[0]
    w_spec = pl.BlockSpec((None, d, MEM_Q), lambda l, b: (l, 0, 0))
    o_spec = pl.BlockSpec((None, None, n, MEM_HEADS, MEM_HD), lambda l, b: (l, b, 0, 0, 0))
    b_spec = pl.BlockSpec((None, n, MEM_Q), lambda l, b: (l, b, 0))
    o_shape = jax.ShapeDtypeStruct((nl, bsz, n, MEM_HEADS, MEM_HD), F32)
    b_shape = jax.ShapeDtypeStruct((nl, bsz * n, MEM_Q), BF16)
    return pl.pallas_call(
        _mem_kv_kernel,
        grid=(nl, bsz),
        in_specs=[pl.BlockSpec((None, n, d), lambda l, b: (b, 0, 0)), w_spec, w_spec],
        out_specs=[o_spec, o_spec, b_spec, b_spec],
        out_shape=[o_shape, o_shape, b_shape, b_shape],
        compiler_params=_cparams(2),
        name="mem_kv_proj",
    )(mem, w_k, w_v)


def _proj_kernel(x_ref, w_ref, o_ref):
    o_ref[...] = jnp.dot(x_ref[...].astype(BF16), w_ref[...], preferred_element_type=F32)


def _pad_cols(w):
    if w.shape[-1] % (4 * V7X_MXU_COLS) != 0:
        return w
    return jnp.pad(w, [(0, 0)] * (w.ndim - 1) + [(0, V7X_MXU_COLS)])


def _proj(x, w, n, name):
    m, k = x.shape
    return pl.pallas_call(
        _proj_kernel,
        grid=(n // PROJ_COLS,),
        in_specs=[pl.BlockSpec((m, k), lambda j: (0, 0)),
                  pl.BlockSpec((k, PROJ_COLS), lambda j: (0, j))],
        out_specs=pl.BlockSpec((m, PROJ_COLS), lambda j: (0, j)),
        out_shape=jax.ShapeDtypeStruct((m, n), F32),
        compiler_params=_cparams(1),
        name=name,
    )(x, w)


def _mem_scores(mq_all, mk_ref):
    return [_dot_nt(mq_all[:, hm * MEM_HD:(hm + 1) * MEM_HD], mk_ref[:, hm * MEM_HD:(hm + 1) * MEM_HD])
            * (MEM_HD ** -0.5) for hm in range(MEM_HEADS)]


def _put_gated(bb_ref, gs_ref, col0, o):
    cols = slice(col0, col0 + o.shape[-1])
    bb_ref[:, cols] = (o * gs_ref[:, cols]).astype(BF16)


def _mem_values(weights, mv_ref, bb_ref, gs_ref, out_col0):
    for hm, (p, l) in enumerate(weights):
        lo = hm * MEM_HD
        _put_gated(bb_ref, gs_ref, out_col0 + lo, _dot(p, mv_ref[:, lo:lo + MEM_HD]) / l)


def _swa_prompt_kernel(relb_ref, sink_ref, h_ref, win_ref, wout_ref, mk_ref, mv_ref, lnw_ref, lnb_ref,
                       out_ref, kout_ref, vout_ref,
                       bias_scr, k_scr, v_scr, gs_scr, bb_scr, *, rows):
    b = pl.program_id(0)
    t = pl.program_id(1)
    w2 = 2 * WINDOW

    @pl.when(jnp.logical_and(b == 0, t == 0))
    def _build_bias():
        qi = lax.broadcasted_iota(jnp.int32, (WINDOW, w2), 0)
        kj = lax.broadcasted_iota(jnp.int32, (WINDOW, w2), 1)
        dist = qi + WINDOW - kj

        def per_head(hq, carry):
            acc = jnp.full((WINDOW, w2), MASKED, F32)
            for bk, rng in enumerate(_BUCKET_RANGES):
                if rng is not None:
                    hit = jnp.logical_and(dist >= rng[0], dist <= rng[1])
                    acc = jnp.where(hit, relb_ref[bk, hq], acc)
            bias_scr[0, hq] = acc
            bias_scr[1, hq] = jnp.where(kj < WINDOW, MASKED, acc)
            return carry

        lax.fori_loop(0, SWA_QH, per_head, 0)

    @pl.when(t == 0)
    def _reset_window():
        k_scr[0:WINDOW, :] = jnp.zeros((WINDOW, SWA_KV), BF16)
        v_scr[0:WINDOW, :] = jnp.zeros((WINDOW, SWA_KV), BF16)

    qkv_w = SWA_Q + 2 * SWA_KV
    n_blocks = rows // WINDOW
    n_gate = D_MODEL // GATE_CHUNK
    heads = [(g, j) for g in range(SWA_KVH) for j in range(SWA_GROUP)]
    xb, qkv, mq, scores, mem_s = {}, {}, {}, {}, {}

    def project(qb):
        r0 = qb * WINDOW
        xb[qb] = h_ref[0, r0:r0 + WINDOW, :].astype(BF16)
        qkv[qb] = jnp.dot(xb[qb], win_ref[:, :qkv_w], preferred_element_type=F32)
        mq[qb] = jnp.dot(xb[qb], win_ref[:, qkv_w:qkv_w + MEM_Q], preferred_element_type=F32)
        k = qkv[qb][:, SWA_Q:SWA_Q + SWA_KV]
        v = qkv[qb][:, SWA_Q + SWA_KV:]
        if qb == n_blocks - 1:
            kout_ref[0] = k
            vout_ref[0] = v
        k_scr[WINDOW + r0:w2 + r0, :] = k.astype(BF16)
        v_scr[WINDOW + r0:w2 + r0, :] = v.astype(BF16)

    def score(qb):
        r0 = qb * WINDOW
        table = jnp.where(t == 0, 1, 0) if qb == 0 else 0
        scores[qb] = []
        for idx, (g, j) in enumerate(heads):
            hq = g * SWA_GROUP + j
            q = qkv[qb][:, hq * SWA_HD:(hq + 1) * SWA_HD] * (SWA_HD ** -0.5)
            scores[qb].append(_dot_nt(q, k_scr[r0:r0 + w2, g * SWA_HD:(g + 1) * SWA_HD]) + bias_scr[table, hq])
            for chunk in range(idx * n_gate // len(heads), (idx + 1) * n_gate // len(heads)):
                lo = chunk * GATE_CHUNK
                gate = jnp.dot(xb[qb], win_ref[:, qkv_w + MEM_Q + lo:qkv_w + MEM_Q + lo + GATE_CHUNK],
                               preferred_element_type=F32)
                gs_scr[r0:r0 + WINDOW, lo:lo + GATE_CHUNK] = _silu(gate)
        mem_s[qb] = _mem_scores(mq[qb], mk_ref)

    def attend(qb):
        r0 = qb * WINDOW
        bb = bb_scr.at[r0:r0 + WINDOW, :]
        gs = gs_scr.at[r0:r0 + WINDOW, :]
        sinks = [sink_ref[g * SWA_GROUP + j] for g, j in heads]
        maxes = [jnp.maximum(jnp.max(s, axis=-1, keepdims=True), sk) for s, sk in zip(scores[qb], sinks)]
        mem_max = [jnp.max(s, axis=-1, keepdims=True) for s in mem_s[qb]]
        probs = [jnp.exp(s - m) for s, m in zip(scores[qb], maxes)]
        mem_p = [jnp.exp(s - m) for s, m in zip(mem_s[qb], mem_max)]
        weights = [(p, jnp.sum(p, axis=-1, keepdims=True) + jnp.exp(sk - m))
                   for p, m, sk in zip(probs, maxes, sinks)]
        mem_w = [(p, jnp.sum(p, axis=-1, keepdims=True)) for p in mem_p]
        for (g, j), (p, l) in zip(heads, weights):
            o = _dot(p, v_scr[r0:r0 + w2, g * SWA_HD:(g + 1) * SWA_HD]) / l
            _put_gated(bb, gs, (g * SWA_GROUP + j) * SWA_HD, o)
        _mem_values(mem_w, mv_ref, bb, gs, SWA_Q)

    def finish(qb):
        r0 = qb * WINDOW
        y = jnp.dot(bb_scr[r0:r0 + WINDOW, :], wout_ref[:, :D_MODEL], preferred_element_type=F32)
        out_ref[0, r0:r0 + WINDOW, :] = _layer_norm(DEEPNORM_ALPHA * h_ref[0, r0:r0 + WINDOW, :] + y,
                                                    lnw_ref[...], lnb_ref[...])

    project(0)
    score(0)
    for qb in range(n_blocks):
        if qb + 1 < n_blocks:
            project(qb + 1)
            score(qb + 1)
        attend(qb)
        finish(qb)

    k_scr[0:WINDOW, :] = k_scr[rows:rows + WINDOW, :]
    v_scr[0:WINDOW, :] = v_scr[rows:rows + WINDOW, :]


def _swa_prompt_layer(h, w_in, w_out, mk_bf, mv_bf, layer, rel_bias, sinks, ln_w, ln_b, rows):
    bsz, t, d = h.shape
    kernel = functools.partial(_swa_prompt_kernel, rows=rows)
    mem_spec = pl.BlockSpec((None, N_MEM, MEM_Q), lambda b, i: (layer, b, 0))
    return pl.pallas_call(
        kernel,
        grid=(bsz, t // rows),
        in_specs=[_SMEM, _SMEM,
                  pl.BlockSpec((1, rows, d), lambda b, i: (b, i, 0)),
                  _resident(w_in.shape),
                  _resident(w_out.shape),
                  mem_spec, mem_spec,
                  _resident((1, d)), _resident((1, d))],
        out_specs=[pl.BlockSpec((1, rows, d), lambda b, i: (b, i, 0)),
                   pl.BlockSpec((1, WINDOW, SWA_KV), lambda b, i: (b, 0, 0)),
                   pl.BlockSpec((1, WINDOW, SWA_KV), lambda b, i: (b, 0, 0))],
        out_shape=[jax.ShapeDtypeStruct((bsz, t, d), F32),
                   jax.ShapeDtypeStruct((bsz, WINDOW, SWA_KV), F32),
                   jax.ShapeDtypeStruct((bsz, WINDOW, SWA_KV), F32)],
        scratch_shapes=[pltpu.VMEM((2, SWA_QH, WINDOW, 2 * WINDOW), F32),
                        pltpu.VMEM((WINDOW + rows, SWA_KV), BF16),
                        pltpu.VMEM((WINDOW + rows, SWA_KV), BF16),
                        pltpu.VMEM((rows, d), F32),
                        pltpu.VMEM((rows, d), BF16)],
        compiler_params=_cparams(2),
        name="swa_prompt_layer",
    )(rel_bias, sinks, h, w_in, w_out, mk_bf, mv_bf, ln_w.reshape(1, d), ln_b.reshape(1, d))


HG_CHUNK = 128
HG_DIRECT = 8


def _lower_bound(logits, layer):
    e = jnp.exp(logits - jnp.max(logits, axis=0, keepdims=True))
    sm = e / jnp.sum(e, axis=0, keepdims=True)
    return jnp.sum(sm[1:layer + 1, :], axis=0, keepdims=True)


def _cumsum_rows(tril_bf, g):
    g1 = g.astype(BF16)
    r1 = g - g1.astype(F32)
    g2 = r1.astype(BF16)
    g3 = (r1 - g2.astype(F32)).astype(BF16)
    d = jnp.dot(tril_bf, jnp.concatenate([g1, g2, g3], axis=1), preferred_element_type=F32)
    return (d[:, :HG_D] + d[:, HG_D:2 * HG_D]) + d[:, 2 * HG_D:]


def _hgrn_chunk_scores(qs, kk, gc):
    c = HG_CHUNK
    nb = c // HG_DIRECT
    q3 = qs.reshape(nb, HG_DIRECT, HG_D)
    g3 = gc.reshape(nb, HG_DIRECT, HG_D)
    k3 = kk.reshape(nb, HG_DIRECT, HG_D)
    lane = lax.broadcasted_iota(jnp.int32, (nb, HG_DIRECT, c), 2)
    blk = lax.broadcasted_iota(jnp.int32, (nb, HG_DIRECT, c), 0)
    trow = lax.broadcasted_iota(jnp.int32, (nb, HG_DIRECT, c), 1)
    a3 = jnp.zeros((nb, HG_DIRECT, c), F32)
    for sl in range(HG_DIRECT):
        e = jnp.exp2(g3 - g3[:, sl:sl + 1, :])
        col = jnp.sum(e * q3 * k3[:, sl:sl + 1, :], axis=-1, keepdims=True)
        hit = jnp.logical_and(lane == blk * HG_DIRECT + sl, trow >= sl)
        a3 = jnp.where(hit, col, a3)
    a = a3.reshape(c, c)

    ti = lax.broadcasted_iota(jnp.int32, (c, c), 0)
    si = lax.broadcasted_iota(jnp.int32, (c, c), 1)
    ri = lax.broadcasted_iota(jnp.int32, (c, 1), 0)
    half = HG_DIRECT
    while half < c:
        span = 2 * half
        gsp = gc.reshape(c // span, span, HG_D)
        gref = jnp.broadcast_to(gsp[:, half - 1:half, :], gsp.shape).reshape(c, HG_D)
        upper = (ri % span) >= half
        eq = jnp.where(upper, jnp.exp2(gc - gref), 0.0)
        ek = jnp.where(upper, 0.0, jnp.exp2(gref - gc))
        a = a + jnp.where((ti // span) == (si // span), _dot_nt(qs * eq, kk * ek), 0.0)
        half = span
    return a


def _hgrn_prompt_kernel(h_ref, win_ref, wout_ref, lbl_ref, nw_ref, mk_ref, mv_ref, lnw_ref, lnb_ref,
                        out_ref, st_ref,
                        st_scr, br_scr, *, layer):
    t = pl.program_id(1)
    c = HG_CHUNK
    heads = range(HG_HEADS)

    @pl.when(t == 0)
    def _reset_state():
        st_scr[...] = jnp.zeros(st_scr.shape, F32)

    lb_all = _lower_bound(lbl_ref[...], layer)
    ri = lax.broadcasted_iota(jnp.int32, (c, c), 0)
    ci = lax.broadcasted_iota(jnp.int32, (c, c), 1)
    tril_bf = jnp.where(ri >= ci, 1.0, 0.0).astype(BF16)

    x = h_ref[0]
    xb = x.astype(BF16)
    pair_w = 2 * HG_D
    parts = [[jnp.dot(xb, win_ref[:, part * HG_K + pair * pair_w:part * HG_K + (pair + 1) * pair_w],
                      preferred_element_type=F32) for part in range(3)] for pair in range(HG_HEADS // 2)]
    ug = jnp.dot(xb, win_ref[:, 3 * HG_K:HG_WIDTH], preferred_element_type=F32)
    mem_s = _mem_scores(ug[:, :MEM_Q], mk_ref)

    qs, kk, iv, gc = [], [], [], []
    for hh in heads:
        cols = slice((hh % 2) * HG_D, (hh % 2 + 1) * HG_D)
        uq, uf, ui = (part[:, cols] for part in parts[hh // 2])
        lb = lb_all[:, hh * HG_D:(hh + 1) * HG_D]
        fg = lb + (1.0 - lb) * jax.nn.sigmoid(uf)
        qs.append(_silu(uq))
        iv.append(ui)
        kk.append(1.0 - fg)
        gc.append(_cumsum_rows(tril_bf, jnp.log2(fg)))
    mem_p = [jnp.exp(s - jnp.max(s, axis=-1, keepdims=True)) for s in mem_s]
    for hm, p in enumerate(mem_p):
        lo = hm * MEM_HD
        br_scr[:, HG_K + lo:HG_K + lo + MEM_HD] = (_dot(p, mv_ref[:, lo:lo + MEM_HD])
                                                   / jnp.sum(p, axis=-1, keepdims=True))

    a = [_hgrn_chunk_scores(qs[hh], kk[hh], gc[hh]) for hh in heads]

    iv_t = [iv[hh].T for hh in heads]
    for hh in heads:
        lhs = jnp.concatenate([a[hh], qs[hh] * jnp.exp2(gc[hh])], axis=1)
        rhs = jnp.concatenate([iv_t[hh], st_scr[hh]], axis=1)
        o = _dot_nt(lhs, rhs)
        ms = jnp.mean(o * o, axis=-1, keepdims=True)
        br_scr[:, hh * HG_D:(hh + 1) * HG_D] = o * lax.rsqrt(ms + RMS_EPS) * nw_ref[:, hh * HG_D:(hh + 1) * HG_D]

    for hh in heads:
        g_last = gc[hh][c - 1:c, :]
        kdec = kk[hh] * jnp.exp2(g_last - gc[hh])
        st_scr[hh] = st_scr[hh] * jnp.exp2(g_last) + _dot(iv_t[hh], kdec)

    @pl.when(t == pl.num_programs(1) - 1)
    def _emit_state():
        for hh in heads:
            st_ref[0, hh] = st_scr[hh].T

    branch = br_scr[...] * _silu(ug[:, MEM_Q:])
    y = jnp.dot(branch.astype(BF16), wout_ref[:, :D_MODEL], preferred_element_type=F32)
    out_ref[0] = _layer_norm(DEEPNORM_ALPHA * x + y, lnw_ref[...], lnb_ref[...])


def _hgrn_prompt_layer(h, w_in, w_out, mk_bf, mv_bf, layer, lb_logits, norm_w, ln_w, ln_b):
    bsz, t, d = h.shape
    rows = HG_CHUNK
    kernel = functools.partial(_hgrn_prompt_kernel, layer=layer)
    mem_spec = pl.BlockSpec((None, N_MEM, MEM_Q), lambda b, i: (layer, b, 0))
    return pl.pallas_call(
        kernel,
        grid=(bsz, t // rows),
        in_specs=[pl.BlockSpec((1, rows, d), lambda b, i: (b, i, 0)),
                  _resident(w_in.shape),
                  _resident(w_out.shape),
                  _resident((DEPTH, HG_K)),
                  _resident((1, HG_K)),
                  mem_spec, mem_spec,
                  _resident((1, d)), _resident((1, d))],
        out_specs=[pl.BlockSpec((1, rows, d), lambda b, i: (b, i, 0)),
                   pl.BlockSpec((1, HG_HEADS, HG_D, HG_D), lambda b, i: (b, 0, 0, 0))],
        out_shape=[jax.ShapeDtypeStruct((bsz, t, d), F32),
                   jax.ShapeDtypeStruct((bsz, HG_HEADS, HG_D, HG_D), F32)],
        scratch_shapes=[pltpu.VMEM((HG_HEADS, HG_D, HG_D), F32),
                        pltpu.VMEM((rows, d), F32)],
        compiler_params=_cparams(2),
        name="hgrn_prompt_layer",
    )(h, w_in, w_out, lb_logits, norm_w.reshape(1, HG_K), mk_bf, mv_bf, ln_w.reshape(1, d), ln_b.reshape(1, d))


REQS_PER_STEP = 4


def _heads_to_sublanes(row, col0, n, width):
    return jnp.concatenate([row[:, col0 + i * width:col0 + (i + 1) * width] for i in range(n)], axis=0)


def _sample_mem_attend(urow, col0, cmk_ref, cmv_ref, r):
    pieces = []
    for hm in range(MEM_HEADS):
        lo = col0 + hm * MEM_HD
        rows = pl.ds(hm, N_MEM, stride=MEM_HEADS)
        s = jnp.sum(cmk_ref[r, rows, :] * urow[:, lo:lo + MEM_HD], axis=-1, keepdims=True) * (MEM_HD ** -0.5)
        s = jnp.broadcast_to(s, (N_MEM, MEM_HD))
        p = jnp.exp(s - jnp.max(s, axis=0, keepdims=True))
        pieces.append(jnp.sum(p * cmv_ref[r, rows, :], axis=0, keepdims=True) / jnp.sum(p, axis=0, keepdims=True))
    return jnp.concatenate(pieces, axis=1)


def _sample_swa_kernel(relb_ref, sink_ref, u_ref, ck_ref, cv_ref, cmk_ref, cmv_ref,
                       br_ref, ko_ref, vo_ref, bias_scr, bias0_scr, sink_scr):
    nbuf = WINDOW

    @pl.when(pl.program_id(0) == 0)
    def _build_tables():
        dist = nbuf - lax.broadcasted_iota(jnp.int32, (V7X_SUBLANES, nbuf), 1)
        rid = lax.broadcasted_iota(jnp.int32, (V7X_SUBLANES, nbuf), 0)
        for g in range(SWA_KVH):
            acc = jnp.where(rid < SWA_GROUP, MASKED, 0.0).astype(F32)
            acc0 = jnp.zeros((V7X_SUBLANES, nbuf), F32)
            sk = jnp.zeros((V7X_SUBLANES, nbuf), F32)
            for j in range(SWA_GROUP):
                hq = g * SWA_GROUP + j
                sk = jnp.where(rid == j, sink_ref[hq], sk)
                acc0 = jnp.where(rid == j, relb_ref[0, hq], acc0)
                for bk, rng in enumerate(_BUCKET_RANGES):
                    if rng is not None:
                        hit = jnp.logical_and(rid == j, jnp.logical_and(dist >= rng[0], dist <= rng[1]))
                        acc = jnp.where(hit, relb_ref[bk, hq], acc)
            bias_scr[g] = acc
            bias0_scr[g] = acc0
            sink_scr[g] = sk

    g0 = SWA_Q + 2 * SWA_KV + MEM_Q
    pad = jnp.zeros((V7X_SUBLANES - SWA_GROUP, SWA_HD), F32)
    pairs = [(r, g) for r in range(REQS_PER_STEP) for g in range(SWA_KVH)]
    urows = [u_ref[r] for r in range(REQS_PER_STEP)]

    diag = (lax.broadcasted_iota(jnp.int32, (SWA_HD, SWA_HD), 0)
            == lax.broadcasted_iota(jnp.int32, (SWA_HD, SWA_HD), 1))
    newest = lax.broadcasted_iota(jnp.int32, (SWA_HD, nbuf), 1) == nbuf - 1

    def rolled(cache_t, new_row):
        new_col = jnp.sum(jnp.where(diag, new_row, 0.0), axis=-1, keepdims=True)
        return jnp.where(newest, new_col, pltpu.roll(cache_t, nbuf - 1, 1))

    scores = []
    for r, g in pairs:
        k_t = ck_ref[r, g]
        k_new = urows[r][:, SWA_Q + g * SWA_HD:SWA_Q + (g + 1) * SWA_HD]
        v_new = urows[r][:, SWA_Q + SWA_KV + g * SWA_HD:SWA_Q + SWA_KV + (g + 1) * SWA_HD]
        qg = jnp.concatenate(
            [urows[r][:, (g * SWA_GROUP + j) * SWA_HD:(g * SWA_GROUP + j + 1) * SWA_HD] for j in range(SWA_GROUP)]
            + [pad], axis=0) * (SWA_HD ** -0.5)
        s = _dot(qg, k_t) + bias_scr[g]
        s_new = jnp.sum(qg * k_new, axis=-1, keepdims=True) + bias0_scr[g][:, 0:1]
        scores.append((s, s_new, k_new, v_new))

    weights = []
    for (r, g), (s, s_new, _, v_new) in zip(pairs, scores):
        sink = sink_scr[g][:, 0:1]
        m = jnp.maximum(jnp.max(s, axis=-1, keepdims=True), jnp.maximum(s_new, sink))
        p = jnp.exp(s - m)
        p_new = jnp.exp(s_new - m)
        l = jnp.sum(p, axis=-1, keepdims=True) + p_new + jnp.exp(sink - m)
        weights.append((p, p_new * v_new, l))

    pieces = [[] for _ in range(REQS_PER_STEP)]
    for (r, g), (p, o_new, l) in zip(pairs, weights):
        o = (_dot_nt(p, cv_ref[r, g]) + o_new) / l
        pieces[r].extend(o[j:j + 1, :] for j in range(SWA_GROUP))

    for (r, g), (_, _, k_new, v_new) in zip(pairs, scores):
        ko_ref[r, g] = rolled(ck_ref[r, g], k_new)
        vo_ref[r, g] = rolled(cv_ref[r, g], v_new)

    for r in range(REQS_PER_STEP):
        mem_o = _sample_mem_attend(urows[r], SWA_Q + 2 * SWA_KV, cmk_ref, cmv_ref, r)
        br_ref[r] = jnp.concatenate(pieces[r] + [mem_o], axis=1) * _silu(urows[r][:, g0:g0 + D_MODEL])


def _cache_spec(layer, shape):
    return pl.BlockSpec((None, REQS_PER_STEP) + shape, lambda i: (layer, i) + (0,) * len(shape))


def _row_spec(width):
    return pl.BlockSpec((REQS_PER_STEP, 1, width), lambda i: (i, 0, 0))


def _head_minor_rows(cache):
    nl, ns, rows, heads, hd = cache.shape
    return cache.reshape(nl, ns, rows * heads, hd)


def _rows_minor(cache):
    return jnp.moveaxis(cache, -3, -1)


def _sample_swa_attend(u, cache_k, cache_v, j, cache_mk, cache_mv, layer, rel_bias, sinks):
    ns = u.shape[0]
    nbuf = cache_k.shape[-1]
    swa_shape = (SWA_KVH, SWA_HD, nbuf)
    mem_shape = (N_MEM * MEM_HEADS, MEM_HD)
    new_cache = pl.BlockSpec((REQS_PER_STEP,) + swa_shape, lambda i: (i, 0, 0, 0))
    br, ko, vo = pl.pallas_call(
        _sample_swa_kernel,
        grid=(ns // REQS_PER_STEP,),
        in_specs=[_SMEM, _SMEM, _row_spec(SWA_WIDTH), _cache_spec(j, swa_shape), _cache_spec(j, swa_shape),
                  _cache_spec(layer, mem_shape), _cache_spec(layer, mem_shape)],
        out_specs=[_row_spec(D_MODEL), new_cache, new_cache],
        out_shape=[jax.ShapeDtypeStruct((ns, 1, D_MODEL), F32),
                   jax.ShapeDtypeStruct((ns,) + swa_shape, F32),
                   jax.ShapeDtypeStruct((ns,) + swa_shape, F32)],
        scratch_shapes=[pltpu.VMEM((SWA_KVH, V7X_SUBLANES, nbuf), F32)] * 3,
        compiler_params=_cparams(1),
        name="sample_swa_attend",
    )(rel_bias, sinks, u.reshape(ns, 1, SWA_WIDTH), cache_k, cache_v, cache_mk, cache_mv)
    return br.reshape(ns, D_MODEL), ko, vo


def _sample_hgrn_gate_kernel(u_ref, mix_ref, cmk_ref, cmv_ref, br_ref):
    g0 = 3 * HG_K + MEM_Q
    for r in range(REQS_PER_STEP):
        urow = u_ref[r]
        mem_o = _sample_mem_attend(urow, 3 * HG_K, cmk_ref, cmv_ref, r)
        br_ref[r] = jnp.concatenate([mix_ref[r], mem_o], axis=1) * _silu(urow[:, g0:g0 + D_MODEL])


def _sample_hgrn_gate(u, mix, cache_mk, cache_mv, layer):
    ns = u.shape[0]
    mem_shape = (N_MEM * MEM_HEADS, MEM_HD)
    br = pl.pallas_call(
        _sample_hgrn_gate_kernel,
        grid=(ns // REQS_PER_STEP,),
        in_specs=[_row_spec(HG_WIDTH), _row_spec(HG_K), _cache_spec(layer, mem_shape), _cache_spec(layer, mem_shape)],
        out_specs=_row_spec(D_MODEL),
        out_shape=jax.ShapeDtypeStruct((ns, 1, D_MODEL), F32),
        compiler_params=_cparams(1),
        name="sample_hgrn_gate",
    )(u.reshape(ns, 1, HG_WIDTH), mix.reshape(ns, 1, HG_K), cache_mk, cache_mv)
    return br.reshape(ns, D_MODEL)


def _hgrn_sample_kernel(q_ref, f_ref, iv_ref, lbl_ref, nw_ref, st_ref, mix_ref, so_ref, o_scr, *, layer, ns):
    lb = _lower_bound(lbl_ref[...], layer)
    fg = lb + (1.0 - lb) * jax.nn.sigmoid(f_ref[...])
    kk_t = (1.0 - fg).T
    fg_t = fg.T
    qs_t = _silu(q_ref[...]).T
    for r in range(ns):
        state = fg_t[:, r:r + 1] * st_ref[r] + kk_t[:, r:r + 1] * iv_ref[r:r + 1, :]
        so_ref[r] = state
        o_scr[r:r + 1, :] = jnp.sum(qs_t[:, r:r + 1] * state, axis=0, keepdims=True)
    o = o_scr[...]
    ms = jnp.mean(o * o, axis=-1, keepdims=True)
    mix_ref[...] = o * lax.rsqrt(ms + RMS_EPS) * nw_ref[...]


def _hgrn_sample_step(u, state, j, lb_logits, norm_w, layer):
    ns = u.shape[0]
    col = lambda part: pl.BlockSpec((ns, HG_D), lambda h: (0, part * HG_HEADS + h))
    kernel = functools.partial(_hgrn_sample_kernel, layer=layer, ns=ns)
    return pl.pallas_call(
        kernel,
        grid=(HG_HEADS,),
        in_specs=[col(0), col(1), col(2),
                  pl.BlockSpec((DEPTH, HG_D), lambda h: (0, h)),
                  pl.BlockSpec((1, HG_D), lambda h: (0, h)),
                  pl.BlockSpec((None, ns, None, HG_D, HG_D), lambda h: (j, 0, h, 0, 0))],
        out_specs=[pl.BlockSpec((ns, HG_D), lambda h: (0, h)),
                   pl.BlockSpec((ns, None, HG_D, HG_D), lambda h: (0, h, 0, 0))],
        out_shape=[jax.ShapeDtypeStruct((ns, HG_K), F32),
                   jax.ShapeDtypeStruct(state.shape[1:], F32)],
        scratch_shapes=[pltpu.VMEM((ns, HG_D), F32)],
        compiler_params=_cparams(1),
        name="hgrn_sample_step",
    )(u, u, u, lb_logits, norm_w.reshape(1, HG_K), state)


def _outproj_ln_kernel(br_ref, h_ref, wout_ref, lnw_ref, lnb_ref, out_ref):
    y = jnp.dot(br_ref[...].astype(BF16), wout_ref[:, :D_MODEL], preferred_element_type=F32)
    out_ref[...] = _layer_norm(DEEPNORM_ALPHA * h_ref[...] + y, lnw_ref[...], lnb_ref[...])


def _outproj_ln(branch, h, w_out, ln_w, ln_b):
    ns, d = h.shape
    full = lambda shape: pl.BlockSpec(shape, lambda i: (0, 0))
    return pl.pallas_call(
        _outproj_ln_kernel,
        grid=(1,),
        in_specs=[full((ns, d)), full((ns, d)), full(w_out.shape), full((1, d)), full((1, d))],
        out_specs=full((ns, d)),
        out_shape=jax.ShapeDtypeStruct((ns, d), F32),
        compiler_params=_cparams(1),
        name="outproj_ln",
    )(branch, h, w_out, ln_w.reshape(1, d), ln_b.reshape(1, d))


def kernel(x_prompt, x_sample, cache_mem_k, cache_mem_v, cache_swa_k, cache_swa_v, state_hgrn, mem_prompt, rel_bias, swa_w_in, swa_sinks, hg_w_in, hg_lb_logits, hg_norm_w, w_mem_k, w_mem_v, w_out, ln_w, ln_b):
    bp, t, d = x_prompt.shape
    ns = x_sample.shape[0]
    assert d == D_MODEL and x_sample.shape[1] == 1 and t % PROMPT_ROWS == 0
    assert cache_swa_k.shape[2] == WINDOW <= PAST_LEN
    assert w_mem_k.shape[0] == DEPTH

    wo = [(_pad_cols(w_out[i]) if i % 2 else w_out[i]).astype(BF16) for i in range(DEPTH)]
    mem_k_prompt, mem_v_prompt, mk_bf, mv_bf = _mem_kv_proj(mem_prompt, w_mem_k, w_mem_v)
    cmk, cmv = _head_minor_rows(cache_mem_k), _head_minor_rows(cache_mem_v)
    csk, csv = _rows_minor(cache_swa_k), _rows_minor(cache_swa_v)

    hp = x_prompt
    hs = x_sample.reshape(ns, d)
    swa_kp, swa_vp, swa_ks, swa_vs, hg_sp, hg_ss = [], [], [], [], [], []
    for i in range(DEPTH):
        j = i // 2
        if i % 2 == 0:
            w_in = _pad_cols(swa_w_in[j]).astype(BF16)
            hp, kw, vw = _swa_prompt_layer(hp, w_in, wo[i], mk_bf, mv_bf, i, rel_bias, swa_sinks[j],
                                           ln_w[i], ln_b[i], rows=PROMPT_ROWS)
            swa_kp.append(kw.reshape(bp, WINDOW, SWA_KVH, SWA_HD))
            swa_vp.append(vw.reshape(bp, WINDOW, SWA_KVH, SWA_HD))
            u = _proj(hs, w_in, SWA_WIDTH, "sample_swa_proj")
            br, ko, vo = _sample_swa_attend(u, csk, csv, j, cmk, cmv, i, rel_bias, swa_sinks[j])
            swa_ks.append(jnp.moveaxis(ko, -1, -3))
            swa_vs.append(jnp.moveaxis(vo, -1, -3))
        else:
            w_in = _pad_cols(hg_w_in[j]).astype(BF16)
            hp, st = _hgrn_prompt_layer(hp, w_in, wo[i], mk_bf, mv_bf, i, hg_lb_logits, hg_norm_w[j],
                                        ln_w[i], ln_b[i])
            hg_sp.append(st)
            u = _proj(hs, w_in, HG_WIDTH, "sample_hgrn_proj")
            mix, so = _hgrn_sample_step(u, state_hgrn, j, hg_lb_logits, hg_norm_w[j], i)
            hg_ss.append(so)
            br = _sample_hgrn_gate(u, mix, cmk, cmv, i)
        hs = _outproj_ln(br, hs, wo[i], ln_w[i], ln_b[i])

    return (hp, hs.reshape(ns, 1, d), mem_k_prompt, mem_v_prompt,
            jnp.stack(swa_kp), jnp.stack(swa_vp), jnp.stack(hg_sp),
            jnp.stack(swa_ks), jnp.stack(swa_vs), jnp.stack(hg_ss))
```

```python
import functools
import math

import numpy as np
import jax
import jax.numpy as jnp
from jax import lax
from jax.experimental import pallas as pl
from jax.experimental.pallas import tpu as pltpu

F32 = jnp.float32
BF16 = jnp.bfloat16

D_MODEL = 2048
DEPTH = 2
PAST_LEN = 8192
N_MEM = 256
MEM_HEADS = 4
MEM_HD = 128
MEM_Q = MEM_HEADS * MEM_HD
WINDOW = 128
SWA_HD = 64
SWA_KVH = 4
SWA_GROUP = 6
SWA_QH = SWA_KVH * SWA_GROUP
SWA_Q = SWA_QH * SWA_HD
SWA_KV = SWA_KVH * SWA_HD
N_BUCKETS = 32
HG_HEADS = 12
HG_D = 128
HG_K = HG_HEADS * HG_D
SWA_WIDTH = SWA_Q + 2 * SWA_KV + MEM_Q + D_MODEL
HG_WIDTH = 3 * HG_K + MEM_Q + D_MODEL
DEEPNORM_ALPHA = (2.0 * DEPTH) ** 0.25
LN_EPS = 1e-5
RMS_EPS = 1e-6
MASKED = -1e30

V7X_SUBLANES = 8
V7X_LANES = 128
V7X_MXU_COLS = 256
V7X_VMEM_LIMIT = 60000 * 1024
PROJ_COLS = 512
GATE_CHUNK = 512
PROMPT_ROWS = 256


def _t5_bucket_ranges():
    n = np.arange(WINDOW)
    max_exact = N_BUCKETS // 2
    nf = np.maximum(n, 1).astype(np.float64)
    large = max_exact + (np.log(nf / max_exact) / math.log(WINDOW / max_exact)
                         * (N_BUCKETS - max_exact)).astype(np.int64)
    bucket = np.where(n < max_exact, n, np.minimum(large, N_BUCKETS - 1))
    ranges = []
    for b in range(N_BUCKETS):
        idx = np.nonzero(bucket == b)[0]
        ranges.append((int(idx[0]), int(idx[-1])) if idx.size else None)
    return ranges


_BUCKET_RANGES = _t5_bucket_ranges()


def _dot(a, b):
    return jnp.dot(a.astype(BF16), b.astype(BF16), preferred_element_type=F32)


def _dot_nt(a, b):
    return lax.dot_general(a.astype(BF16), b.astype(BF16), (((1,), (1,)), ((), ())),
                           preferred_element_type=F32)


def _layer_norm(z, w, b):
    mu = jnp.mean(z, axis=-1, keepdims=True)
    zc = z - mu
    var = jnp.mean(zc * zc, axis=-1, keepdims=True)
    return zc * lax.rsqrt(var + LN_EPS) * w + b


def _silu(x):
    return x * jax.nn.sigmoid(x)


def _cparams(n_axes):
    return pltpu.CompilerParams(dimension_semantics=("arbitrary",) * n_axes,
                                vmem_limit_bytes=V7X_VMEM_LIMIT)


def _resident(shape):
    nd = len(shape)
    return pl.BlockSpec(shape, lambda *_: (0,) * nd, pipeline_mode=pl.Buffered(1))


_SMEM = pl.BlockSpec(memory_space=pltpu.SMEM)


def _mem_kv_kernel(x_ref, wk_ref, wv_ref, ko_ref, vo_ref, kb_ref, vb_ref):
    xb = x_ref[...].astype(BF16)
    for w_ref, o_ref, b_ref in ((wk_ref, ko_ref, kb_ref), (wv_ref, vo_ref, vb_ref)):
        y = jnp.dot(xb, w_ref[...].astype(BF16), preferred_element_type=F32)
        b_ref[...] = y.astype(BF16)
        for hm in range(MEM_HEADS):
            o_ref[:, hm, :] = y[:, hm * MEM_HD:(hm + 1) * MEM_HD]


def _mem_kv_proj(mem, w_k, w_v):
    bsz, n, d = mem.shape
    nl = w_k.shape[0]
    w_spec = pl.BlockSpec((None, d, MEM_Q), lambda l, b: (l, 0, 0))
    o_spec = pl.BlockSpec((None, None, n, MEM_HEADS, MEM_HD), lambda l, b: (l, b, 0, 0, 0))
    b_spec = pl.BlockSpec((None, n, MEM_Q), lambda l, b: (l, b, 0))
    o_shape = jax.ShapeDtypeStruct((nl, bsz, n, MEM_HEADS, MEM_HD), F32)
    b_shape = jax.ShapeDtypeStruct((nl, bsz * n, MEM_Q), BF16)
    return pl.pallas_call(
        _mem_kv_kernel,
        grid=(nl, bsz),
        in_specs=[pl.BlockSpec((None, n, d), lambda l, b: (b, 0, 0)), w_spec, w_spec],
        out_specs=[o_spec, o_spec, b_spec, b_spec],
        out_shape=[o_shape, o_shape, b_shape, b_shape],
        compiler_params=_cparams(2),
        name="mem_kv_proj",
    )(mem, w_k, w_v)


def _proj_kernel(x_ref, w_ref, o_ref):
    o_ref[...] = jnp.dot(x_ref[...].astype(BF16), w_ref[...], preferred_element_type=F32)


def _pad_cols(w):
    if w.shape[-1] % (4 * V7X_MXU_COLS) != 0:
        return w
    return jnp.concatenate([w, jnp.zeros(w.shape[:-1] + (V7X_MXU_COLS,), w.dtype)], axis=-1)


def _proj(x, w, n, name):
    m, k = x.shape
    return pl.pallas_call(
        _proj_kernel,
        grid=(n // PROJ_COLS,),
        in_specs=[pl.BlockSpec((m, k), lambda j: (0, 0)),
                  pl.BlockSpec((k, PROJ_COLS), lambda j: (0, j))],
        out_specs=pl.BlockSpec((m, PROJ_COLS), lambda j: (0, j)),
        out_shape=jax.ShapeDtypeStruct((m, n), F32),
        compiler_params=_cparams(1),
        name=name,
    )(x, w)


def _mem_scores(mq_all, mk_ref):
    return [_dot_nt(mq_all[:, hm * MEM_HD:(hm + 1) * MEM_HD], mk_ref[:, hm * MEM_HD:(hm + 1) * MEM_HD])
            * (MEM_HD ** -0.5) for hm in range(MEM_HEADS)]


def _put_gated(bb_ref, gs_ref, col0, o):
    cols = slice(col0, col0 + o.shape[-1])
    bb_ref[:, cols] = (o * gs_ref[:, cols]).astype(BF16)


def _mem_values(weights, mv_ref, bb_ref, gs_ref, out_col0):
    for hm, (p, l) in enumerate(weights):
        lo = hm * MEM_HD
        _put_gated(bb_ref, gs_ref, out_col0 + lo, _dot(p, mv_ref[:, lo:lo + MEM_HD]) / l)


def _swa_prompt_kernel(relb_ref, sink_ref, h_ref, win_ref, wout_ref, mk_ref, mv_ref, lnw_ref, lnb_ref,
                       out_ref, kout_ref, vout_ref,
                       bias_scr, k_scr, v_scr, gs_scr, bb_scr, *, rows):
    b = pl.program_id(0)
    t = pl.program_id(1)
    w2 = 2 * WINDOW

    @pl.when(jnp.logical_and(b == 0, t == 0))
    def _build_bias():
        qi = lax.broadcasted_iota(jnp.int32, (WINDOW, w2), 0)
        kj = lax.broadcasted_iota(jnp.int32, (WINDOW, w2), 1)
        dist = qi + WINDOW - kj

        def per_head(hq, carry):
            acc = jnp.full((WINDOW, w2), MASKED, F32)
            for bk, rng in enumerate(_BUCKET_RANGES):
                if rng is not None:
                    hit = jnp.logical_and(dist >= rng[0], dist <= rng[1])
                    acc = jnp.where(hit, relb_ref[bk, hq], acc)
            bias_scr[0, hq] = acc
            bias_scr[1, hq] = jnp.where(kj < WINDOW, MASKED, acc)
            return carry

        lax.fori_loop(0, SWA_QH, per_head, 0)

    @pl.when(t == 0)
    def _reset_window():
        k_scr[0:WINDOW, :] = jnp.zeros((WINDOW, SWA_KV), BF16)
        v_scr[0:WINDOW, :] = jnp.zeros((WINDOW, SWA_KV), BF16)

    qkv_w = SWA_Q + 2 * SWA_KV
    n_blocks = rows // WINDOW
    n_gate = D_MODEL // GATE_CHUNK
    heads = [(g, j) for g in range(SWA_KVH) for j in range(SWA_GROUP)]
    xb, qkv, mq, scores, mem_s = {}, {}, {}, {}, {}

    def project(qb):
        r0 = qb * WINDOW
        xb[qb] = h_ref[0, r0:r0 + WINDOW, :].astype(BF16)
        qkv[qb] = jnp.dot(xb[qb], win_ref[:, :qkv_w], preferred_element_type=F32)
        mq[qb] = jnp.dot(xb[qb], win_ref[:, qkv_w:qkv_w + MEM_Q], preferred_element_type=F32)
        k = qkv[qb][:, SWA_Q:SWA_Q + SWA_KV]
        v = qkv[qb][:, SWA_Q + SWA_KV:]
        if qb == n_blocks - 1:
            kout_ref[0] = k
            vout_ref[0] = v
        k_scr[WINDOW + r0:w2 + r0, :] = k.astype(BF16)
        v_scr[WINDOW + r0:w2 + r0, :] = v.astype(BF16)

    def score(qb):
        r0 = qb * WINDOW
        table = jnp.where(t == 0, 1, 0) if qb == 0 else 0
        scores[qb] = []
        for idx, (g, j) in enumerate(heads):
            hq = g * SWA_GROUP + j
            q = qkv[qb][:, hq * SWA_HD:(hq + 1) * SWA_HD] * (SWA_HD ** -0.5)
            scores[qb].append(_dot_nt(q, k_scr[r0:r0 + w2, g * SWA_HD:(g + 1) * SWA_HD]) + bias_scr[table, hq])
            for chunk in range(idx * n_gate // len(heads), (idx + 1) * n_gate // len(heads)):
                lo = chunk * GATE_CHUNK
                gate = jnp.dot(xb[qb], win_ref[:, qkv_w + MEM_Q + lo:qkv_w + MEM_Q + lo + GATE_CHUNK],
                               preferred_element_type=F32)
                gs_scr[r0:r0 + WINDOW, lo:lo + GATE_CHUNK] = _silu(gate)
        mem_s[qb] = _mem_scores(mq[qb], mk_ref)

    def attend(qb):
        r0 = qb * WINDOW
        bb = bb_scr.at[r0:r0 + WINDOW, :]
        gs = gs_scr.at[r0:r0 + WINDOW, :]
        sinks = [sink_ref[g * SWA_GROUP + j] for g, j in heads]
        maxes = [jnp.maximum(jnp.max(s, axis=-1, keepdims=True), sk) for s, sk in zip(scores[qb], sinks)]
        mem_max = [jnp.max(s, axis=-1, keepdims=True) for s in mem_s[qb]]
        probs = [jnp.exp(s - m) for s, m in zip(scores[qb], maxes)]
        mem_p = [jnp.exp(s - m) for s, m in zip(mem_s[qb], mem_max)]
        weights = [(p, jnp.sum(p, axis=-1, keepdims=True) + jnp.exp(sk - m))
                   for p, m, sk in zip(probs, maxes, sinks)]
        mem_w = [(p, jnp.sum(p, axis=-1, keepdims=True)) for p in mem_p]
        for (g, j), (p, l) in zip(heads, weights):
            o = _dot(p, v_scr[r0:r0 + w2, g * SWA_HD:(g + 1) * SWA_HD]) / l
            _put_gated(bb, gs, (g * SWA_GROUP + j) * SWA_HD, o)
        _mem_values(mem_w, mv_ref, bb, gs, SWA_Q)

    def finish(qb):
        r0 = qb * WINDOW
        y = jnp.dot(bb_scr[r0:r0 + WINDOW, :], wout_ref[:, :D_MODEL], preferred_element_type=F32)
        out_ref[0, r0:r0 + WINDOW, :] = _layer_norm(DEEPNORM_ALPHA * h_ref[0, r0:r0 + WINDOW, :] + y,
                                                    lnw_ref[...], lnb_ref[...])

    project(0)
    score(0)
    for qb in range(n_blocks):
        if qb + 1 < n_blocks:
            project(qb + 1)
            score(qb + 1)
        attend(qb)
        finish(qb)

    k_scr[0:WINDOW, :] = k_scr[rows:rows + WINDOW, :]
    v_scr[0:WINDOW, :] = v_scr[rows:rows + WINDOW, :]


def _swa_prompt_layer(h, w_in, w_out, mk_bf, mv_bf, layer, rel_bias, sinks, ln_w, ln_b, rows):
    bsz, t, d = h.shape
    kernel = functools.partial(_swa_prompt_kernel, rows=rows)
    mem_spec = pl.BlockSpec((None, N_MEM, MEM_Q), lambda b, i: (layer, b, 0))
    return pl.pallas_call(
        kernel,
        grid=(bsz, t // rows),
        in_specs=[_SMEM, _SMEM,
                  pl.BlockSpec((1, rows, d), lambda b, i: (b, i, 0)),
                  _resident(w_in.shape),
                  _resident(w_out.shape),
                  mem_spec, mem_spec,
                  _resident((1, d)), _resident((1, d))],
        out_specs=[pl.BlockSpec((1, rows, d), lambda b, i: (b, i, 0)),
                   pl.BlockSpec((1, WINDOW, SWA_KV), lambda b, i: (b, 0, 0)),
                   pl.BlockSpec((1, WINDOW, SWA_KV), lambda b, i: (b, 0, 0))],
        out_shape=[jax.ShapeDtypeStruct((bsz, t, d), F32),
                   jax.ShapeDtypeStruct((bsz, WINDOW, SWA_KV), F32),
                   jax.ShapeDtypeStruct((bsz, WINDOW, SWA_KV), F32)],
        scratch_shapes=[pltpu.VMEM((2, SWA_QH, WINDOW, 2 * WINDOW), F32),
                        pltpu.VMEM((WINDOW + rows, SWA_KV), BF16),
                        pltpu.VMEM((WINDOW + rows, SWA_KV), BF16),
                        pltpu.VMEM((rows, d), F32),
                        pltpu.VMEM((rows, d), BF16)],
        compiler_params=_cparams(2),
        name="swa_prompt_layer",
    )(rel_bias, sinks, h, w_in, w_out, mk_bf, mv_bf, ln_w.reshape(1, d), ln_b.reshape(1, d))


HG_CHUNK = 128
HG_DIRECT = 8


def _lower_bound(logits, layer):
    e = jnp.exp(logits - jnp.max(logits, axis=0, keepdims=True))
    sm = e / jnp.sum(e, axis=0, keepdims=True)
    return jnp.sum(sm[1:layer + 1, :], axis=0, keepdims=True)


def _cumsum_rows(tril_bf, g):
    w = g.shape[1]
    g1 = g.astype(BF16)
    r1 = g - g1.astype(F32)
    g2 = r1.astype(BF16)
    g3 = (r1 - g2.astype(F32)).astype(BF16)
    d = jnp.dot(tril_bf, jnp.concatenate([g1, g2, g3], axis=1), preferred_element_type=F32)
    return (d[:, :w] + d[:, w:2 * w]) + d[:, 2 * w:]


def _hgrn_chunk_scores(qs, kk, gc):
    c = HG_CHUNK
    nb = c // HG_DIRECT
    q3 = qs.reshape(nb, HG_DIRECT, HG_D)
    g3 = gc.reshape(nb, HG_DIRECT, HG_D)
    k3 = kk.reshape(nb, HG_DIRECT, HG_D)
    lane = lax.broadcasted_iota(jnp.int32, (nb, HG_DIRECT, c), 2)
    blk = lax.broadcasted_iota(jnp.int32, (nb, HG_DIRECT, c), 0)
    trow = lax.broadcasted_iota(jnp.int32, (nb, HG_DIRECT, c), 1)
    a3 = jnp.zeros((nb, HG_DIRECT, c), F32)
    for sl in range(HG_DIRECT):
        e = jnp.exp2(g3 - g3[:, sl:sl + 1, :])
        col = jnp.sum(e * q3 * k3[:, sl:sl + 1, :], axis=-1, keepdims=True)
        hit = jnp.logical_and(lane == blk * HG_DIRECT + sl, trow >= sl)
        a3 = jnp.where(hit, col, a3)
    a = a3.reshape(c, c)

    ti = lax.broadcasted_iota(jnp.int32, (c, c), 0)
    si = lax.broadcasted_iota(jnp.int32, (c, c), 1)
    ri = lax.broadcasted_iota(jnp.int32, (c, 1), 0)
    half = HG_DIRECT
    while half < c:
        span = 2 * half
        gsp = gc.reshape(c // span, span, HG_D)
        gref = jnp.broadcast_to(gsp[:, half - 1:half, :], gsp.shape).reshape(c, HG_D)
        upper = (ri % span) >= half
        eq = jnp.where(upper, jnp.exp2(gc - gref), 0.0)
        ek = jnp.where(upper, 0.0, jnp.exp2(gref - gc))
        a = a + jnp.where((ti // span) == (si // span), _dot_nt(qs * eq, kk * ek), 0.0)
        half = span
    return a


def _hgrn_prompt_kernel(h_ref, win_ref, wout_ref, lbl_ref, nw_ref, mk_ref, mv_ref, lnw_ref, lnb_ref,
                        out_ref, st_ref,
                        st_scr, br_scr, *, layer):
    t = pl.program_id(1)
    c = HG_CHUNK
    heads = range(HG_HEADS)

    @pl.when(t == 0)
    def _reset_state():
        st_scr[...] = jnp.zeros(st_scr.shape, F32)

    lb_all = _lower_bound(lbl_ref[...], layer)
    ri = lax.broadcasted_iota(jnp.int32, (c, c), 0)
    ci = lax.broadcasted_iota(jnp.int32, (c, c), 1)
    tril_bf = jnp.where(ri >= ci, 1.0, 0.0).astype(BF16)

    x = h_ref[0]
    xb = x.astype(BF16)
    pair_w = 2 * HG_D
    n_pairs = HG_HEADS // 2
    qs, kk, iv, gc = [], [], [], []

    def project(pair):
        return [jnp.dot(xb, win_ref[:, part * HG_K + pair * pair_w:part * HG_K + (pair + 1) * pair_w],
                        preferred_element_type=F32) for part in range(3)]

    def gates(pair, uq, uf, ui):
        lb = lb_all[:, pair * pair_w:(pair + 1) * pair_w]
        fg = lb + (1.0 - lb) * jax.nn.sigmoid(uf)
        gc2 = _cumsum_rows(tril_bf, jnp.log2(fg))
        for cols in (slice(0, HG_D), slice(HG_D, pair_w)):
            qs.append(_silu(uq[:, cols]))
            iv.append(ui[:, cols])
            kk.append(1.0 - fg[:, cols])
            gc.append(gc2[:, cols])

    parts = [project(0)]
    for pair in range(1, n_pairs):
        parts.append(project(pair))
        gates(pair - 1, *parts[pair - 1])
    ug = jnp.dot(xb, win_ref[:, 3 * HG_K:HG_WIDTH], preferred_element_type=F32)
    gates(n_pairs - 1, *parts[n_pairs - 1])
    mem_s = _mem_scores(ug[:, :MEM_Q], mk_ref)
    mem_p = [jnp.exp(s - jnp.max(s, axis=-1, keepdims=True)) for s in mem_s]
    for hm, p in enumerate(mem_p):
        lo = hm * MEM_HD
        br_scr[:, HG_K + lo:HG_K + lo + MEM_HD] = (_dot(p, mv_ref[:, lo:lo + MEM_HD])
                                                   / jnp.sum(p, axis=-1, keepdims=True))

    a = [_hgrn_chunk_scores(qs[hh], kk[hh], gc[hh]) for hh in heads]

    iv_t = [iv[hh].T for hh in heads]
    for hh in heads:
        lhs = jnp.concatenate([a[hh], qs[hh] * jnp.exp2(gc[hh])], axis=1)
        rhs = jnp.concatenate([iv_t[hh], st_scr[hh]], axis=1)
        o = _dot_nt(lhs, rhs)
        ms = jnp.mean(o * o, axis=-1, keepdims=True)
        br_scr[:, hh * HG_D:(hh + 1) * HG_D] = o * lax.rsqrt(ms + RMS_EPS) * nw_ref[:, hh * HG_D:(hh + 1) * HG_D]

    for hh in heads:
        g_last = gc[hh][c - 1:c, :]
        kdec = kk[hh] * jnp.exp2(g_last - gc[hh])
        st_scr[hh] = st_scr[hh] * jnp.exp2(g_last) + _dot(iv_t[hh], kdec)

    @pl.when(t == pl.num_programs(1) - 1)
    def _emit_state():
        for hh in heads:
            st_ref[0, hh] = st_scr[hh].T

    branch = br_scr[...] * _silu(ug[:, MEM_Q:])
    y = jnp.dot(branch.astype(BF16), wout_ref[:, :D_MODEL], preferred_element_type=F32)
    out_ref[0] = _layer_norm(DEEPNORM_ALPHA * x + y, lnw_ref[...], lnb_ref[...])


def _hgrn_prompt_layer(h, w_in, w_out, mk_bf, mv_bf, layer, lb_logits, norm_w, ln_w, ln_b):
    bsz, t, d = h.shape
    rows = HG_CHUNK
    kernel = functools.partial(_hgrn_prompt_kernel, layer=layer)
    mem_spec = pl.BlockSpec((None, N_MEM, MEM_Q), lambda b, i: (layer, b, 0))
    return pl.pallas_call(
        kernel,
        grid=(bsz, t // rows),
        in_specs=[pl.BlockSpec((1, rows, d), lambda b, i: (b, i, 0)),
                  _resident(w_in.shape),
                  _resident(w_out.shape),
                  _resident((DEPTH, HG_K)),
                  _resident((1, HG_K)),
                  mem_spec, mem_spec,
                  _resident((1, d)), _resident((1, d))],
        out_specs=[pl.BlockSpec((1, rows, d), lambda b, i: (b, i, 0)),
                   pl.BlockSpec((1, HG_HEADS, HG_D, HG_D), lambda b, i: (b, 0, 0, 0))],
        out_shape=[jax.ShapeDtypeStruct((bsz, t, d), F32),
                   jax.ShapeDtypeStruct((bsz, HG_HEADS, HG_D, HG_D), F32)],
        scratch_shapes=[pltpu.VMEM((HG_HEADS, HG_D, HG_D), F32),
                        pltpu.VMEM((rows, d), F32)],
        compiler_params=_cparams(2),
        name="hgrn_prompt_layer",
    )(h, w_in, w_out, lb_logits, norm_w.reshape(1, HG_K), mk_bf, mv_bf, ln_w.reshape(1, d), ln_b.reshape(1, d))


REQS_PER_STEP = 4


def _heads_to_sublanes(row, col0, n, width):
    return jnp.concatenate([row[:, col0 + i * width:col0 + (i + 1) * width] for i in range(n)], axis=0)


def _sample_mem_attend(urow, col0, cmk_ref, cmv_ref, r):
    pieces = []
    for hm in range(MEM_HEADS):
        lo = col0 + hm * MEM_HD
        rows = pl.ds(hm, N_MEM, stride=MEM_HEADS)
        s = jnp.sum(cmk_ref[r, rows, :] * urow[:, lo:lo + MEM_HD], axis=-1, keepdims=True) * (MEM_HD ** -0.5)
        s = jnp.broadcast_to(s, (N_MEM, MEM_HD))
        p = jnp.exp(s - jnp.max(s, axis=0, keepdims=True))
        pieces.append(jnp.sum(p * cmv_ref[r, rows, :], axis=0, keepdims=True) / jnp.sum(p, axis=0, keepdims=True))
    return jnp.concatenate(pieces, axis=1)


def _sample_swa_kernel(relb_ref, sink_ref, u_ref, ck_ref, cv_ref, cmk_ref, cmv_ref,
                       br_ref, ko_ref, vo_ref, bias_scr, bias0_scr, sink_scr):
    nbuf = WINDOW

    @pl.when(pl.program_id(0) == 0)
    def _build_tables():
        dist = nbuf - lax.broadcasted_iota(jnp.int32, (V7X_SUBLANES, nbuf), 1)
        rid = lax.broadcasted_iota(jnp.int32, (V7X_SUBLANES, nbuf), 0)
        for g in range(SWA_KVH):
            acc = jnp.where(rid < SWA_GROUP, MASKED, 0.0).astype(F32)
            acc0 = jnp.zeros((V7X_SUBLANES, nbuf), F32)
            sk = jnp.zeros((V7X_SUBLANES, nbuf), F32)
            for j in range(SWA_GROUP):
                hq = g * SWA_GROUP + j
                sk = jnp.where(rid == j, sink_ref[hq], sk)
                acc0 = jnp.where(rid == j, relb_ref[0, hq], acc0)
                for bk, rng in enumerate(_BUCKET_RANGES):
                    if rng is not None:
                        hit = jnp.logical_and(rid == j, jnp.logical_and(dist >= rng[0], dist <= rng[1]))
                        acc = jnp.where(hit, relb_ref[bk, hq], acc)
            bias_scr[g] = acc
            bias0_scr[g] = acc0
            sink_scr[g] = sk

    g0 = SWA_Q + 2 * SWA_KV + MEM_Q
    pad = jnp.zeros((V7X_SUBLANES - SWA_GROUP, SWA_HD), F32)
    pairs = [(r, g) for r in range(REQS_PER_STEP) for g in range(SWA_KVH)]
    urows = [u_ref[r] for r in range(REQS_PER_STEP)]

    diag = (lax.broadcasted_iota(jnp.int32, (SWA_HD, SWA_HD), 0)
            == lax.broadcasted_iota(jnp.int32, (SWA_HD, SWA_HD), 1))
    newest = lax.broadcasted_iota(jnp.int32, (SWA_HD, nbuf), 1) == nbuf - 1

    def rolled(cache_t, new_row):
        new_col = jnp.sum(jnp.where(diag, new_row, 0.0), axis=-1, keepdims=True)
        return jnp.where(newest, new_col, pltpu.roll(cache_t, nbuf - 1, 1))

    scores = []
    for r, g in pairs:
        k_t = ck_ref[r, g]
        k_new = urows[r][:, SWA_Q + g * SWA_HD:SWA_Q + (g + 1) * SWA_HD]
        v_new = urows[r][:, SWA_Q + SWA_KV + g * SWA_HD:SWA_Q + SWA_KV + (g + 1) * SWA_HD]
        qg = jnp.concatenate(
            [urows[r][:, (g * SWA_GROUP + j) * SWA_HD:(g * SWA_GROUP + j + 1) * SWA_HD] for j in range(SWA_GROUP)]
            + [pad], axis=0) * (SWA_HD ** -0.5)
        s = _dot(qg, k_t) + bias_scr[g]
        s_new = jnp.sum(qg * k_new, axis=-1, keepdims=True) + bias0_scr[g][:, 0:1]
        scores.append((s, s_new, k_new, v_new))

    weights = []
    for (r, g), (s, s_new, _, v_new) in zip(pairs, scores):
        sink = sink_scr[g][:, 0:1]
        m = jnp.maximum(jnp.max(s, axis=-1, keepdims=True), jnp.maximum(s_new, sink))
        p = jnp.exp(s - m)
        p_new = jnp.exp(s_new - m)
        l = jnp.sum(p, axis=-1, keepdims=True) + p_new + jnp.exp(sink - m)
        weights.append((p, p_new * v_new, l))

    pieces = [[] for _ in range(REQS_PER_STEP)]
    for (r, g), (p, o_new, l) in zip(pairs, weights):
        o = (_dot_nt(p, cv_ref[r, g]) + o_new) / l
        pieces[r].extend(o[j:j + 1, :] for j in range(SWA_GROUP))

    for (r, g), (_, _, k_new, v_new) in zip(pairs, scores):
        ko_ref[r, g] = rolled(ck_ref[r, g], k_new)
        vo_ref[r, g] = rolled(cv_ref[r, g], v_new)

    for r in range(REQS_PER_STEP):
        mem_o = _sample_mem_attend(urows[r], SWA_Q + 2 * SWA_KV, cmk_ref, cmv_ref, r)
        br_ref[r] = jnp.concatenate(pieces[r] + [mem_o], axis=1) * _silu(urows[r][:, g0:g0 + D_MODEL])


def _cache_spec(layer, shape):
    return pl.BlockSpec((None, REQS_PER_STEP) + shape, lambda i: (layer, i) + (0,) * len(shape))


def _row_spec(width):
    return pl.BlockSpec((REQS_PER_STEP, 1, width), lambda i: (i, 0, 0))


def _head_minor_rows(cache):
    nl, ns, rows, heads, hd = cache.shape
    return cache.reshape(nl, ns, rows * heads, hd)


def _rows_minor(cache):
    return jnp.moveaxis(cache, -3, -1)


def _sample_swa_attend(u, cache_k, cache_v, j, cache_mk, cache_mv, layer, rel_bias, sinks):
    ns = u.shape[0]
    nbuf = cache_k.shape[-1]
    swa_shape = (SWA_KVH, SWA_HD, nbuf)
    mem_shape = (N_MEM * MEM_HEADS, MEM_HD)
    new_cache = pl.BlockSpec((REQS_PER_STEP,) + swa_shape, lambda i: (i, 0, 0, 0))
    br, ko, vo = pl.pallas_call(
        _sample_swa_kernel,
        grid=(ns // REQS_PER_STEP,),
        in_specs=[_SMEM, _SMEM, _row_spec(SWA_WIDTH), _cache_spec(j, swa_shape), _cache_spec(j, swa_shape),
                  _cache_spec(layer, mem_shape), _cache_spec(layer, mem_shape)],
        out_specs=[_row_spec(D_MODEL), new_cache, new_cache],
        out_shape=[jax.ShapeDtypeStruct((ns, 1, D_MODEL), F32),
                   jax.ShapeDtypeStruct((ns,) + swa_shape, F32),
                   jax.ShapeDtypeStruct((ns,) + swa_shape, F32)],
        scratch_shapes=[pltpu.VMEM((SWA_KVH, V7X_SUBLANES, nbuf), F32)] * 3,
        compiler_params=_cparams(1),
        name="sample_swa_attend",
    )(rel_bias, sinks, u.reshape(ns, 1, SWA_WIDTH), cache_k, cache_v, cache_mk, cache_mv)
    return br.reshape(ns, D_MODEL), ko, vo


def _sample_hgrn_gate_kernel(u_ref, mix_ref, cmk_ref, cmv_ref, br_ref):
    g0 = 3 * HG_K + MEM_Q
    for r in range(REQS_PER_STEP):
        urow = u_ref[r]
        mem_o = _sample_mem_attend(urow, 3 * HG_K, cmk_ref, cmv_ref, r)
        br_ref[r] = jnp.concatenate([mix_ref[r], mem_o], axis=1) * _silu(urow[:, g0:g0 + D_MODEL])


def _sample_hgrn_gate(u, mix, cache_mk, cache_mv, layer):
    ns = u.shape[0]
    mem_shape = (N_MEM * MEM_HEADS, MEM_HD)
    br = pl.pallas_call(
        _sample_hgrn_gate_kernel,
        grid=(ns // REQS_PER_STEP,),
        in_specs=[_row_spec(HG_WIDTH), _row_spec(HG_K), _cache_spec(layer, mem_shape), _cache_spec(layer, mem_shape)],
        out_specs=_row_spec(D_MODEL),
        out_shape=jax.ShapeDtypeStruct((ns, 1, D_MODEL), F32),
        compiler_params=_cparams(1),
        name="sample_hgrn_gate",
    )(u.reshape(ns, 1, HG_WIDTH), mix.reshape(ns, 1, HG_K), cache_mk, cache_mv)
    return br.reshape(ns, D_MODEL)


def _hgrn_sample_kernel(q_ref, f_ref, iv_ref, lbl_ref, nw_ref, st_ref, mix_ref, so_ref, o_scr, *, layer, ns):
    lb = _lower_bound(lbl_ref[...], layer)
    fg = lb + (1.0 - lb) * jax.nn.sigmoid(f_ref[...])
    kk_t = (1.0 - fg).T
    fg_t = fg.T
    qs_t = _silu(q_ref[...]).T
    for r in range(ns):
        state = fg_t[:, r:r + 1] * st_ref[r] + kk_t[:, r:r + 1] * iv_ref[r:r + 1, :]
        so_ref[r] = state
        o_scr[r:r + 1, :] = jnp.sum(qs_t[:, r:r + 1] * state, axis=0, keepdims=True)
    o = o_scr[...]
    ms = jnp.mean(o * o, axis=-1, keepdims=True)
    mix_ref[...] = o * lax.rsqrt(ms + RMS_EPS) * nw_ref[...]


def _hgrn_sample_step(u, state, j, lb_logits, norm_w, layer):
    ns = u.shape[0]
    col = lambda part: pl.BlockSpec((ns, HG_D), lambda h: (0, part * HG_HEADS + h))
    kernel = functools.partial(_hgrn_sample_kernel, layer=layer, ns=ns)
    return pl.pallas_call(
        kernel,
        grid=(HG_HEADS,),
        in_specs=[col(0), col(1), col(2),
                  pl.BlockSpec((DEPTH, HG_D), lambda h: (0, h)),
                  pl.BlockSpec((1, HG_D), lambda h: (0, h)),
                  pl.BlockSpec((None, ns, None, HG_D, HG_D), lambda h: (j, 0, h, 0, 0))],
        out_specs=[pl.BlockSpec((ns, HG_D), lambda h: (0, h)),
                   pl.BlockSpec((ns, None, HG_D, HG_D), lambda h: (0, h, 0, 0))],
        out_shape=[jax.ShapeDtypeStruct((ns, HG_K), F32),
                   jax.ShapeDtypeStruct(state.shape[1:], F32)],
        scratch_shapes=[pltpu.VMEM((ns, HG_D), F32)],
        compiler_params=_cparams(1),
        name="hgrn_sample_step",
    )(u, u, u, lb_logits, norm_w.reshape(1, HG_K), state)


def _outproj_ln_kernel(br_ref, h_ref, wout_ref, lnw_ref, lnb_ref, out_ref):
    y = jnp.dot(br_ref[...].astype(BF16), wout_ref[:, :D_MODEL], preferred_element_type=F32)
    out_ref[...] = _layer_norm(DEEPNORM_ALPHA * h_ref[...] + y, lnw_ref[...], lnb_ref[...])


def _outproj_ln(branch, h, w_out, ln_w, ln_b):
    ns, d = h.shape
    full = lambda shape: pl.BlockSpec(shape, lambda i: (0, 0))
    return pl.pallas_call(
        _outproj_ln_kernel,
        grid=(1,),
        in_specs=[full((ns, d)), full((ns, d)), full(w_out.shape), full((1, d)), full((1, d))],
        out_specs=full((ns, d)),
        out_shape=jax.ShapeDtypeStruct((ns, d), F32),
        compiler_params=_cparams(1),
        name="outproj_ln",
    )(branch, h, w_out, ln_w.reshape(1, d), ln_b.reshape(1, d))


def kernel(x_prompt, x_sample, cache_mem_k, cache_mem_v, cache_swa_k, cache_swa_v, state_hgrn, mem_prompt, rel_bias, swa_w_in, swa_sinks, hg_w_in, hg_lb_logits, hg_norm_w, w_mem_k, w_mem_v, w_out, ln_w, ln_b):
    bp, t, d = x_prompt.shape
    ns = x_sample.shape[0]
    assert d == D_MODEL and x_sample.shape[1] == 1 and t % PROMPT_ROWS == 0
    assert cache_swa_k.shape[2] == WINDOW <= PAST_LEN
    assert w_mem_k.shape[0] == DEPTH

    wo = [_pad_cols(w_out[i].astype(BF16)) if i % 2 else w_out[i].astype(BF16) for i in range(DEPTH)]
    mem_k_prompt, mem_v_prompt, mk_bf, mv_bf = _mem_kv_proj(mem_prompt, w_mem_k, w_mem_v)
    cmk, cmv = _head_minor_rows(cache_mem_k), _head_minor_rows(cache_mem_v)
    csk, csv = _rows_minor(cache_swa_k), _rows_minor(cache_swa_v)

    hp = x_prompt
    hs = x_sample.reshape(ns, d)
    swa_kp, swa_vp, swa_ks, swa_vs, hg_sp, hg_ss = [], [], [], [], [], []
    for i in range(DEPTH):
        j = i // 2
        if i % 2 == 0:
            w_in = _pad_cols(swa_w_in[j].astype(BF16))
            hp, kw, vw = _swa_prompt_layer(hp, w_in, wo[i], mk_bf, mv_bf, i, rel_bias, swa_sinks[j],
                                           ln_w[i], ln_b[i], rows=PROMPT_ROWS)
            swa_kp.append(kw.reshape(bp, WINDOW, SWA_KVH, SWA_HD))
            swa_vp.append(vw.reshape(bp, WINDOW, SWA_KVH, SWA_HD))
            u = _proj(hs, w_in, SWA_WIDTH, "sample_swa_proj")
            br, ko, vo = _sample_swa_attend(u, csk, csv, j, cmk, cmv, i, rel_bias, swa_sinks[j])
            swa_ks.append(jnp.moveaxis(ko, -1, -3))
            swa_vs.append(jnp.moveaxis(vo, -1, -3))
        else:
            w_in = _pad_cols(hg_w_in[j].astype(BF16))
            hp, st = _hgrn_prompt_layer(hp, w_in, wo[i], mk_bf, mv_bf, i, hg_lb_logits, hg_norm_w[j],
                                        ln_w[i], ln_b[i])
            hg_sp.append(st)
            u = _proj(hs, w_in, HG_WIDTH, "sample_hgrn_proj")
            mix, so = _hgrn_sample_step(u, state_hgrn, j, hg_lb_logits, hg_norm_w[j], i)
            hg_ss.append(so)
            br = _sample_hgrn_gate(u, mix, cmk, cmv, i)
        hs = _outproj_ln(br, hs, wo[i], ln_w[i], ln_b[i])

    return (hp, hs.reshape(ns, 1, d), mem_k_prompt, mem_v_prompt,
            jnp.stack(swa_kp), jnp.stack(swa_vp), jnp.stack(hg_sp),
            jnp.stack(swa_ks), jnp.stack(swa_vs), jnp.stack(hg_ss))
```

```python
import functools
import math

import numpy as np
import jax
import jax.numpy as jnp
from jax import lax
from jax.experimental import pallas as pl
from jax.experimental.pallas import tpu as pltpu

F32 = jnp.float32
BF16 = jnp.bfloat16

D_MODEL = 2048
DEPTH = 2
PAST_LEN = 8192
N_MEM = 256
MEM_HEADS = 4
MEM_HD = 128
MEM_Q = MEM_HEADS * MEM_HD
WINDOW = 128
SWA_HD = 64
SWA_KVH = 4
SWA_GROUP = 6
SWA_QH = SWA_KVH * SWA_GROUP
SWA_Q = SWA_QH * SWA_HD
SWA_KV = SWA_KVH * SWA_HD
N_BUCKETS = 32
HG_HEADS = 12
HG_D = 128
HG_K = HG_HEADS * HG_D
SWA_WIDTH = SWA_Q + 2 * SWA_KV + MEM_Q + D_MODEL
HG_WIDTH = 3 * HG_K + MEM_Q + D_MODEL
DEEPNORM_ALPHA = (2.0 * DEPTH) ** 0.25
LN_EPS = 1e-5
RMS_EPS = 1e-6
MASKED = -1e30

V7X_SUBLANES = 8
V7X_LANES = 128
V7X_MXU_COLS = 256
V7X_VMEM_LIMIT = 60000 * 1024
PROJ_COLS = 512
GATE_CHUNK = 512
PROMPT_ROWS = 256


def _t5_bucket_ranges():
    n = np.arange(WINDOW)
    max_exact = N_BUCKETS // 2
    nf = np.maximum(n, 1).astype(np.float64)
    large = max_exact + (np.log(nf / max_exact) / math.log(WINDOW / max_exact)
                         * (N_BUCKETS - max_exact)).astype(np.int64)
    bucket = np.where(n < max_exact, n, np.minimum(large, N_BUCKETS - 1))
    ranges = []
    for b in range(N_BUCKETS):
        idx = np.nonzero(bucket == b)[0]
        ranges.append((int(idx[0]), int(idx[-1])) if idx.size else None)
    return ranges


_BUCKET_RANGES = _t5_bucket_ranges()


def _dot(a, b):
    return jnp.dot(a.astype(BF16), b.astype(BF16), preferred_element_type=F32)


def _dot_nt(a, b):
    return lax.dot_general(a.astype(BF16), b.astype(BF16), (((1,), (1,)), ((), ())),
                           preferred_element_type=F32)


def _layer_norm(z, w, b):
    mu = jnp.mean(z, axis=-1, keepdims=True)
    zc = z - mu
    var = jnp.mean(zc * zc, axis=-1, keepdims=True)
    return zc * lax.rsqrt(var + LN_EPS) * w + b


def _silu(x):
    return x * jax.nn.sigmoid(x)


def _cparams(n_axes):
    return pltpu.CompilerParams(dimension_semantics=("arbitrary",) * n_axes,
                                vmem_limit_bytes=V7X_VMEM_LIMIT)


def _resident(shape):
    nd = len(shape)
    return pl.BlockSpec(shape, lambda *_: (0,) * nd, pipeline_mode=pl.Buffered(1))


_SMEM = pl.BlockSpec(memory_space=pltpu.SMEM)


def _mem_kv_kernel(x_ref, wk_ref, wv_ref, ko_ref, vo_ref, kb_ref, vb_ref):
    xb = x_ref[...].astype(BF16)
    for w_ref, o_ref, b_ref in ((wk_ref, ko_ref, kb_ref), (wv_ref, vo_ref, vb_ref)):
        y = jnp.dot(xb, w_ref[...].astype(BF16), preferred_element_type=F32)
        b_ref[...] = y.astype(BF16)
        for hm in range(MEM_HEADS):
            o_ref[:, hm, :] = y[:, hm * MEM_HD:(hm + 1) * MEM_HD]


def _mem_kv_proj(mem, w_k, w_v):
    bsz, n, d = mem.shape
    nl = w_k.shape[0]
    w_spec = pl.BlockSpec((None, d, MEM_Q), lambda l, b: (l, 0, 0))
    o_spec = pl.BlockSpec((None, None, n, MEM_HEADS, MEM_HD), lambda l, b: (l, b, 0, 0, 0))
    b_spec = pl.BlockSpec((None, n, MEM_Q), lambda l, b: (l, b, 0))
    o_shape = jax.ShapeDtypeStruct((nl, bsz, n, MEM_HEADS, MEM_HD), F32)
    b_shape = jax.ShapeDtypeStruct((nl, bsz * n, MEM_Q), BF16)
    return pl.pallas_call(
        _mem_kv_kernel,
        grid=(nl, bsz),
        in_specs=[pl.BlockSpec((None, n, d), lambda l, b: (b, 0, 0)), w_spec, w_spec],
        out_specs=[o_spec, o_spec, b_spec, b_spec],
        out_shape=[o_shape, o_shape, b_shape, b_shape],
        compiler_params=_cparams(2),
        name="mem_kv_proj",
    )(mem, w_k, w_v)


def _proj_kernel(x_ref, w_ref, o_ref):
    o_ref[...] = jnp.dot(x_ref[...].astype(BF16), w_ref[...], preferred_element_type=F32)


def _pad_cols(w):
    if w.shape[-1] % (4 * V7X_MXU_COLS) != 0:
        return w
    return jnp.concatenate([w, jnp.zeros(w.shape[:-1] + (V7X_MXU_COLS,), w.dtype)], axis=-1)


def _proj(x, w, n, name):
    m, k = x.shape
    return pl.pallas_call(
        _proj_kernel,
        grid=(n // PROJ_COLS,),
        in_specs=[pl.BlockSpec((m, k), lambda j: (0, 0)),
                  pl.BlockSpec((k, PROJ_COLS), lambda j: (0, j))],
        out_specs=pl.BlockSpec((m, PROJ_COLS), lambda j: (0, j)),
        out_shape=jax.ShapeDtypeStruct((m, n), F32),
        compiler_params=_cparams(1),
        name=name,
    )(x, w)


def _mem_scores(mq_all, mk_ref):
    return [_dot_nt(mq_all[:, hm * MEM_HD:(hm + 1) * MEM_HD], mk_ref[:, hm * MEM_HD:(hm + 1) * MEM_HD])
            * (MEM_HD ** -0.5) for hm in range(MEM_HEADS)]


def _put_gated(bb_ref, gs_ref, col0, o):
    cols = slice(col0, col0 + o.shape[-1])
    bb_ref[:, cols] = (o * gs_ref[:, cols]).astype(BF16)


def _mem_values(weights, mv_ref, bb_ref, gs_ref, out_col0):
    for hm, (p, l) in enumerate(weights):
        lo = hm * MEM_HD
        _put_gated(bb_ref, gs_ref, out_col0 + lo, _dot(p, mv_ref[:, lo:lo + MEM_HD]) / l)


def _swa_prompt_kernel(relb_ref, sink_ref, h_ref, win_ref, wout_ref, mk_ref, mv_ref, lnw_ref, lnb_ref,
                       out_ref, kout_ref, vout_ref,
                       bias_scr, k_scr, v_scr, gs_scr, bb_scr, *, rows):
    b = pl.program_id(0)
    t = pl.program_id(1)
    w2 = 2 * WINDOW

    @pl.when(jnp.logical_and(b == 0, t == 0))
    def _build_bias():
        qi = lax.broadcasted_iota(jnp.int32, (WINDOW, w2), 0)
        kj = lax.broadcasted_iota(jnp.int32, (WINDOW, w2), 1)
        dist = qi + WINDOW - kj

        def per_head(hq, carry):
            acc = jnp.full((WINDOW, w2), MASKED, F32)
            for bk, rng in enumerate(_BUCKET_RANGES):
                if rng is not None:
                    hit = jnp.logical_and(dist >= rng[0], dist <= rng[1])
                    acc = jnp.where(hit, relb_ref[bk, hq], acc)
            bias_scr[0, hq] = acc
            bias_scr[1, hq] = jnp.where(kj < WINDOW, MASKED, acc)
            return carry

        lax.fori_loop(0, SWA_QH, per_head, 0)

    @pl.when(t == 0)
    def _reset_window():
        k_scr[0:WINDOW, :] = jnp.zeros((WINDOW, SWA_KV), BF16)
        v_scr[0:WINDOW, :] = jnp.zeros((WINDOW, SWA_KV), BF16)

    qkv_w = SWA_Q + 2 * SWA_KV
    n_blocks = rows // WINDOW
    n_gate = D_MODEL // GATE_CHUNK
    heads = [(g, j) for g in range(SWA_KVH) for j in range(SWA_GROUP)]
    xb, qkv, mq, scores, mem_s = {}, {}, {}, {}, {}

    def project(qb):
        r0 = qb * WINDOW
        xb[qb] = h_ref[0, r0:r0 + WINDOW, :].astype(BF16)
        qkv[qb] = jnp.dot(xb[qb], win_ref[:, :qkv_w], preferred_element_type=F32)
        mq[qb] = jnp.dot(xb[qb], win_ref[:, qkv_w:qkv_w + MEM_Q], preferred_element_type=F32)
        k = qkv[qb][:, SWA_Q:SWA_Q + SWA_KV]
        v = qkv[qb][:, SWA_Q + SWA_KV:]
        if qb == n_blocks - 1:
            kout_ref[0] = k
            vout_ref[0] = v
        k_scr[WINDOW + r0:w2 + r0, :] = k.astype(BF16)
        v_scr[WINDOW + r0:w2 + r0, :] = v.astype(BF16)

    def score(qb):
        r0 = qb * WINDOW
        table = jnp.where(t == 0, 1, 0) if qb == 0 else 0
        scores[qb] = []
        for idx, (g, j) in enumerate(heads):
            hq = g * SWA_GROUP + j
            q = qkv[qb][:, hq * SWA_HD:(hq + 1) * SWA_HD] * (SWA_HD ** -0.5)
            scores[qb].append(_dot_nt(q, k_scr[r0:r0 + w2, g * SWA_HD:(g + 1) * SWA_HD]) + bias_scr[table, hq])
            for chunk in range(idx * n_gate // len(heads), (idx + 1) * n_gate // len(heads)):
                lo = chunk * GATE_CHUNK
                gate = jnp.dot(xb[qb], win_ref[:, qkv_w + MEM_Q + lo:qkv_w + MEM_Q + lo + GATE_CHUNK],
                               preferred_element_type=F32)
                gs_scr[r0:r0 + WINDOW, lo:lo + GATE_CHUNK] = _silu(gate)
        mem_s[qb] = _mem_scores(mq[qb], mk_ref)

    def attend(qb):
        r0 = qb * WINDOW
        bb = bb_scr.at[r0:r0 + WINDOW, :]
        gs = gs_scr.at[r0:r0 + WINDOW, :]
        sinks = [sink_ref[g * SWA_GROUP + j] for g, j in heads]
        maxes = [jnp.maximum(jnp.max(s, axis=-1, keepdims=True), sk) for s, sk in zip(scores[qb], sinks)]
        mem_max = [jnp.max(s, axis=-1, keepdims=True) for s in mem_s[qb]]
        probs = [jnp.exp(s - m) for s, m in zip(scores[qb], maxes)]
        mem_p = [jnp.exp(s - m) for s, m in zip(mem_s[qb], mem_max)]
        weights = [(p, jnp.sum(p, axis=-1, keepdims=True) + jnp.exp(sk - m))
                   for p, m, sk in zip(probs, maxes, sinks)]
        mem_w = [(p, jnp.sum(p, axis=-1, keepdims=True)) for p in mem_p]
        for (g, j), (p, l) in zip(heads, weights):
            o = _dot(p, v_scr[r0:r0 + w2, g * SWA_HD:(g + 1) * SWA_HD]) / l
            _put_gated(bb, gs, (g * SWA_GROUP + j) * SWA_HD, o)
        _mem_values(mem_w, mv_ref, bb, gs, SWA_Q)

    def finish(qb):
        r0 = qb * WINDOW
        y = jnp.dot(bb_scr[r0:r0 + WINDOW, :], wout_ref[:, :D_MODEL], preferred_element_type=F32)
        out_ref[0, r0:r0 + WINDOW, :] = _layer_norm(DEEPNORM_ALPHA * h_ref[0, r0:r0 + WINDOW, :] + y,
                                                    lnw_ref[...], lnb_ref[...])

    project(0)
    score(0)
    for qb in range(n_blocks):
        if qb + 1 < n_blocks:
            project(qb + 1)
            score(qb + 1)
        attend(qb)
        finish(qb)

    k_scr[0:WINDOW, :] = k_scr[rows:rows + WINDOW, :]
    v_scr[0:WINDOW, :] = v_scr[rows:rows + WINDOW, :]


def _swa_prompt_layer(h, w_in, w_out, mk_bf, mv_bf, layer, rel_bias, sinks, ln_w, ln_b, rows):
    bsz, t, d = h.shape
    kernel = functools.partial(_swa_prompt_kernel, rows=rows)
    mem_spec = pl.BlockSpec((None, N_MEM, MEM_Q), lambda b, i: (layer, b, 0))
    return pl.pallas_call(
        kernel,
        grid=(bsz, t // rows),
        in_specs=[_SMEM, _SMEM,
                  pl.BlockSpec((1, rows, d), lambda b, i: (b, i, 0)),
                  _resident(w_in.shape),
                  _resident(w_out.shape),
                  mem_spec, mem_spec,
                  _resident((1, d)), _resident((1, d))],
        out_specs=[pl.BlockSpec((1, rows, d), lambda b, i: (b, i, 0)),
                   pl.BlockSpec((1, WINDOW, SWA_KV), lambda b, i: (b, 0, 0)),
                   pl.BlockSpec((1, WINDOW, SWA_KV), lambda b, i: (b, 0, 0))],
        out_shape=[jax.ShapeDtypeStruct((bsz, t, d), F32),
                   jax.ShapeDtypeStruct((bsz, WINDOW, SWA_KV), F32),
                   jax.ShapeDtypeStruct((bsz, WINDOW, SWA_KV), F32)],
        scratch_shapes=[pltpu.VMEM((2, SWA_QH, WINDOW, 2 * WINDOW), F32),
                        pltpu.VMEM((WINDOW + rows, SWA_KV), BF16),
                        pltpu.VMEM((WINDOW + rows, SWA_KV), BF16),
                        pltpu.VMEM((rows, d), F32),
                        pltpu.VMEM((rows, d), BF16)],
        compiler_params=_cparams(2),
        name="swa_prompt_layer",
    )(rel_bias, sinks, h, w_in, w_out, mk_bf, mv_bf, ln_w.reshape(1, d), ln_b.reshape(1, d))


HG_CHUNK = 128
HG_DIRECT = 8


def _lower_bound(logits, layer):
    e = jnp.exp(logits - jnp.max(logits, axis=0, keepdims=True))
    sm = e / jnp.sum(e, axis=0, keepdims=True)
    return jnp.sum(sm[1:layer + 1, :], axis=0, keepdims=True)


def _cumsum_rows(tril_bf, g):
    w = g.shape[1]
    g1 = g.astype(BF16)
    r1 = g - g1.astype(F32)
    g2 = r1.astype(BF16)
    g3 = (r1 - g2.astype(F32)).astype(BF16)
    d = jnp.dot(tril_bf, jnp.concatenate([g1, g2, g3], axis=1), preferred_element_type=F32)
    return (d[:, :w] + d[:, w:2 * w]) + d[:, 2 * w:]


def _hgrn_chunk_operands(qs, kk, gc):
    c = HG_CHUNK
    nb = c // HG_DIRECT
    q3 = qs.reshape(nb, HG_DIRECT, HG_D)
    g3 = gc.reshape(nb, HG_DIRECT, HG_D)
    k3 = kk.reshape(nb, HG_DIRECT, HG_D)
    lane = lax.broadcasted_iota(jnp.int32, (nb, HG_DIRECT, c), 2)
    blk = lax.broadcasted_iota(jnp.int32, (nb, HG_DIRECT, c), 0)
    trow = lax.broadcasted_iota(jnp.int32, (nb, HG_DIRECT, c), 1)
    a3 = jnp.zeros((nb, HG_DIRECT, c), F32)
    for sl in range(HG_DIRECT):
        e = jnp.exp2(g3 - g3[:, sl:sl + 1, :])
        col = jnp.sum(e * q3 * k3[:, sl:sl + 1, :], axis=-1, keepdims=True)
        hit = jnp.logical_and(lane == blk * HG_DIRECT + sl, trow >= sl)
        a3 = jnp.where(hit, col, a3)
    ri = lax.broadcasted_iota(jnp.int32, (c, 1), 0)
    levels = []
    half = HG_DIRECT
    while half < c:
        span = 2 * half
        gsp = gc.reshape(c // span, span, HG_D)
        gref = jnp.broadcast_to(gsp[:, half - 1:half, :], gsp.shape).reshape(c, HG_D)
        upper = (ri % span) >= half
        eq = jnp.where(upper, jnp.exp2(gc - gref), 0.0)
        ek = jnp.where(upper, 0.0, jnp.exp2(gref - gc))
        levels.append(((qs * eq).astype(BF16), (kk * ek).astype(BF16)))
        half = span
    return a3.reshape(c, c), levels


def _hgrn_chunk_products(direct, levels):
    c = HG_CHUNK
    ti = lax.broadcasted_iota(jnp.int32, (c, c), 0)
    si = lax.broadcasted_iota(jnp.int32, (c, c), 1)
    a = direct
    for lvl, (ql, kl) in enumerate(levels):
        span = 2 * HG_DIRECT << lvl
        a = a + jnp.where((ti // span) == (si // span), _dot_nt(ql, kl), 0.0)
    return a


def _hgrn_prompt_kernel(h_ref, win_ref, wout_ref, lbl_ref, nw_ref, mk_ref, mv_ref, lnw_ref, lnb_ref,
                        out_ref, st_ref,
                        st_scr, br_scr, *, layer):
    t = pl.program_id(1)
    c = HG_CHUNK
    heads = range(HG_HEADS)

    @pl.when(t == 0)
    def _reset_state():
        st_scr[...] = jnp.zeros(st_scr.shape, F32)

    lb_all = _lower_bound(lbl_ref[...], layer)
    ri = lax.broadcasted_iota(jnp.int32, (c, c), 0)
    ci = lax.broadcasted_iota(jnp.int32, (c, c), 1)
    tril_bf = jnp.where(ri >= ci, 1.0, 0.0).astype(BF16)

    x = h_ref[0]
    xb = x.astype(BF16)
    pair_w = 2 * HG_D
    n_pairs = HG_HEADS // 2
    qs, kk, iv, gc, operands = [], [], [], [], []

    def project(pair):
        return [jnp.dot(xb, win_ref[:, part * HG_K + pair * pair_w:part * HG_K + (pair + 1) * pair_w],
                        preferred_element_type=F32) for part in range(3)]

    def gates(pair, uq, uf, ui):
        lb = lb_all[:, pair * pair_w:(pair + 1) * pair_w]
        fg = lb + (1.0 - lb) * jax.nn.sigmoid(uf)
        gc2 = _cumsum_rows(tril_bf, jnp.log2(fg))
        for cols in (slice(0, HG_D), slice(HG_D, pair_w)):
            qs.append(_silu(uq[:, cols]))
            iv.append(ui[:, cols])
            kk.append(1.0 - fg[:, cols])
            gc.append(gc2[:, cols])
            operands.append(_hgrn_chunk_operands(qs[-1], kk[-1], gc[-1]))

    a = {}

    def products(pair):
        for hh in (2 * pair, 2 * pair + 1):
            a[hh] = _hgrn_chunk_products(*operands[hh])

    parts = [project(0)]
    for pair in range(1, n_pairs):
        parts.append(project(pair))
        gates(pair - 1, *parts[pair - 1])
        if pair >= 2:
            products(pair - 2)
    ug = jnp.dot(xb, win_ref[:, 3 * HG_K:HG_WIDTH], preferred_element_type=F32)
    gates(n_pairs - 1, *parts[n_pairs - 1])
    products(n_pairs - 2)
    mem_s = _mem_scores(ug[:, :MEM_Q], mk_ref)
    mem_p = [jnp.exp(s - jnp.max(s, axis=-1, keepdims=True)) for s in mem_s]
    for hm, p in enumerate(mem_p):
        lo = hm * MEM_HD
        br_scr[:, HG_K + lo:HG_K + lo + MEM_HD] = (_dot(p, mv_ref[:, lo:lo + MEM_HD])
                                                   / jnp.sum(p, axis=-1, keepdims=True))

    products(n_pairs - 1)

    iv_t = [iv[hh].T for hh in heads]
    for hh in heads:
        lhs = jnp.concatenate([a[hh], qs[hh] * jnp.exp2(gc[hh])], axis=1)
        rhs = jnp.concatenate([iv_t[hh], st_scr[hh]], axis=1)
        o = _dot_nt(lhs, rhs)
        ms = jnp.mean(o * o, axis=-1, keepdims=True)
        br_scr[:, hh * HG_D:(hh + 1) * HG_D] = o * lax.rsqrt(ms + RMS_EPS) * nw_ref[:, hh * HG_D:(hh + 1) * HG_D]

    for hh in heads:
        g_last = gc[hh][c - 1:c, :]
        kdec = kk[hh] * jnp.exp2(g_last - gc[hh])
        st_scr[hh] = st_scr[hh] * jnp.exp2(g_last) + _dot(iv_t[hh], kdec)

    @pl.when(t == pl.num_programs(1) - 1)
    def _emit_state():
        for hh in heads:
            st_ref[0, hh] = st_scr[hh].T

    branch = br_scr[...] * _silu(ug[:, MEM_Q:])
    y = jnp.dot(branch.astype(BF16), wout_ref[:, :D_MODEL], preferred_element_type=F32)
    out_ref[0] = _layer_norm(DEEPNORM_ALPHA * x + y, lnw_ref[...], lnb_ref[...])


def _hgrn_prompt_layer(h, w_in, w_out, mk_bf, mv_bf, layer, lb_logits, norm_w, ln_w, ln_b):
    bsz, t, d = h.shape
    rows = HG_CHUNK
    kernel = functools.partial(_hgrn_prompt_kernel, layer=layer)
    mem_spec = pl.BlockSpec((None, N_MEM, MEM_Q), lambda b, i: (layer, b, 0))
    return pl.pallas_call(
        kernel,
        grid=(bsz, t // rows),
        in_specs=[pl.BlockSpec((1, rows, d), lambda b, i: (b, i, 0)),
                  _resident(w_in.shape),
                  _resident(w_out.shape),
                  _resident((DEPTH, HG_K)),
                  _resident((1, HG_K)),
                  mem_spec, mem_spec,
                  _resident((1, d)), _resident((1, d))],
        out_specs=[pl.BlockSpec((1, rows, d), lambda b, i: (b, i, 0)),
                   pl.BlockSpec((1, HG_HEADS, HG_D, HG_D), lambda b, i: (b, 0, 0, 0))],
        out_shape=[jax.ShapeDtypeStruct((bsz, t, d), F32),
                   jax.ShapeDtypeStruct((bsz, HG_HEADS, HG_D, HG_D), F32)],
        scratch_shapes=[pltpu.VMEM((HG_HEADS, HG_D, HG_D), F32),
                        pltpu.VMEM((rows, d), F32)],
        compiler_params=_cparams(2),
        name="hgrn_prompt_layer",
    )(h, w_in, w_out, lb_logits, norm_w.reshape(1, HG_K), mk_bf, mv_bf, ln_w.reshape(1, d), ln_b.reshape(1, d))


REQS_PER_STEP = 4


def _heads_to_sublanes(row, col0, n, width):
    return jnp.concatenate([row[:, col0 + i * width:col0 + (i + 1) * width] for i in range(n)], axis=0)


def _sample_mem_attend(urow, col0, cmk_ref, cmv_ref, r):
    pieces = []
    for hm in range(MEM_HEADS):
        lo = col0 + hm * MEM_HD
        rows = pl.ds(hm, N_MEM, stride=MEM_HEADS)
        s = jnp.sum(cmk_ref[r, rows, :] * urow[:, lo:lo + MEM_HD], axis=-1, keepdims=True) * (MEM_HD ** -0.5)
        s = jnp.broadcast_to(s, (N_MEM, MEM_HD))
        p = jnp.exp(s - jnp.max(s, axis=0, keepdims=True))
        pieces.append(jnp.sum(p * cmv_ref[r, rows, :], axis=0, keepdims=True) / jnp.sum(p, axis=0, keepdims=True))
    return jnp.concatenate(pieces, axis=1)


def _sample_swa_kernel(relb_ref, sink_ref, u_ref, ck_ref, cv_ref, cmk_ref, cmv_ref,
                       br_ref, ko_ref, vo_ref, bias_scr, bias0_scr, sink_scr):
    nbuf = WINDOW

    @pl.when(pl.program_id(0) == 0)
    def _build_tables():
        dist = nbuf - lax.broadcasted_iota(jnp.int32, (V7X_SUBLANES, nbuf), 1)
        rid = lax.broadcasted_iota(jnp.int32, (V7X_SUBLANES, nbuf), 0)
        for g in range(SWA_KVH):
            acc = jnp.where(rid < SWA_GROUP, MASKED, 0.0).astype(F32)
            acc0 = jnp.zeros((V7X_SUBLANES, nbuf), F32)
            sk = jnp.zeros((V7X_SUBLANES, nbuf), F32)
            for j in range(SWA_GROUP):
                hq = g * SWA_GROUP + j
                sk = jnp.where(rid == j, sink_ref[hq], sk)
                acc0 = jnp.where(rid == j, relb_ref[0, hq], acc0)
                for bk, rng in enumerate(_BUCKET_RANGES):
                    if rng is not None:
                        hit = jnp.logical_and(rid == j, jnp.logical_and(dist >= rng[0], dist <= rng[1]))
                        acc = jnp.where(hit, relb_ref[bk, hq], acc)
            bias_scr[g] = acc
            bias0_scr[g] = acc0
            sink_scr[g] = sk

    g0 = SWA_Q + 2 * SWA_KV + MEM_Q
    pad = jnp.zeros((V7X_SUBLANES - SWA_GROUP, SWA_HD), F32)
    pairs = [(r, g) for r in range(REQS_PER_STEP) for g in range(SWA_KVH)]
    urows = [u_ref[r] for r in range(REQS_PER_STEP)]

    diag = (lax.broadcasted_iota(jnp.int32, (SWA_HD, SWA_HD), 0)
            == lax.broadcasted_iota(jnp.int32, (SWA_HD, SWA_HD), 1))
    newest = lax.broadcasted_iota(jnp.int32, (SWA_HD, nbuf), 1) == nbuf - 1

    def rolled(cache_t, new_row):
        new_col = jnp.sum(jnp.where(diag, new_row, 0.0), axis=-1, keepdims=True)
        return jnp.where(newest, new_col, pltpu.roll(cache_t, nbuf - 1, 1))

    scores = []
    for r, g in pairs:
        k_t = ck_ref[r, g]
        k_new = urows[r][:, SWA_Q + g * SWA_HD:SWA_Q + (g + 1) * SWA_HD]
        v_new = urows[r][:, SWA_Q + SWA_KV + g * SWA_HD:SWA_Q + SWA_KV + (g + 1) * SWA_HD]
        qg = jnp.concatenate(
            [urows[r][:, (g * SWA_GROUP + j) * SWA_HD:(g * SWA_GROUP + j + 1) * SWA_HD] for j in range(SWA_GROUP)]
            + [pad], axis=0) * (SWA_HD ** -0.5)
        s = _dot(qg, k_t) + bias_scr[g]
        s_new = jnp.sum(qg * k_new, axis=-1, keepdims=True) + bias0_scr[g][:, 0:1]
        scores.append((s, s_new, k_new, v_new))

    weights = []
    for (r, g), (s, s_new, _, v_new) in zip(pairs, scores):
        sink = sink_scr[g][:, 0:1]
        m = jnp.maximum(jnp.max(s, axis=-1, keepdims=True), jnp.maximum(s_new, sink))
        p = jnp.exp(s - m)
        p_new = jnp.exp(s_new - m)
        l = jnp.sum(p, axis=-1, keepdims=True) + p_new + jnp.exp(sink - m)
        weights.append((p, p_new * v_new, l))

    pieces = [[] for _ in range(REQS_PER_STEP)]
    for (r, g), (p, o_new, l) in zip(pairs, weights):
        o = (_dot_nt(p, cv_ref[r, g]) + o_new) / l
        pieces[r].extend(o[j:j + 1, :] for j in range(SWA_GROUP))

    for (r, g), (_, _, k_new, v_new) in zip(pairs, scores):
        ko_ref[r, g] = rolled(ck_ref[r, g], k_new)
        vo_ref[r, g] = rolled(cv_ref[r, g], v_new)

    for r in range(REQS_PER_STEP):
        mem_o = _sample_mem_attend(urows[r], SWA_Q + 2 * SWA_KV, cmk_ref, cmv_ref, r)
        br_ref[r] = jnp.concatenate(pieces[r] + [mem_o], axis=1) * _silu(urows[r][:, g0:g0 + D_MODEL])


def _cache_spec(layer, shape):
    return pl.BlockSpec((None, REQS_PER_STEP) + shape, lambda i: (layer, i) + (0,) * len(shape))


def _row_spec(width):
    return pl.BlockSpec((REQS_PER_STEP, 1, width), lambda i: (i, 0, 0))


def _head_minor_rows(cache):
    nl, ns, rows, heads, hd = cache.shape
    return cache.reshape(nl, ns, rows * heads, hd)


def _rows_minor(cache):
    return jnp.moveaxis(cache, -3, -1)


def _sample_swa_attend(u, cache_k, cache_v, j, cache_mk, cache_mv, layer, rel_bias, sinks):
    ns = u.shape[0]
    nbuf = cache_k.shape[-1]
    swa_shape = (SWA_KVH, SWA_HD, nbuf)
    mem_shape = (N_MEM * MEM_HEADS, MEM_HD)
    new_cache = pl.BlockSpec((REQS_PER_STEP,) + swa_shape, lambda i: (i, 0, 0, 0))
    br, ko, vo = pl.pallas_call(
        _sample_swa_kernel,
        grid=(ns // REQS_PER_STEP,),
        in_specs=[_SMEM, _SMEM, _row_spec(SWA_WIDTH), _cache_spec(j, swa_shape), _cache_spec(j, swa_shape),
                  _cache_spec(layer, mem_shape), _cache_spec(layer, mem_shape)],
        out_specs=[_row_spec(D_MODEL), new_cache, new_cache],
        out_shape=[jax.ShapeDtypeStruct((ns, 1, D_MODEL), F32),
                   jax.ShapeDtypeStruct((ns,) + swa_shape, F32),
                   jax.ShapeDtypeStruct((ns,) + swa_shape, F32)],
        scratch_shapes=[pltpu.VMEM((SWA_KVH, V7X_SUBLANES, nbuf), F32)] * 3,
        compiler_params=_cparams(1),
        name="sample_swa_attend",
    )(rel_bias, sinks, u.reshape(ns, 1, SWA_WIDTH), cache_k, cache_v, cache_mk, cache_mv)
    return br.reshape(ns, D_MODEL), ko, vo


def _sample_hgrn_gate_kernel(u_ref, mix_ref, cmk_ref, cmv_ref, br_ref):
    g0 = 3 * HG_K + MEM_Q
    for r in range(REQS_PER_STEP):
        urow = u_ref[r]
        mem_o = _sample_mem_attend(urow, 3 * HG_K, cmk_ref, cmv_ref, r)
        br_ref[r] = jnp.concatenate([mix_ref[r], mem_o], axis=1) * _silu(urow[:, g0:g0 + D_MODEL])


def _sample_hgrn_gate(u, mix, cache_mk, cache_mv, layer):
    ns = u.shape[0]
    mem_shape = (N_MEM * MEM_HEADS, MEM_HD)
    br = pl.pallas_call(
        _sample_hgrn_gate_kernel,
        grid=(ns // REQS_PER_STEP,),
        in_specs=[_row_spec(HG_WIDTH), _row_spec(HG_K), _cache_spec(layer, mem_shape), _cache_spec(layer, mem_shape)],
        out_specs=_row_spec(D_MODEL),
        out_shape=jax.ShapeDtypeStruct((ns, 1, D_MODEL), F32),
        compiler_params=_cparams(1),
        name="sample_hgrn_gate",
    )(u.reshape(ns, 1, HG_WIDTH), mix.reshape(ns, 1, HG_K), cache_mk, cache_mv)
    return br.reshape(ns, D_MODEL)


def _hgrn_sample_kernel(q_ref, f_ref, iv_ref, lbl_ref, nw_ref, st_ref, mix_ref, so_ref, o_scr, *, layer, ns):
    lb = _lower_bound(lbl_ref[...], layer)
    fg = lb + (1.0 - lb) * jax.nn.sigmoid(f_ref[...])
    kk_t = (1.0 - fg).T
    fg_t = fg.T
    qs_t = _silu(q_ref[...]).T
    for r in range(ns):
        state = fg_t[:, r:r + 1] * st_ref[r] + kk_t[:, r:r + 1] * iv_ref[r:r + 1, :]
        so_ref[r] = state
        o_scr[r:r + 1, :] = jnp.sum(qs_t[:, r:r + 1] * state, axis=0, keepdims=True)
    o = o_scr[...]
    ms = jnp.mean(o * o, axis=-1, keepdims=True)
    mix_ref[...] = o * lax.rsqrt(ms + RMS_EPS) * nw_ref[...]


def _hgrn_sample_step(u, state, j, lb_logits, norm_w, layer):
    ns = u.shape[0]
    col = lambda part: pl.BlockSpec((ns, HG_D), lambda h: (0, part * HG_HEADS + h))
    kernel = functools.partial(_hgrn_sample_kernel, layer=layer, ns=ns)
    return pl.pallas_call(
        kernel,
        grid=(HG_HEADS,),
        in_specs=[col(0), col(1), col(2),
                  pl.BlockSpec((DEPTH, HG_D), lambda h: (0, h)),
                  pl.BlockSpec((1, HG_D), lambda h: (0, h)),
                  pl.BlockSpec((None, ns, None, HG_D, HG_D), lambda h: (j, 0, h, 0, 0))],
        out_specs=[pl.BlockSpec((ns, HG_D), lambda h: (0, h)),
                   pl.BlockSpec((ns, None, HG_D, HG_D), lambda h: (0, h, 0, 0))],
        out_shape=[jax.ShapeDtypeStruct((ns, HG_K), F32),
                   jax.ShapeDtypeStruct(state.shape[1:], F32)],
        scratch_shapes=[pltpu.VMEM((ns, HG_D), F32)],
        compiler_params=_cparams(1),
        name="hgrn_sample_step",
    )(u, u, u, lb_logits, norm_w.reshape(1, HG_K), state)


def _outproj_ln_kernel(br_ref, h_ref, wout_ref, lnw_ref, lnb_ref, out_ref):
    y = jnp.dot(br_ref[...].astype(BF16), wout_ref[:, :D_MODEL], preferred_element_type=F32)
    out_ref[...] = _layer_norm(DEEPNORM_ALPHA * h_ref[...] + y, lnw_ref[...], lnb_ref[...])


def _outproj_ln(branch, h, w_out, ln_w, ln_b):
    ns, d = h.shape
    full = lambda shape: pl.BlockSpec(shape, lambda i: (0, 0))
    return pl.pallas_call(
        _outproj_ln_kernel,
        grid=(1,),
        in_specs=[full((ns, d)), full((ns, d)), full(w_out.shape), full((1, d)), full((1, d))],
        out_specs=full((ns, d)),
        out_shape=jax.ShapeDtypeStruct((ns, d), F32),
        compiler_params=_cparams(1),
        name="outproj_ln",
    )(branch, h, w_out, ln_w.reshape(1, d), ln_b.reshape(1, d))


def kernel(x_prompt, x_sample, cache_mem_k, cache_mem_v, cache_swa_k, cache_swa_v, state_hgrn, mem_prompt, rel_bias, swa_w_in, swa_sinks, hg_w_in, hg_lb_logits, hg_norm_w, w_mem_k, w_mem_v, w_out, ln_w, ln_b):
    bp, t, d = x_prompt.shape
    ns = x_sample.shape[0]
    assert d == D_MODEL and x_sample.shape[1] == 1 and t % PROMPT_ROWS == 0
    assert cache_swa_k.shape[2] == WINDOW <= PAST_LEN
    assert w_mem_k.shape[0] == DEPTH

    wo = [_pad_cols(w_out[i].astype(BF16)) if i % 2 else w_out[i].astype(BF16) for i in range(DEPTH)]
    mem_k_prompt, mem_v_prompt, mk_bf, mv_bf = _mem_kv_proj(mem_prompt, w_mem_k, w_mem_v)
    cmk, cmv = _head_minor_rows(cache_mem_k), _head_minor_rows(cache_mem_v)
    csk, csv = _rows_minor(cache_swa_k), _rows_minor(cache_swa_v)

    hp = x_prompt
    hs = x_sample.reshape(ns, d)
    swa_kp, swa_vp, swa_ks, swa_vs, hg_sp, hg_ss = [], [], [], [], [], []
    for i in range(DEPTH):
        j = i // 2
        if i % 2 == 0:
            w_in = _pad_cols(swa_w_in[j].astype(BF16))
            hp, kw, vw = _swa_prompt_layer(hp, w_in, wo[i], mk_bf, mv_bf, i, rel_bias, swa_sinks[j],
                                           ln_w[i], ln_b[i], rows=PROMPT_ROWS)
            swa_kp.append(kw.reshape(bp, WINDOW, SWA_KVH, SWA_HD))
            swa_vp.append(vw.reshape(bp, WINDOW, SWA_KVH, SWA_HD))
            u = _proj(hs, w_in, SWA_WIDTH, "sample_swa_proj")
            br, ko, vo = _sample_swa_attend(u, csk, csv, j, cmk, cmv, i, rel_bias, swa_sinks[j])
            swa_ks.append(jnp.moveaxis(ko, -1, -3))
            swa_vs.append(jnp.moveaxis(vo, -1, -3))
        else:
            w_in = _pad_cols(hg_w_in[j].astype(BF16))
            hp, st = _hgrn_prompt_layer(hp, w_in, wo[i], mk_bf, mv_bf, i, hg_lb_logits, hg_norm_w[j],
                                        ln_w[i], ln_b[i])
            hg_sp.append(st)
            u = _proj(hs, w_in, HG_WIDTH, "sample_hgrn_proj")
            mix, so = _hgrn_sample_step(u, state_hgrn, j, hg_lb_logits, hg_norm_w[j], i)
            hg_ss.append(so)
            br = _sample_hgrn_gate(u, mix, cmk, cmv, i)
        hs = _outproj_ln(br, hs, wo[i], ln_w[i], ln_b[i])

    return (hp, hs.reshape(ns, 1, d), mem_k_prompt, mem_v_prompt,
            jnp.stack(swa_kp), jnp.stack(swa_vp), jnp.stack(hg_sp),
            jnp.stack(swa_ks), jnp.stack(swa_vs), jnp.stack(hg_ss))
```

```python
import functools
import math

import numpy as np
import jax
import jax.numpy as jnp
from jax import lax
from jax.experimental import pallas as pl
from jax.experimental.pallas import tpu as pltpu

F32 = jnp.float32
BF16 = jnp.bfloat16

D_MODEL = 2048
DEPTH = 2
PAST_LEN = 8192
N_MEM = 256
MEM_HEADS = 4
MEM_HD = 128
MEM_Q = MEM_HEADS * MEM_HD
WINDOW = 128
SWA_HD = 64
SWA_KVH = 4
SWA_GROUP = 6
SWA_QH = SWA_KVH * SWA_GROUP
SWA_Q = SWA_QH * SWA_HD
SWA_KV = SWA_KVH * SWA_HD
N_BUCKETS = 32
HG_HEADS = 12
HG_D = 128
HG_K = HG_HEADS * HG_D
SWA_WIDTH = SWA_Q + 2 * SWA_KV + MEM_Q + D_MODEL
HG_WIDTH = 3 * HG_K + MEM_Q + D_MODEL
DEEPNORM_ALPHA = (2.0 * DEPTH) ** 0.25
LN_EPS = 1e-5
RMS_EPS = 1e-6
MASKED = -1e30

V7X_SUBLANES = 8
V7X_LANES = 128
V7X_MXU_COLS = 256
V7X_VMEM_LIMIT = 60000 * 1024
PROJ_COLS = 512
GATE_CHUNK = 512
PROMPT_ROWS = 256


def _t5_bucket_ranges():
    n = np.arange(WINDOW)
    max_exact = N_BUCKETS // 2
    nf = np.maximum(n, 1).astype(np.float64)
    large = max_exact + (np.log(nf / max_exact) / math.log(WINDOW / max_exact)
                         * (N_BUCKETS - max_exact)).astype(np.int64)
    bucket = np.where(n < max_exact, n, np.minimum(large, N_BUCKETS - 1))
    ranges = []
    for b in range(N_BUCKETS):
        idx = np.nonzero(bucket == b)[0]
        ranges.append((int(idx[0]), int(idx[-1])) if idx.size else None)
    return ranges


_BUCKET_RANGES = _t5_bucket_ranges()


def _dot(a, b):
    return jnp.dot(a.astype(BF16), b.astype(BF16), preferred_element_type=F32)


def _dot_nt(a, b):
    return lax.dot_general(a.astype(BF16), b.astype(BF16), (((1,), (1,)), ((), ())),
                           preferred_element_type=F32)


def _layer_norm(z, w, b):
    mu = jnp.mean(z, axis=-1, keepdims=True)
    zc = z - mu
    var = jnp.mean(zc * zc, axis=-1, keepdims=True)
    return zc * lax.rsqrt(var + LN_EPS) * w + b


def _silu(x):
    return x * jax.nn.sigmoid(x)


def _cparams(n_axes):
    return pltpu.CompilerParams(dimension_semantics=("arbitrary",) * n_axes,
                                vmem_limit_bytes=V7X_VMEM_LIMIT)


def _resident(shape):
    nd = len(shape)
    return pl.BlockSpec(shape, lambda *_: (0,) * nd, pipeline_mode=pl.Buffered(1))


_SMEM = pl.BlockSpec(memory_space=pltpu.SMEM)


def _mem_kv_kernel(x_ref, wk_ref, wv_ref, ko_ref, vo_ref, kb_ref, vb_ref):
    xb = x_ref[...].astype(BF16)
    for w_ref, o_ref, b_ref in ((wk_ref, ko_ref, kb_ref), (wv_ref, vo_ref, vb_ref)):
        y = jnp.dot(xb, w_ref[...].astype(BF16), preferred_element_type=F32)
        b_ref[...] = y.astype(BF16)
        for hm in range(MEM_HEADS):
            o_ref[:, hm, :] = y[:, hm * MEM_HD:(hm + 1) * MEM_HD]


def _mem_kv_proj(mem, w_k, w_v):
    bsz, n, d = mem.shape
    nl = w_k.shape[0]
    w_spec = pl.BlockSpec((None, d, MEM_Q), lambda l, b: (l, 0, 0))
    o_spec = pl.BlockSpec((None, None, n, MEM_HEADS, MEM_HD), lambda l, b: (l, b, 0, 0, 0))
    b_spec = pl.BlockSpec((None, n, MEM_Q), lambda l, b: (l, b, 0))
    o_shape = jax.ShapeDtypeStruct((nl, bsz, n, MEM_HEADS, MEM_HD), F32)
    b_shape = jax.ShapeDtypeStruct((nl, bsz * n, MEM_Q), BF16)
    return pl.pallas_call(
        _mem_kv_kernel,
        grid=(nl, bsz),
        in_specs=[pl.BlockSpec((None, n, d), lambda l, b: (b, 0, 0)), w_spec, w_spec],
        out_specs=[o_spec, o_spec, b_spec, b_spec],
        out_shape=[o_shape, o_shape, b_shape, b_shape],
        compiler_params=_cparams(2),
        name="mem_kv_proj",
    )(mem, w_k, w_v)


def _proj_kernel(x_ref, w_ref, o_ref):
    o_ref[...] = jnp.dot(x_ref[...].astype(BF16), w_ref[...], preferred_element_type=F32)


def _pad_cols(w):
    if w.shape[-1] % (4 * V7X_MXU_COLS) != 0:
        return w
    return jnp.concatenate([w, jnp.zeros(w.shape[:-1] + (V7X_MXU_COLS,), w.dtype)], axis=-1)


def _proj(x, w, n, name):
    m, k = x.shape
    return pl.pallas_call(
        _proj_kernel,
        grid=(n // PROJ_COLS,),
        in_specs=[pl.BlockSpec((m, k), lambda j: (0, 0)),
                  pl.BlockSpec((k, PROJ_COLS), lambda j: (0, j))],
        out_specs=pl.BlockSpec((m, PROJ_COLS), lambda j: (0, j)),
        out_shape=jax.ShapeDtypeStruct((m, n), F32),
        compiler_params=_cparams(1),
        name=name,
    )(x, w)


def _mem_scores(mq_all, mk_ref):
    return [_dot_nt(mq_all[:, hm * MEM_HD:(hm + 1) * MEM_HD], mk_ref[:, hm * MEM_HD:(hm + 1) * MEM_HD])
            * (MEM_HD ** -0.5) for hm in range(MEM_HEADS)]


def _put_gated(bb_ref, gs_ref, col0, o):
    cols = slice(col0, col0 + o.shape[-1])
    bb_ref[:, cols] = (o * gs_ref[:, cols]).astype(BF16)


def _mem_values(weights, mv_ref, bb_ref, gs_ref, out_col0):
    for hm, (p, l) in enumerate(weights):
        lo = hm * MEM_HD
        _put_gated(bb_ref, gs_ref, out_col0 + lo, _dot(p, mv_ref[:, lo:lo + MEM_HD]) / l)


def _swa_prompt_kernel(relb_ref, sink_ref, h_ref, win_ref, wout_ref, mk_ref, mv_ref, lnw_ref, lnb_ref,
                       out_ref, kout_ref, vout_ref,
                       bias_scr, k_scr, v_scr, gs_scr, bb_scr, *, rows):
    b = pl.program_id(0)
    t = pl.program_id(1)
    w2 = 2 * WINDOW

    @pl.when(jnp.logical_and(b == 0, t == 0))
    def _build_bias():
        qi = lax.broadcasted_iota(jnp.int32, (WINDOW, w2), 0)
        kj = lax.broadcasted_iota(jnp.int32, (WINDOW, w2), 1)
        dist = qi + WINDOW - kj

        def per_head(hq, carry):
            acc = jnp.full((WINDOW, w2), MASKED, F32)
            for bk, rng in enumerate(_BUCKET_RANGES):
                if rng is not None:
                    hit = jnp.logical_and(dist >= rng[0], dist <= rng[1])
                    acc = jnp.where(hit, relb_ref[bk, hq], acc)
            bias_scr[0, hq] = acc
            bias_scr[1, hq] = jnp.where(kj < WINDOW, MASKED, acc)
            return carry

        lax.fori_loop(0, SWA_QH, per_head, 0)

    @pl.when(t == 0)
    def _reset_window():
        k_scr[0:WINDOW, :] = jnp.zeros((WINDOW, SWA_KV), BF16)
        v_scr[0:WINDOW, :] = jnp.zeros((WINDOW, SWA_KV), BF16)

    qkv_w = SWA_Q + 2 * SWA_KV
    n_blocks = rows // WINDOW
    n_gate = D_MODEL // GATE_CHUNK
    heads = [(g, j) for g in range(SWA_KVH) for j in range(SWA_GROUP)]
    xb, qkv, mq, scores, mem_s = {}, {}, {}, {}, {}

    def project(qb):
        r0 = qb * WINDOW
        xb[qb] = h_ref[0, r0:r0 + WINDOW, :].astype(BF16)
        qkv[qb] = jnp.dot(xb[qb], win_ref[:, :qkv_w], preferred_element_type=F32)
        mq[qb] = jnp.dot(xb[qb], win_ref[:, qkv_w:qkv_w + MEM_Q], preferred_element_type=F32)
        k = qkv[qb][:, SWA_Q:SWA_Q + SWA_KV]
        v = qkv[qb][:, SWA_Q + SWA_KV:]
        if qb == n_blocks - 1:
            kout_ref[0] = k
            vout_ref[0] = v
        k_scr[WINDOW + r0:w2 + r0, :] = k.astype(BF16)
        v_scr[WINDOW + r0:w2 + r0, :] = v.astype(BF16)

    def score(qb):
        r0 = qb * WINDOW
        table = jnp.where(t == 0, 1, 0) if qb == 0 else 0
        scores[qb] = []
        for idx, (g, j) in enumerate(heads):
            hq = g * SWA_GROUP + j
            q = qkv[qb][:, hq * SWA_HD:(hq + 1) * SWA_HD] * (SWA_HD ** -0.5)
            scores[qb].append(_dot_nt(q, k_scr[r0:r0 + w2, g * SWA_HD:(g + 1) * SWA_HD]) + bias_scr[table, hq])
            for chunk in range(idx * n_gate // len(heads), (idx + 1) * n_gate // len(heads)):
                lo = chunk * GATE_CHUNK
                gate = jnp.dot(xb[qb], win_ref[:, qkv_w + MEM_Q + lo:qkv_w + MEM_Q + lo + GATE_CHUNK],
                               preferred_element_type=F32)
                gs_scr[r0:r0 + WINDOW, lo:lo + GATE_CHUNK] = _silu(gate)
        mem_s[qb] = _mem_scores(mq[qb], mk_ref)

    def attend(qb):
        r0 = qb * WINDOW
        bb = bb_scr.at[r0:r0 + WINDOW, :]
        gs = gs_scr.at[r0:r0 + WINDOW, :]
        sinks = [sink_ref[g * SWA_GROUP + j] for g, j in heads]
        maxes = [jnp.maximum(jnp.max(s, axis=-1, keepdims=True), sk) for s, sk in zip(scores[qb], sinks)]
        mem_max = [jnp.max(s, axis=-1, keepdims=True) for s in mem_s[qb]]
        probs = [jnp.exp(s - m) for s, m in zip(scores[qb], maxes)]
        mem_p = [jnp.exp(s - m) for s, m in zip(mem_s[qb], mem_max)]
        weights = [(p, jnp.sum(p, axis=-1, keepdims=True) + jnp.exp(sk - m))
                   for p, m, sk in zip(probs, maxes, sinks)]
        mem_w = [(p, jnp.sum(p, axis=-1, keepdims=True)) for p in mem_p]
        for (g, j), (p, l) in zip(heads, weights):
            o = _dot(p, v_scr[r0:r0 + w2, g * SWA_HD:(g + 1) * SWA_HD]) / l
            _put_gated(bb, gs, (g * SWA_GROUP + j) * SWA_HD, o)
        _mem_values(mem_w, mv_ref, bb, gs, SWA_Q)

    def finish(qb):
        r0 = qb * WINDOW
        y = jnp.dot(bb_scr[r0:r0 + WINDOW, :], wout_ref[:, :D_MODEL], preferred_element_type=F32)
        out_ref[0, r0:r0 + WINDOW, :] = _layer_norm(DEEPNORM_ALPHA * h_ref[0, r0:r0 + WINDOW, :] + y,
                                                    lnw_ref[...], lnb_ref[...])

    project(0)
    score(0)
    for qb in range(n_blocks):
        if qb + 1 < n_blocks:
            project(qb + 1)
            score(qb + 1)
        attend(qb)
        finish(qb)

    k_scr[0:WINDOW, :] = k_scr[rows:rows + WINDOW, :]
    v_scr[0:WINDOW, :] = v_scr[rows:rows + WINDOW, :]


def _swa_prompt_layer(h, w_in, w_out, mk_bf, mv_bf, layer, rel_bias, sinks, ln_w, ln_b, rows):
    bsz, t, d = h.shape
    kernel = functools.partial(_swa_prompt_kernel, rows=rows)
    mem_spec = pl.BlockSpec((None, N_MEM, MEM_Q), lambda b, i: (layer, b, 0))
    return pl.pallas_call(
        kernel,
        grid=(bsz, t // rows),
        in_specs=[_SMEM, _SMEM,
                  pl.BlockSpec((1, rows, d), lambda b, i: (b, i, 0)),
                  _resident(w_in.shape),
                  _resident(w_out.shape),
                  mem_spec, mem_spec,
                  _resident((1, d)), _resident((1, d))],
        out_specs=[pl.BlockSpec((1, rows, d), lambda b, i: (b, i, 0)),
                   pl.BlockSpec((1, WINDOW, SWA_KV), lambda b, i: (b, 0, 0)),
                   pl.BlockSpec((1, WINDOW, SWA_KV), lambda b, i: (b, 0, 0))],
        out_shape=[jax.ShapeDtypeStruct((bsz, t, d), F32),
                   jax.ShapeDtypeStruct((bsz, WINDOW, SWA_KV), F32),
                   jax.ShapeDtypeStruct((bsz, WINDOW, SWA_KV), F32)],
        scratch_shapes=[pltpu.VMEM((2, SWA_QH, WINDOW, 2 * WINDOW), F32),
                        pltpu.VMEM((WINDOW + rows, SWA_KV), BF16),
                        pltpu.VMEM((WINDOW + rows, SWA_KV), BF16),
                        pltpu.VMEM((rows, d), F32),
                        pltpu.VMEM((rows, d), BF16)],
        compiler_params=_cparams(2),
        name="swa_prompt_layer",
    )(rel_bias, sinks, h, w_in, w_out, mk_bf, mv_bf, ln_w.reshape(1, d), ln_b.reshape(1, d))


HG_CHUNK = 128
HG_DIRECT = 8


def _lower_bound(logits, layer):
    e = jnp.exp(logits - jnp.max(logits, axis=0, keepdims=True))
    sm = e / jnp.sum(e, axis=0, keepdims=True)
    return jnp.sum(sm[1:layer + 1, :], axis=0, keepdims=True)


def _cumsum_rows(tril_bf, g):
    w = g.shape[1]
    g1 = g.astype(BF16)
    r1 = g - g1.astype(F32)
    g2 = r1.astype(BF16)
    g3 = (r1 - g2.astype(F32)).astype(BF16)
    d = jnp.dot(tril_bf, jnp.concatenate([g1, g2, g3], axis=1), preferred_element_type=F32)
    return (d[:, :w] + d[:, w:2 * w]) + d[:, 2 * w:]


def _hgrn_chunk_operands(qs, kk, gc):
    c = HG_CHUNK
    nb = c // HG_DIRECT
    q3 = qs.reshape(nb, HG_DIRECT, HG_D)
    g3 = gc.reshape(nb, HG_DIRECT, HG_D)
    k3 = kk.reshape(nb, HG_DIRECT, HG_D)
    lane = lax.broadcasted_iota(jnp.int32, (nb, HG_DIRECT, c), 2)
    blk = lax.broadcasted_iota(jnp.int32, (nb, HG_DIRECT, c), 0)
    trow = lax.broadcasted_iota(jnp.int32, (nb, HG_DIRECT, c), 1)
    a3 = jnp.zeros((nb, HG_DIRECT, c), F32)
    for sl in range(HG_DIRECT):
        e = jnp.exp2(g3 - g3[:, sl:sl + 1, :])
        col = jnp.sum(e * q3 * k3[:, sl:sl + 1, :], axis=-1, keepdims=True)
        hit = jnp.logical_and(lane == blk * HG_DIRECT + sl, trow >= sl)
        a3 = jnp.where(hit, col, a3)
    ri = lax.broadcasted_iota(jnp.int32, (c, 1), 0)
    levels = []
    half = HG_DIRECT
    while half < c:
        span = 2 * half
        gsp = gc.reshape(c // span, span, HG_D)
        gref = jnp.broadcast_to(gsp[:, half - 1:half, :], gsp.shape).reshape(c, HG_D)
        upper = (ri % span) >= half
        eq = jnp.where(upper, jnp.exp2(gc - gref), 0.0)
        ek = jnp.where(upper, 0.0, jnp.exp2(gref - gc))
        levels.append(((qs * eq).astype(BF16), (kk * ek).astype(BF16)))
        half = span
    return a3.reshape(c, c), levels


def _hgrn_chunk_products(direct, levels):
    c = HG_CHUNK
    ti = lax.broadcasted_iota(jnp.int32, (c, c), 0)
    si = lax.broadcasted_iota(jnp.int32, (c, c), 1)
    a = direct
    for lvl, (ql, kl) in enumerate(levels):
        span = 2 * HG_DIRECT << lvl
        s = _dot_nt(ql, kl)
        a = a + (s if span == c else jnp.where((ti // span) == (si // span), s, 0.0))
    return a


def _hgrn_prompt_kernel(h_ref, win_ref, wout_ref, lbl_ref, nw_ref, mk_ref, mv_ref, lnw_ref, lnb_ref,
                        out_ref, st_ref,
                        st_scr, br_scr, *, layer):
    t = pl.program_id(1)
    c = HG_CHUNK
    heads = range(HG_HEADS)

    @pl.when(t == 0)
    def _reset_state():
        st_scr[...] = jnp.zeros(st_scr.shape, F32)

    lb_all = _lower_bound(lbl_ref[...], layer)
    ri = lax.broadcasted_iota(jnp.int32, (c, c), 0)
    ci = lax.broadcasted_iota(jnp.int32, (c, c), 1)
    tril_bf = jnp.where(ri >= ci, 1.0, 0.0).astype(BF16)

    x = h_ref[0]
    xb = x.astype(BF16)
    pair_w = 2 * HG_D
    n_pairs = HG_HEADS // 2
    qs, kk, iv, gc, operands = [], [], [], [], []

    def project(pair):
        return [jnp.dot(xb, win_ref[:, part * HG_K + pair * pair_w:part * HG_K + (pair + 1) * pair_w],
                        preferred_element_type=F32) for part in range(3)]

    def gates(pair, uq, uf, ui):
        lb = lb_all[:, pair * pair_w:(pair + 1) * pair_w]
        fg = lb + (1.0 - lb) * jax.nn.sigmoid(uf)
        gc2 = _cumsum_rows(tril_bf, jnp.log2(fg))
        for cols in (slice(0, HG_D), slice(HG_D, pair_w)):
            qs.append(_silu(uq[:, cols]))
            iv.append(ui[:, cols])
            kk.append(1.0 - fg[:, cols])
            gc.append(gc2[:, cols])
            operands.append(_hgrn_chunk_operands(qs[-1], kk[-1], gc[-1]))

    a = {}

    def products(pair):
        for hh in (2 * pair, 2 * pair + 1):
            a[hh] = _hgrn_chunk_products(*operands[hh])

    ug_w = (MEM_Q + D_MODEL) // (n_pairs - 1)
    ug_cols = []
    parts = [project(0)]
    for pair in range(1, n_pairs):
        parts.append(project(pair))
        gates(pair - 1, *parts[pair - 1])
        lo = 3 * HG_K + (pair - 1) * ug_w
        ug_cols.append(jnp.dot(xb, win_ref[:, lo:lo + ug_w], preferred_element_type=F32))
        if pair >= 2:
            products(pair - 2)
    ug = jnp.concatenate(ug_cols, axis=1)
    gates(n_pairs - 1, *parts[n_pairs - 1])
    products(n_pairs - 2)
    mem_s = _mem_scores(ug[:, :MEM_Q], mk_ref)
    mem_p = [jnp.exp(s - jnp.max(s, axis=-1, keepdims=True)) for s in mem_s]
    for hm, p in enumerate(mem_p):
        lo = hm * MEM_HD
        br_scr[:, HG_K + lo:HG_K + lo + MEM_HD] = (_dot(p, mv_ref[:, lo:lo + MEM_HD])
                                                   / jnp.sum(p, axis=-1, keepdims=True))

    products(n_pairs - 1)

    iv_t = [iv[hh].T for hh in heads]
    for hh in heads:
        lhs = jnp.concatenate([a[hh], qs[hh] * jnp.exp2(gc[hh])], axis=1)
        rhs = jnp.concatenate([iv_t[hh], st_scr[hh]], axis=1)
        o = _dot_nt(lhs, rhs)
        ms = jnp.mean(o * o, axis=-1, keepdims=True)
        br_scr[:, hh * HG_D:(hh + 1) * HG_D] = o * lax.rsqrt(ms + RMS_EPS) * nw_ref[:, hh * HG_D:(hh + 1) * HG_D]

    for hh in heads:
        g_last = gc[hh][c - 1:c, :]
        kdec = kk[hh] * jnp.exp2(g_last - gc[hh])
        st_scr[hh] = st_scr[hh] * jnp.exp2(g_last) + _dot(iv_t[hh], kdec)

    @pl.when(t == pl.num_programs(1) - 1)
    def _emit_state():
        for hh in heads:
            st_ref[0, hh] = st_scr[hh].T

    branch = br_scr[...] * _silu(ug[:, MEM_Q:])
    y = jnp.dot(branch.astype(BF16), wout_ref[:, :D_MODEL], preferred_element_type=F32)
    out_ref[0] = _layer_norm(DEEPNORM_ALPHA * x + y, lnw_ref[...], lnb_ref[...])


def _hgrn_prompt_layer(h, w_in, w_out, mk_bf, mv_bf, layer, lb_logits, norm_w, ln_w, ln_b):
    bsz, t, d = h.shape
    rows = HG_CHUNK
    kernel = functools.partial(_hgrn_prompt_kernel, layer=layer)
    mem_spec = pl.BlockSpec((None, N_MEM, MEM_Q), lambda b, i: (layer, b, 0))
    return pl.pallas_call(
        kernel,
        grid=(bsz, t // rows),
        in_specs=[pl.BlockSpec((1, rows, d), lambda b, i: (b, i, 0)),
                  _resident(w_in.shape),
                  _resident(w_out.shape),
                  _resident((DEPTH, HG_K)),
                  _resident((1, HG_K)),
                  mem_spec, mem_spec,
                  _resident((1, d)), _resident((1, d))],
        out_specs=[pl.BlockSpec((1, rows, d), lambda b, i: (b, i, 0)),
                   pl.BlockSpec((1, HG_HEADS, HG_D, HG_D), lambda b, i: (b, 0, 0, 0))],
        out_shape=[jax.ShapeDtypeStruct((bsz, t, d), F32),
                   jax.ShapeDtypeStruct((bsz, HG_HEADS, HG_D, HG_D), F32)],
        scratch_shapes=[pltpu.VMEM((HG_HEADS, HG_D, HG_D), F32),
                        pltpu.VMEM((rows, d), F32)],
        compiler_params=_cparams(2),
        name="hgrn_prompt_layer",
    )(h, w_in, w_out, lb_logits, norm_w.reshape(1, HG_K), mk_bf, mv_bf, ln_w.reshape(1, d), ln_b.reshape(1, d))


REQS_PER_STEP = 4


def _heads_to_sublanes(row, col0, n, width):
    return jnp.concatenate([row[:, col0 + i * width:col0 + (i + 1) * width] for i in range(n)], axis=0)


def _sample_mem_attend(urow, col0, cmk_ref, cmv_ref, r):
    pieces = []
    for hm in range(MEM_HEADS):
        lo = col0 + hm * MEM_HD
        rows = pl.ds(hm, N_MEM, stride=MEM_HEADS)
        s = jnp.sum(cmk_ref[r, rows, :] * urow[:, lo:lo + MEM_HD], axis=-1, keepdims=True) * (MEM_HD ** -0.5)
        s = jnp.broadcast_to(s, (N_MEM, MEM_HD))
        p = jnp.exp(s - jnp.max(s, axis=0, keepdims=True))
        pieces.append(jnp.sum(p * cmv_ref[r, rows, :], axis=0, keepdims=True) / jnp.sum(p, axis=0, keepdims=True))
    return jnp.concatenate(pieces, axis=1)


def _sample_swa_kernel(relb_ref, sink_ref, u_ref, ck_ref, cv_ref, cmk_ref, cmv_ref,
                       br_ref, ko_ref, vo_ref, bias_scr, bias0_scr, sink_scr):
    nbuf = WINDOW

    @pl.when(pl.program_id(0) == 0)
    def _build_tables():
        dist = nbuf - lax.broadcasted_iota(jnp.int32, (V7X_SUBLANES, nbuf), 1)
        rid = lax.broadcasted_iota(jnp.int32, (V7X_SUBLANES, nbuf), 0)
        for g in range(SWA_KVH):
            acc = jnp.where(rid < SWA_GROUP, MASKED, 0.0).astype(F32)
            acc0 = jnp.zeros((V7X_SUBLANES, nbuf), F32)
            sk = jnp.zeros((V7X_SUBLANES, nbuf), F32)
            for j in range(SWA_GROUP):
                hq = g * SWA_GROUP + j
                sk = jnp.where(rid == j, sink_ref[hq], sk)
                acc0 = jnp.where(rid == j, relb_ref[0, hq], acc0)
                for bk, rng in enumerate(_BUCKET_RANGES):
                    if rng is not None:
                        hit = jnp.logical_and(rid == j, jnp.logical_and(dist >= rng[0], dist <= rng[1]))
                        acc = jnp.where(hit, relb_ref[bk, hq], acc)
            bias_scr[g] = acc
            bias0_scr[g] = acc0
            sink_scr[g] = sk

    g0 = SWA_Q + 2 * SWA_KV + MEM_Q
    pad = jnp.zeros((V7X_SUBLANES - SWA_GROUP, SWA_HD), F32)
    pairs = [(r, g) for r in range(REQS_PER_STEP) for g in range(SWA_KVH)]
    urows = [u_ref[r] for r in range(REQS_PER_STEP)]

    diag = (lax.broadcasted_iota(jnp.int32, (SWA_HD, SWA_HD), 0)
            == lax.broadcasted_iota(jnp.int32, (SWA_HD, SWA_HD), 1))
    newest = lax.broadcasted_iota(jnp.int32, (SWA_HD, nbuf), 1) == nbuf - 1

    def rolled(cache_t, new_row):
        new_col = jnp.sum(jnp.where(diag, new_row, 0.0), axis=-1, keepdims=True)
        return jnp.where(newest, new_col, pltpu.roll(cache_t, nbuf - 1, 1))

    scores = []
    for r, g in pairs:
        k_t = ck_ref[r, g]
        k_new = urows[r][:, SWA_Q + g * SWA_HD:SWA_Q + (g + 1) * SWA_HD]
        v_new = urows[r][:, SWA_Q + SWA_KV + g * SWA_HD:SWA_Q + SWA_KV + (g + 1) * SWA_HD]
        qg = jnp.concatenate(
            [urows[r][:, (g * SWA_GROUP + j) * SWA_HD:(g * SWA_GROUP + j + 1) * SWA_HD] for j in range(SWA_GROUP)]
            + [pad], axis=0) * (SWA_HD ** -0.5)
        s = _dot(qg, k_t) + bias_scr[g]
        s_new = jnp.sum(qg * k_new, axis=-1, keepdims=True) + bias0_scr[g][:, 0:1]
        scores.append((s, s_new, k_new, v_new))

    weights = []
    for (r, g), (s, s_new, _, v_new) in zip(pairs, scores):
        sink = sink_scr[g][:, 0:1]
        m = jnp.maximum(jnp.max(s, axis=-1, keepdims=True), jnp.maximum(s_new, sink))
        p = jnp.exp(s - m)
        p_new = jnp.exp(s_new - m)
        l = jnp.sum(p, axis=-1, keepdims=True) + p_new + jnp.exp(sink - m)
        weights.append((p, p_new * v_new, l))

    pieces = [[] for _ in range(REQS_PER_STEP)]
    for (r, g), (p, o_new, l) in zip(pairs, weights):
        o = (_dot_nt(p, cv_ref[r, g]) + o_new) / l
        pieces[r].extend(o[j:j + 1, :] for j in range(SWA_GROUP))

    for (r, g), (_, _, k_new, v_new) in zip(pairs, scores):
        ko_ref[r, g] = rolled(ck_ref[r, g], k_new)
        vo_ref[r, g] = rolled(cv_ref[r, g], v_new)

    for r in range(REQS_PER_STEP):
        mem_o = _sample_mem_attend(urows[r], SWA_Q + 2 * SWA_KV, cmk_ref, cmv_ref, r)
        br_ref[r] = jnp.concatenate(pieces[r] + [mem_o], axis=1) * _silu(urows[r][:, g0:g0 + D_MODEL])


def _cache_spec(layer, shape):
    return pl.BlockSpec((None, REQS_PER_STEP) + shape, lambda i: (layer, i) + (0,) * len(shape))


def _row_spec(width):
    return pl.BlockSpec((REQS_PER_STEP, 1, width), lambda i: (i, 0, 0))


def _head_minor_rows(cache):
    nl, ns, rows, heads, hd = cache.shape
    return cache.reshape(nl, ns, rows * heads, hd)


def _rows_minor(cache):
    return jnp.moveaxis(cache, -3, -1)


def _sample_swa_attend(u, cache_k, cache_v, j, cache_mk, cache_mv, layer, rel_bias, sinks):
    ns = u.shape[0]
    nbuf = cache_k.shape[-1]
    swa_shape = (SWA_KVH, SWA_HD, nbuf)
    mem_shape = (N_MEM * MEM_HEADS, MEM_HD)
    new_cache = pl.BlockSpec((REQS_PER_STEP,) + swa_shape, lambda i: (i, 0, 0, 0))
    br, ko, vo = pl.pallas_call(
        _sample_swa_kernel,
        grid=(ns // REQS_PER_STEP,),
        in_specs=[_SMEM, _SMEM, _row_spec(SWA_WIDTH), _cache_spec(j, swa_shape), _cache_spec(j, swa_shape),
                  _cache_spec(layer, mem_shape), _cache_spec(layer, mem_shape)],
        out_specs=[_row_spec(D_MODEL), new_cache, new_cache],
        out_shape=[jax.ShapeDtypeStruct((ns, 1, D_MODEL), F32),
                   jax.ShapeDtypeStruct((ns,) + swa_shape, F32),
                   jax.ShapeDtypeStruct((ns,) + swa_shape, F32)],
        scratch_shapes=[pltpu.VMEM((SWA_KVH, V7X_SUBLANES, nbuf), F32)] * 3,
        compiler_params=_cparams(1),
        name="sample_swa_attend",
    )(rel_bias, sinks, u.reshape(ns, 1, SWA_WIDTH), cache_k, cache_v, cache_mk, cache_mv)
    return br.reshape(ns, D_MODEL), ko, vo


def _sample_hgrn_gate_kernel(u_ref, mix_ref, cmk_ref, cmv_ref, br_ref):
    g0 = 3 * HG_K + MEM_Q
    for r in range(REQS_PER_STEP):
        urow = u_ref[r]
        mem_o = _sample_mem_attend(urow, 3 * HG_K, cmk_ref, cmv_ref, r)
        br_ref[r] = jnp.concatenate([mix_ref[r], mem_o], axis=1) * _silu(urow[:, g0:g0 + D_MODEL])


def _sample_hgrn_gate(u, mix, cache_mk, cache_mv, layer):
    ns = u.shape[0]
    mem_shape = (N_MEM * MEM_HEADS, MEM_HD)
    br = pl.pallas_call(
        _sample_hgrn_gate_kernel,
        grid=(ns // REQS_PER_STEP,),
        in_specs=[_row_spec(HG_WIDTH), _row_spec(HG_K), _cache_spec(layer, mem_shape), _cache_spec(layer, mem_shape)],
        out_specs=_row_spec(D_MODEL),
        out_shape=jax.ShapeDtypeStruct((ns, 1, D_MODEL), F32),
        compiler_params=_cparams(1),
        name="sample_hgrn_gate",
    )(u.reshape(ns, 1, HG_WIDTH), mix.reshape(ns, 1, HG_K), cache_mk, cache_mv)
    return br.reshape(ns, D_MODEL)


def _hgrn_sample_kernel(q_ref, f_ref, iv_ref, lbl_ref, nw_ref, st_ref, mix_ref, so_ref, o_scr, *, layer, ns):
    lb = _lower_bound(lbl_ref[...], layer)
    fg = lb + (1.0 - lb) * jax.nn.sigmoid(f_ref[...])
    kk_t = (1.0 - fg).T
    fg_t = fg.T
    qs_t = _silu(q_ref[...]).T
    for r in range(ns):
        state = fg_t[:, r:r + 1] * st_ref[r] + kk_t[:, r:r + 1] * iv_ref[r:r + 1, :]
        so_ref[r] = state
        o_scr[r:r + 1, :] = jnp.sum(qs_t[:, r:r + 1] * state, axis=0, keepdims=True)
    o = o_scr[...]
    ms = jnp.mean(o * o, axis=-1, keepdims=True)
    mix_ref[...] = o * lax.rsqrt(ms + RMS_EPS) * nw_ref[...]


def _hgrn_sample_step(u, state, j, lb_logits, norm_w, layer):
    ns = u.shape[0]
    col = lambda part: pl.BlockSpec((ns, HG_D), lambda h: (0, part * HG_HEADS + h))
    kernel = functools.partial(_hgrn_sample_kernel, layer=layer, ns=ns)
    return pl.pallas_call(
        kernel,
        grid=(HG_HEADS,),
        in_specs=[col(0), col(1), col(2),
                  pl.BlockSpec((DEPTH, HG_D), lambda h: (0, h)),
                  pl.BlockSpec((1, HG_D), lambda h: (0, h)),
                  pl.BlockSpec((None, ns, None, HG_D, HG_D), lambda h: (j, 0, h, 0, 0))],
        out_specs=[pl.BlockSpec((ns, HG_D), lambda h: (0, h)),
                   pl.BlockSpec((ns, None, HG_D, HG_D), lambda h: (0, h, 0, 0))],
        out_shape=[jax.ShapeDtypeStruct((ns, HG_K), F32),
                   jax.ShapeDtypeStruct(state.shape[1:], F32)],
        scratch_shapes=[pltpu.VMEM((ns, HG_D), F32)],
        compiler_params=_cparams(1),
        name="hgrn_sample_step",
    )(u, u, u, lb_logits, norm_w.reshape(1, HG_K), state)


def _outproj_ln_kernel(br_ref, h_ref, wout_ref, lnw_ref, lnb_ref, out_ref):
    y = jnp.dot(br_ref[...].astype(BF16), wout_ref[:, :D_MODEL], preferred_element_type=F32)
    out_ref[...] = _layer_norm(DEEPNORM_ALPHA * h_ref[...] + y, lnw_ref[...], lnb_ref[...])


def _outproj_ln(branch, h, w_out, ln_w, ln_b):
    ns, d = h.shape
    full = lambda shape: pl.BlockSpec(shape, lambda i: (0, 0))
    return pl.pallas_call(
        _outproj_ln_kernel,
        grid=(1,),
        in_specs=[full((ns, d)), full((ns, d)), full(w_out.shape), full((1, d)), full((1, d))],
        out_specs=full((ns, d)),
        out_shape=jax.ShapeDtypeStruct((ns, d), F32),
        compiler_params=_cparams(1),
        name="outproj_ln",
    )(branch, h, w_out, ln_w.reshape(1, d), ln_b.reshape(1, d))


def kernel(x_prompt, x_sample, cache_mem_k, cache_mem_v, cache_swa_k, cache_swa_v, state_hgrn, mem_prompt, rel_bias, swa_w_in, swa_sinks, hg_w_in, hg_lb_logits, hg_norm_w, w_mem_k, w_mem_v, w_out, ln_w, ln_b):
    bp, t, d = x_prompt.shape
    ns = x_sample.shape[0]
    assert d == D_MODEL and x_sample.shape[1] == 1 and t % PROMPT_ROWS == 0
    assert cache_swa_k.shape[2] == WINDOW <= PAST_LEN
    assert w_mem_k.shape[0] == DEPTH

    wo = [_pad_cols(w_out[i].astype(BF16)) if i % 2 else w_out[i].astype(BF16) for i in range(DEPTH)]
    mem_k_prompt, mem_v_prompt, mk_bf, mv_bf = _mem_kv_proj(mem_prompt, w_mem_k, w_mem_v)
    cmk, cmv = _head_minor_rows(cache_mem_k), _head_minor_rows(cache_mem_v)
    csk, csv = _rows_minor(cache_swa_k), _rows_minor(cache_swa_v)

    hp = x_prompt
    hs = x_sample.reshape(ns, d)
    swa_kp, swa_vp, swa_ks, swa_vs, hg_sp, hg_ss = [], [], [], [], [], []
    for i in range(DEPTH):
        j = i // 2
        if i % 2 == 0:
            w_in = _pad_cols(swa_w_in[j].astype(BF16))
            hp, kw, vw = _swa_prompt_layer(hp, w_in, wo[i], mk_bf, mv_bf, i, rel_bias, swa_sinks[j],
                                           ln_w[i], ln_b[i], rows=PROMPT_ROWS)
            swa_kp.append(kw.reshape(bp, WINDOW, SWA_KVH, SWA_HD))
            swa_vp.append(vw.reshape(bp, WINDOW, SWA_KVH, SWA_HD))
            u = _proj(hs, w_in, SWA_WIDTH, "sample_swa_proj")
            br, ko, vo = _sample_swa_attend(u, csk, csv, j, cmk, cmv, i, rel_bias, swa_sinks[j])
            swa_ks.append(jnp.moveaxis(ko, -1, -3))
            swa_vs.append(jnp.moveaxis(vo, -1, -3))
        else:
            w_in = _pad_cols(hg_w_in[j].astype(BF16))
            hp, st = _hgrn_prompt_layer(hp, w_in, wo[i], mk_bf, mv_bf, i, hg_lb_logits, hg_norm_w[j],
                                        ln_w[i], ln_b[i])
            hg_sp.append(st)
            u = _proj(hs, w_in, HG_WIDTH, "sample_hgrn_proj")
            mix, so = _hgrn_sample_step(u, state_hgrn, j, hg_lb_logits, hg_norm_w[j], i)
            hg_ss.append(so)
            br = _sample_hgrn_gate(u, mix, cmk, cmv, i)
        hs = _outproj_ln(br, hs, wo[i], ln_w[i], ln_b[i])

    return (hp, hs.reshape(ns, 1, d), mem_k_prompt, mem_v_prompt,
            jnp.stack(swa_kp), jnp.stack(swa_vp), jnp.stack(hg_sp),
            jnp.stack(swa_ks), jnp.stack(swa_vs), jnp.stack(hg_ss))
```

```python
import functools
import math

import numpy as np
import jax
import jax.numpy as jnp
from jax import lax
from jax.experimental import pallas as pl
from jax.experimental.pallas import tpu as pltpu

F32 = jnp.float32
BF16 = jnp.bfloat16

D_MODEL = 2048
DEPTH = 2
PAST_LEN = 8192
N_MEM = 256
MEM_HEADS = 4
MEM_HD = 128
MEM_Q = MEM_HEADS * MEM_HD
WINDOW = 128
SWA_HD = 64
SWA_KVH = 4
SWA_GROUP = 6
SWA_QH = SWA_KVH * SWA_GROUP
SWA_Q = SWA_QH * SWA_HD
SWA_KV = SWA_KVH * SWA_HD
N_BUCKETS = 32
HG_HEADS = 12
HG_D = 128
HG_K = HG_HEADS * HG_D
SWA_WIDTH = SWA_Q + 2 * SWA_KV + MEM_Q + D_MODEL
HG_WIDTH = 3 * HG_K + MEM_Q + D_MODEL
DEEPNORM_ALPHA = (2.0 * DEPTH) ** 0.25
LN_EPS = 1e-5
RMS_EPS = 1e-6
MASKED = -1e30

V7X_SUBLANES = 8
V7X_LANES = 128
V7X_MXU_COLS = 256
V7X_VMEM_LIMIT = 60000 * 1024
PROJ_COLS = 512
GATE_CHUNK = 512
PROMPT_ROWS = 256


def _t5_bucket_ranges():
    n = np.arange(WINDOW)
    max_exact = N_BUCKETS // 2
    nf = np.maximum(n, 1).astype(np.float64)
    large = max_exact + (np.log(nf / max_exact) / math.log(WINDOW / max_exact)
                         * (N_BUCKETS - max_exact)).astype(np.int64)
    bucket = np.where(n < max_exact, n, np.minimum(large, N_BUCKETS - 1))
    ranges = []
    for b in range(N_BUCKETS):
        idx = np.nonzero(bucket == b)[0]
        ranges.append((int(idx[0]), int(idx[-1])) if idx.size else None)
    return ranges


_BUCKET_RANGES = _t5_bucket_ranges()


def _dot(a, b):
    return jnp.dot(a.astype(BF16), b.astype(BF16), preferred_element_type=F32)


def _dot_nt(a, b):
    return lax.dot_general(a.astype(BF16), b.astype(BF16), (((1,), (1,)), ((), ())),
                           preferred_element_type=F32)


def _layer_norm(z, w, b):
    mu = jnp.mean(z, axis=-1, keepdims=True)
    zc = z - mu
    var = jnp.mean(zc * zc, axis=-1, keepdims=True)
    return zc * lax.rsqrt(var + LN_EPS) * w + b


def _silu(x):
    return x * jax.nn.sigmoid(x)


def _cparams(n_axes):
    return pltpu.CompilerParams(dimension_semantics=("arbitrary",) * n_axes,
                                vmem_limit_bytes=V7X_VMEM_LIMIT)


def _resident(shape):
    nd = len(shape)
    return pl.BlockSpec(shape, lambda *_: (0,) * nd, pipeline_mode=pl.Buffered(1))


_SMEM = pl.BlockSpec(memory_space=pltpu.SMEM)


def _mem_kv_kernel(x_ref, wk_ref, wv_ref, ko_ref, vo_ref, kb_ref, vb_ref):
    xb = x_ref[...].astype(BF16)
    for w_ref, o_ref, b_ref in ((wk_ref, ko_ref, kb_ref), (wv_ref, vo_ref, vb_ref)):
        y = jnp.dot(xb, w_ref[...].astype(BF16), preferred_element_type=F32)
        b_ref[...] = y.astype(BF16)
        for hm in range(MEM_HEADS):
            o_ref[:, hm, :] = y[:, hm * MEM_HD:(hm + 1) * MEM_HD]


def _mem_kv_proj(mem, w_k, w_v):
    bsz, n, d = mem.shape
    nl = w_k.shape[0]
    w_spec = pl.BlockSpec((None, d, MEM_Q), lambda l, b: (l, 0, 0))
    o_spec = pl.BlockSpec((None, None, n, MEM_HEADS, MEM_HD), lambda l, b: (l, b, 0, 0, 0))
    b_spec = pl.BlockSpec((None, n, MEM_Q), lambda l, b: (l, b, 0))
    o_shape = jax.ShapeDtypeStruct((nl, bsz, n, MEM_HEADS, MEM_HD), F32)
    b_shape = jax.ShapeDtypeStruct((nl, bsz * n, MEM_Q), BF16)
    return pl.pallas_call(
        _mem_kv_kernel,
        grid=(nl, bsz),
        in_specs=[pl.BlockSpec((None, n, d), lambda l, b: (b, 0, 0)), w_spec, w_spec],
        out_specs=[o_spec, o_spec, b_spec, b_spec],
        out_shape=[o_shape, o_shape, b_shape, b_shape],
        compiler_params=_cparams(2),
        name="mem_kv_proj",
    )(mem, w_k, w_v)


def _proj_kernel(x_ref, w_ref, o_ref):
    o_ref[...] = jnp.dot(x_ref[...].astype(BF16), w_ref[...], preferred_element_type=F32)


def _pad_cols(w):
    if w.shape[-1] % (4 * V7X_MXU_COLS) != 0:
        return w
    return jnp.concatenate([w, jnp.zeros(w.shape[:-1] + (V7X_MXU_COLS,), w.dtype)], axis=-1)


def _proj(x, w, n, name):
    m, k = x.shape
    return pl.pallas_call(
        _proj_kernel,
        grid=(n // PROJ_COLS,),
        in_specs=[pl.BlockSpec((m, k), lambda j: (0, 0)),
                  pl.BlockSpec((k, PROJ_COLS), lambda j: (0, j))],
        out_specs=pl.BlockSpec((m, PROJ_COLS), lambda j: (0, j)),
        out_shape=jax.ShapeDtypeStruct((m, n), F32),
        compiler_params=_cparams(1),
        name=name,
    )(x, w)


def _mem_scores(mq_all, mk_ref):
    return [_dot_nt(mq_all[:, hm * MEM_HD:(hm + 1) * MEM_HD], mk_ref[:, hm * MEM_HD:(hm + 1) * MEM_HD])
            * (MEM_HD ** -0.5) for hm in range(MEM_HEADS)]


def _put_gated(bb_ref, gs_ref, col0, o):
    cols = slice(col0, col0 + o.shape[-1])
    bb_ref[:, cols] = (o * gs_ref[:, cols]).astype(BF16)


def _mem_values(weights, mv_ref, bb_ref, gs_ref, out_col0):
    for hm, (p, l) in enumerate(weights):
        lo = hm * MEM_HD
        _put_gated(bb_ref, gs_ref, out_col0 + lo, _dot(p, mv_ref[:, lo:lo + MEM_HD]) / l)


def _swa_prompt_kernel(relb_ref, sink_ref, h_ref, win_ref, wout_ref, mk_ref, mv_ref, lnw_ref, lnb_ref,
                       out_ref, kout_ref, vout_ref,
                       bias_scr, k_scr, v_scr, gs_scr, bb_scr, *, rows):
    b = pl.program_id(0)
    t = pl.program_id(1)
    w2 = 2 * WINDOW

    @pl.when(jnp.logical_and(b == 0, t == 0))
    def _build_bias():
        qi = lax.broadcasted_iota(jnp.int32, (WINDOW, w2), 0)
        kj = lax.broadcasted_iota(jnp.int32, (WINDOW, w2), 1)
        dist = qi + WINDOW - kj

        def per_head(hq, carry):
            acc = jnp.full((WINDOW, w2), MASKED, F32)
            for bk, rng in enumerate(_BUCKET_RANGES):
                if rng is not None:
                    hit = jnp.logical_and(dist >= rng[0], dist <= rng[1])
                    acc = jnp.where(hit, relb_ref[bk, hq], acc)
            bias_scr[0, hq] = acc
            bias_scr[1, hq] = jnp.where(kj < WINDOW, MASKED, acc)
            return carry

        lax.fori_loop(0, SWA_QH, per_head, 0)

    @pl.when(t == 0)
    def _reset_window():
        k_scr[0:WINDOW, :] = jnp.zeros((WINDOW, SWA_KV), BF16)
        v_scr[0:WINDOW, :] = jnp.zeros((WINDOW, SWA_KV), BF16)

    qkv_w = SWA_Q + 2 * SWA_KV
    n_blocks = rows // WINDOW
    n_gate = D_MODEL // GATE_CHUNK
    heads = [(g, j) for g in range(SWA_KVH) for j in range(SWA_GROUP)]
    xb, qkv, mq, scores, mem_s = {}, {}, {}, {}, {}

    def project(qb):
        r0 = qb * WINDOW
        xb[qb] = h_ref[0, r0:r0 + WINDOW, :].astype(BF16)
        qkv[qb] = jnp.dot(xb[qb], win_ref[:, :qkv_w], preferred_element_type=F32)
        mq[qb] = jnp.dot(xb[qb], win_ref[:, qkv_w:qkv_w + MEM_Q], preferred_element_type=F32)
        k = qkv[qb][:, SWA_Q:SWA_Q + SWA_KV]
        v = qkv[qb][:, SWA_Q + SWA_KV:]
        if qb == n_blocks - 1:
            kout_ref[0] = k
            vout_ref[0] = v
        k_scr[WINDOW + r0:w2 + r0, :] = k.astype(BF16)
        v_scr[WINDOW + r0:w2 + r0, :] = v.astype(BF16)

    def score(qb):
        r0 = qb * WINDOW
        table = jnp.where(t == 0, 1, 0) if qb == 0 else 0
        scores[qb] = []
        for idx, (g, j) in enumerate(heads):
            hq = g * SWA_GROUP + j
            q = qkv[qb][:, hq * SWA_HD:(hq + 1) * SWA_HD] * (SWA_HD ** -0.5)
            scores[qb].append(_dot_nt(q, k_scr[r0:r0 + w2, g * SWA_HD:(g + 1) * SWA_HD]) + bias_scr[table, hq])
            for chunk in range(idx * n_gate // len(heads), (idx + 1) * n_gate // len(heads)):
                lo = chunk * GATE_CHUNK
                gate = jnp.dot(xb[qb], win_ref[:, qkv_w + MEM_Q + lo:qkv_w + MEM_Q + lo + GATE_CHUNK],
                               preferred_element_type=F32)
                gs_scr[r0:r0 + WINDOW, lo:lo + GATE_CHUNK] = _silu(gate)
        mem_s[qb] = _mem_scores(mq[qb], mk_ref)

    def attend(qb):
        r0 = qb * WINDOW
        bb = bb_scr.at[r0:r0 + WINDOW, :]
        gs = gs_scr.at[r0:r0 + WINDOW, :]
        sinks = [sink_ref[g * SWA_GROUP + j] for g, j in heads]
        maxes = [jnp.maximum(jnp.max(s, axis=-1, keepdims=True), sk) for s, sk in zip(scores[qb], sinks)]
        mem_max = [jnp.max(s, axis=-1, keepdims=True) for s in mem_s[qb]]
        probs = [jnp.exp(s - m) for s, m in zip(scores[qb], maxes)]
        mem_p = [jnp.exp(s - m) for s, m in zip(mem_s[qb], mem_max)]
        weights = [(p, jnp.sum(p, axis=-1, keepdims=True) + jnp.exp(sk - m))
                   for p, m, sk in zip(probs, maxes, sinks)]
        mem_w = [(p, jnp.sum(p, axis=-1, keepdims=True)) for p in mem_p]
        for (g, j), (p, l) in zip(heads, weights):
            o = _dot(p, v_scr[r0:r0 + w2, g * SWA_HD:(g + 1) * SWA_HD]) / l
            _put_gated(bb, gs, (g * SWA_GROUP + j) * SWA_HD, o)
        _mem_values(mem_w, mv_ref, bb, gs, SWA_Q)

    def finish(qb):
        r0 = qb * WINDOW
        y = jnp.dot(bb_scr[r0:r0 + WINDOW, :], wout_ref[:, :D_MODEL], preferred_element_type=F32)
        out_ref[0, r0:r0 + WINDOW, :] = _layer_norm(DEEPNORM_ALPHA * h_ref[0, r0:r0 + WINDOW, :] + y,
                                                    lnw_ref[...], lnb_ref[...])

    project(0)
    score(0)
    for qb in range(n_blocks):
        if qb + 1 < n_blocks:
            project(qb + 1)
            score(qb + 1)
        attend(qb)
        finish(qb)

    k_scr[0:WINDOW, :] = k_scr[rows:rows + WINDOW, :]
    v_scr[0:WINDOW, :] = v_scr[rows:rows + WINDOW, :]


def _swa_prompt_layer(h, w_in, w_out, mk_bf, mv_bf, layer, rel_bias, sinks, ln_w, ln_b, rows):
    bsz, t, d = h.shape
    kernel = functools.partial(_swa_prompt_kernel, rows=rows)
    mem_spec = pl.BlockSpec((None, N_MEM, MEM_Q), lambda b, i: (layer, b, 0))
    return pl.pallas_call(
        kernel,
        grid=(bsz, t // rows),
        in_specs=[_SMEM, _SMEM,
                  pl.BlockSpec((1, rows, d), lambda b, i: (b, i, 0)),
                  _resident(w_in.shape),
                  _resident(w_out.shape),
                  mem_spec, mem_spec,
                  _resident((1, d)), _resident((1, d))],
        out_specs=[pl.BlockSpec((1, rows, d), lambda b, i: (b, i, 0)),
                   pl.BlockSpec((1, WINDOW, SWA_KV), lambda b, i: (b, 0, 0)),
                   pl.BlockSpec((1, WINDOW, SWA_KV), lambda b, i: (b, 0, 0))],
        out_shape=[jax.ShapeDtypeStruct((bsz, t, d), F32),
                   jax.ShapeDtypeStruct((bsz, WINDOW, SWA_KV), F32),
                   jax.ShapeDtypeStruct((bsz, WINDOW, SWA_KV), F32)],
        scratch_shapes=[pltpu.VMEM((2, SWA_QH, WINDOW, 2 * WINDOW), F32),
                        pltpu.VMEM((WINDOW + rows, SWA_KV), BF16),
                        pltpu.VMEM((WINDOW + rows, SWA_KV), BF16),
                        pltpu.VMEM((rows, d), F32),
                        pltpu.VMEM((rows, d), BF16)],
        compiler_params=_cparams(2),
        name="swa_prompt_layer",
    )(rel_bias, sinks, h, w_in, w_out, mk_bf, mv_bf, ln_w.reshape(1, d), ln_b.reshape(1, d))


HG_CHUNK = 128
HG_DIRECT = 8


def _lower_bound(logits, layer):
    e = jnp.exp(logits - jnp.max(logits, axis=0, keepdims=True))
    sm = e / jnp.sum(e, axis=0, keepdims=True)
    return jnp.sum(sm[1:layer + 1, :], axis=0, keepdims=True)


def _cumsum_rows(tril_bf, g):
    w = g.shape[1]
    g1 = g.astype(BF16)
    r1 = g - g1.astype(F32)
    g2 = r1.astype(BF16)
    g3 = (r1 - g2.astype(F32)).astype(BF16)
    d = jnp.dot(tril_bf, jnp.concatenate([g1, g2, g3], axis=1), preferred_element_type=F32)
    return (d[:, :w] + d[:, w:2 * w]) + d[:, 2 * w:]


def _hgrn_chunk_operands(qs, kk, gc):
    c = HG_CHUNK
    nb = c // HG_DIRECT
    q3 = qs.reshape(nb, HG_DIRECT, HG_D)
    g3 = gc.reshape(nb, HG_DIRECT, HG_D)
    k3 = kk.reshape(nb, HG_DIRECT, HG_D)
    lane = lax.broadcasted_iota(jnp.int32, (nb, HG_DIRECT, c), 2)
    blk = lax.broadcasted_iota(jnp.int32, (nb, HG_DIRECT, c), 0)
    trow = lax.broadcasted_iota(jnp.int32, (nb, HG_DIRECT, c), 1)
    a3 = jnp.zeros((nb, HG_DIRECT, c), F32)
    for sl in range(HG_DIRECT):
        e = jnp.exp2(g3 - g3[:, sl:sl + 1, :])
        col = jnp.sum(e * q3 * k3[:, sl:sl + 1, :], axis=-1, keepdims=True)
        hit = jnp.logical_and(lane == blk * HG_DIRECT + sl, trow >= sl)
        a3 = jnp.where(hit, col, a3)
    ri = lax.broadcasted_iota(jnp.int32, (c, 1), 0)
    levels = []
    half = HG_DIRECT
    while half < c:
        span = 2 * half
        gsp = gc.reshape(c // span, span, HG_D)
        gref = jnp.broadcast_to(gsp[:, half - 1:half, :], gsp.shape).reshape(c, HG_D)
        upper = (ri % span) >= half
        eq = jnp.where(upper, jnp.exp2(gc - gref), 0.0)
        ek = jnp.where(upper, 0.0, jnp.exp2(gref - gc))
        levels.append(((qs * eq).astype(BF16), (kk * ek).astype(BF16)))
        half = span
    return a3.reshape(c, c), levels


def _hgrn_chunk_products(direct, levels):
    c = HG_CHUNK
    ti = lax.broadcasted_iota(jnp.int32, (c, c), 0)
    si = lax.broadcasted_iota(jnp.int32, (c, c), 1)
    a = direct
    for lvl, (ql, kl) in enumerate(levels):
        span = 2 * HG_DIRECT << lvl
        a = a + jnp.where((ti // span) == (si // span), _dot_nt(ql, kl), 0.0)
    return a


def _hgrn_prompt_kernel(h_ref, win_ref, wout_ref, lbl_ref, nw_ref, mk_ref, mv_ref, lnw_ref, lnb_ref,
                        out_ref, st_ref,
                        st_scr, br_scr, *, layer):
    t = pl.program_id(1)
    c = HG_CHUNK
    heads = range(HG_HEADS)

    @pl.when(t == 0)
    def _reset_state():
        st_scr[...] = jnp.zeros(st_scr.shape, F32)

    lb_all = _lower_bound(lbl_ref[...], layer)
    ri = lax.broadcasted_iota(jnp.int32, (c, c), 0)
    ci = lax.broadcasted_iota(jnp.int32, (c, c), 1)
    tril_bf = jnp.where(ri >= ci, 1.0, 0.0).astype(BF16)

    x = h_ref[0]
    xb = x.astype(BF16)
    pair_w = 2 * HG_D
    n_pairs = HG_HEADS // 2
    qs, kk, iv, gc, operands = [], [], [], [], []

    def project(pair):
        return [jnp.dot(xb, win_ref[:, part * HG_K + pair * pair_w:part * HG_K + (pair + 1) * pair_w],
                        preferred_element_type=F32) for part in range(3)]

    def gates(pair, uq, uf, ui):
        lb = lb_all[:, pair * pair_w:(pair + 1) * pair_w]
        fg = lb + (1.0 - lb) * jax.nn.sigmoid(uf)
        gc2 = _cumsum_rows(tril_bf, jnp.log2(fg))
        for cols in (slice(0, HG_D), slice(HG_D, pair_w)):
            qs.append(_silu(uq[:, cols]))
            iv.append(ui[:, cols])
            kk.append(1.0 - fg[:, cols])
            gc.append(gc2[:, cols])
            operands.append(_hgrn_chunk_operands(qs[-1], kk[-1], gc[-1]))

    a = {}

    def products(pair):
        for hh in (2 * pair, 2 * pair + 1):
            a[hh] = _hgrn_chunk_products(*operands[hh])

    parts = [project(0)]
    for pair in range(1, n_pairs):
        parts.append(project(pair))
        gates(pair - 1, *parts[pair - 1])
        if pair >= 2:
            products(pair - 2)
    ug = jnp.dot(xb, win_ref[:, 3 * HG_K:HG_WIDTH], preferred_element_type=F32)
    gates(n_pairs - 1, *parts[n_pairs - 1])
    products(n_pairs - 2)
    mem_s = _mem_scores(ug[:, :MEM_Q], mk_ref)
    mem_p = [jnp.exp(s - jnp.max(s, axis=-1, keepdims=True)) for s in mem_s]
    for hm, p in enumerate(mem_p):
        lo = hm * MEM_HD
        br_scr[:, HG_K + lo:HG_K + lo + MEM_HD] = (_dot(p, mv_ref[:, lo:lo + MEM_HD])
                                                   / jnp.sum(p, axis=-1, keepdims=True))

    products(n_pairs - 1)

    iv_t = [iv[hh].T for hh in heads]
    for hh in heads:
        lhs = jnp.concatenate([a[hh], qs[hh] * jnp.exp2(gc[hh])], axis=1)
        rhs = jnp.concatenate([iv_t[hh], st_scr[hh]], axis=1)
        o = _dot_nt(lhs, rhs)
        ms = jnp.mean(o * o, axis=-1, keepdims=True)
        br_scr[:, hh * HG_D:(hh + 1) * HG_D] = o * lax.rsqrt(ms + RMS_EPS) * nw_ref[:, hh * HG_D:(hh + 1) * HG_D]

    for hh in heads:
        g_last = gc[hh][c - 1:c, :]
        kdec = kk[hh] * jnp.exp2(g_last - gc[hh])
        st_scr[hh] = st_scr[hh] * jnp.exp2(g_last) + _dot(iv_t[hh], kdec)

    @pl.when(t == pl.num_programs(1) - 1)
    def _emit_state():
        for hh in heads:
            st_ref[0, hh] = st_scr[hh].T

    branch = br_scr[...] * _silu(ug[:, MEM_Q:])
    y = jnp.dot(branch.astype(BF16), wout_ref[:, :D_MODEL], preferred_element_type=F32)
    out_ref[0] = _layer_norm(DEEPNORM_ALPHA * x + y, lnw_ref[...], lnb_ref[...])


def _hgrn_prompt_layer(h, w_in, w_out, mk_bf, mv_bf, layer, lb_logits, norm_w, ln_w, ln_b):
    bsz, t, d = h.shape
    rows = HG_CHUNK
    kernel = functools.partial(_hgrn_prompt_kernel, layer=layer)
    mem_spec = pl.BlockSpec((None, N_MEM, MEM_Q), lambda b, i: (layer, b, 0))
    return pl.pallas_call(
        kernel,
        grid=(bsz, t // rows),
        in_specs=[pl.BlockSpec((1, rows, d), lambda b, i: (b, i, 0)),
                  _resident(w_in.shape),
                  _resident(w_out.shape),
                  _resident((DEPTH, HG_K)),
                  _resident((1, HG_K)),
                  mem_spec, mem_spec,
                  _resident((1, d)), _resident((1, d))],
        out_specs=[pl.BlockSpec((1, rows, d), lambda b, i: (b, i, 0)),
                   pl.BlockSpec((1, HG_HEADS, HG_D, HG_D), lambda b, i: (b, 0, 0, 0))],
        out_shape=[jax.ShapeDtypeStruct((bsz, t, d), F32),
                   jax.ShapeDtypeStruct((bsz, HG_HEADS, HG_D, HG_D), F32)],
        scratch_shapes=[pltpu.VMEM((HG_HEADS, HG_D, HG_D), F32),
                        pltpu.VMEM((rows, d), F32)],
        compiler_params=_cparams(2),
        name="hgrn_prompt_layer",
    )(h, w_in, w_out, lb_logits, norm_w.reshape(1, HG_K), mk_bf, mv_bf, ln_w.reshape(1, d), ln_b.reshape(1, d))


REQS_PER_STEP = 4


def _heads_to_sublanes(row, col0, n, width):
    return jnp.concatenate([row[:, col0 + i * width:col0 + (i + 1) * width] for i in range(n)], axis=0)


def _sample_mem_attend(urow, col0, cmk_ref, cmv_ref, r):
    pieces = []
    for hm in range(MEM_HEADS):
        lo = col0 + hm * MEM_HD
        rows = pl.ds(hm, N_MEM, stride=MEM_HEADS)
        s = jnp.sum(cmk_ref[r, rows, :] * urow[:, lo:lo + MEM_HD], axis=-1, keepdims=True) * (MEM_HD ** -0.5)
        s = jnp.broadcast_to(s, (N_MEM, MEM_HD))
        p = jnp.exp(s - jnp.max(s, axis=0, keepdims=True))
        pieces.append(jnp.sum(p * cmv_ref[r, rows, :], axis=0, keepdims=True) / jnp.sum(p, axis=0, keepdims=True))
    return jnp.concatenate(pieces, axis=1)


def _sample_swa_kernel(relb_ref, sink_ref, u_ref, ck_ref, cv_ref, cmk_ref, cmv_ref,
                       br_ref, ko_ref, vo_ref, bias_scr, bias0_scr, sink_scr):
    nbuf = WINDOW

    @pl.when(pl.program_id(0) == 0)
    def _build_tables():
        dist = nbuf - lax.broadcasted_iota(jnp.int32, (V7X_SUBLANES, nbuf), 1)
        rid = lax.broadcasted_iota(jnp.int32, (V7X_SUBLANES, nbuf), 0)
        for g in range(SWA_KVH):
            acc = jnp.where(rid < SWA_GROUP, MASKED, 0.0).astype(F32)
            acc0 = jnp.zeros((V7X_SUBLANES, nbuf), F32)
            sk = jnp.zeros((V7X_SUBLANES, nbuf), F32)
            for j in range(SWA_GROUP):
                hq = g * SWA_GROUP + j
                sk = jnp.where(rid == j, sink_ref[hq], sk)
                acc0 = jnp.where(rid == j, relb_ref[0, hq], acc0)
                for bk, rng in enumerate(_BUCKET_RANGES):
                    if rng is not None:
                        hit = jnp.logical_and(rid == j, jnp.logical_and(dist >= rng[0], dist <= rng[1]))
                        acc = jnp.where(hit, relb_ref[bk, hq], acc)
            bias_scr[g] = acc
            bias0_scr[g] = acc0
            sink_scr[g] = sk

    g0 = SWA_Q + 2 * SWA_KV + MEM_Q
    pad = jnp.zeros((V7X_SUBLANES - SWA_GROUP, SWA_HD), F32)
    pairs = [(r, g) for r in range(REQS_PER_STEP) for g in range(SWA_KVH)]
    urows = [u_ref[r] for r in range(REQS_PER_STEP)]

    diag = (lax.broadcasted_iota(jnp.int32, (SWA_HD, SWA_HD), 0)
            == lax.broadcasted_iota(jnp.int32, (SWA_HD, SWA_HD), 1))
    newest = lax.broadcasted_iota(jnp.int32, (SWA_HD, nbuf), 1) == nbuf - 1

    def rolled(cache_t, new_row):
        new_col = jnp.sum(jnp.where(diag, new_row, 0.0), axis=-1, keepdims=True)
        return jnp.where(newest, new_col, pltpu.roll(cache_t, nbuf - 1, 1))

    scores = []
    for r, g in pairs:
        k_t = ck_ref[r, g]
        k_new = urows[r][:, SWA_Q + g * SWA_HD:SWA_Q + (g + 1) * SWA_HD]
        v_new = urows[r][:, SWA_Q + SWA_KV + g * SWA_HD:SWA_Q + SWA_KV + (g + 1) * SWA_HD]
        qg = jnp.concatenate(
            [urows[r][:, (g * SWA_GROUP + j) * SWA_HD:(g * SWA_GROUP + j + 1) * SWA_HD] for j in range(SWA_GROUP)]
            + [pad], axis=0) * (SWA_HD ** -0.5)
        s = _dot(qg, k_t) + bias_scr[g]
        s_new = jnp.sum(qg * k_new, axis=-1, keepdims=True) + bias0_scr[g][:, 0:1]
        scores.append((s, s_new, k_new, v_new))

    weights = []
    for (r, g), (s, s_new, _, v_new) in zip(pairs, scores):
        sink = sink_scr[g][:, 0:1]
        m = jnp.maximum(jnp.max(s, axis=-1, keepdims=True), jnp.maximum(s_new, sink))
        p = jnp.exp(s - m)
        p_new = jnp.exp(s_new - m)
        l = jnp.sum(p, axis=-1, keepdims=True) + p_new + jnp.exp(sink - m)
        weights.append((p, p_new * v_new, l))

    pieces = [[] for _ in range(REQS_PER_STEP)]
    for (r, g), (p, o_new, l) in zip(pairs, weights):
        o = (_dot_nt(p, cv_ref[r, g]) + o_new) / l
        pieces[r].extend(o[j:j + 1, :] for j in range(SWA_GROUP))

    for (r, g), (_, _, k_new, v_new) in zip(pairs, scores):
        ko_ref[r, g] = rolled(ck_ref[r, g], k_new)
        vo_ref[r, g] = rolled(cv_ref[r, g], v_new)

    for r in range(REQS_PER_STEP):
        mem_o = _sample_mem_attend(urows[r], SWA_Q + 2 * SWA_KV, cmk_ref, cmv_ref, r)
        br_ref[r] = jnp.concatenate(pieces[r] + [mem_o], axis=1) * _silu(urows[r][:, g0:g0 + D_MODEL])


def _cache_spec(layer, shape):
    return pl.BlockSpec((None, REQS_PER_STEP) + shape, lambda i: (layer, i) + (0,) * len(shape))


def _row_spec(width):
    return pl.BlockSpec((REQS_PER_STEP, 1, width), lambda i: (i, 0, 0))


def _head_minor_rows(cache):
    nl, ns, rows, heads, hd = cache.shape
    return cache.reshape(nl, ns, rows * heads, hd)


def _rows_minor(cache):
    return jnp.moveaxis(cache, -3, -1)


def _sample_swa_attend(u, cache_k, cache_v, j, cache_mk, cache_mv, layer, rel_bias, sinks):
    ns = u.shape[0]
    nbuf = cache_k.shape[-1]
    swa_shape = (SWA_KVH, SWA_HD, nbuf)
    mem_shape = (N_MEM * MEM_HEADS, MEM_HD)
    new_cache = pl.BlockSpec((REQS_PER_STEP,) + swa_shape, lambda i: (i, 0, 0, 0))
    br, ko, vo = pl.pallas_call(
        _sample_swa_kernel,
        grid=(ns // REQS_PER_STEP,),
        in_specs=[_SMEM, _SMEM, _row_spec(SWA_WIDTH), _cache_spec(j, swa_shape), _cache_spec(j, swa_shape),
                  _cache_spec(layer, mem_shape), _cache_spec(layer, mem_shape)],
        out_specs=[_row_spec(D_MODEL), new_cache, new_cache],
        out_shape=[jax.ShapeDtypeStruct((ns, 1, D_MODEL), F32),
                   jax.ShapeDtypeStruct((ns,) + swa_shape, F32),
                   jax.ShapeDtypeStruct((ns,) + swa_shape, F32)],
        scratch_shapes=[pltpu.VMEM((SWA_KVH, V7X_SUBLANES, nbuf), F32)] * 3,
        compiler_params=_cparams(1),
        name="sample_swa_attend",
    )(rel_bias, sinks, u.reshape(ns, 1, SWA_WIDTH), cache_k, cache_v, cache_mk, cache_mv)
    return br.reshape(ns, D_MODEL), ko, vo


def _sample_hgrn_gate_kernel(u_ref, mix_ref, cmk_ref, cmv_ref, br_ref):
    g0 = 3 * HG_K + MEM_Q
    for r in range(REQS_PER_STEP):
        urow = u_ref[r]
        mem_o = _sample_mem_attend(urow, 3 * HG_K, cmk_ref, cmv_ref, r)
        br_ref[r] = jnp.concatenate([mix_ref[r], mem_o], axis=1) * _silu(urow[:, g0:g0 + D_MODEL])


def _sample_hgrn_gate(u, mix, cache_mk, cache_mv, layer):
    ns = u.shape[0]
    mem_shape = (N_MEM * MEM_HEADS, MEM_HD)
    br = pl.pallas_call(
        _sample_hgrn_gate_kernel,
        grid=(ns // REQS_PER_STEP,),
        in_specs=[_row_spec(HG_WIDTH), _row_spec(HG_K), _cache_spec(layer, mem_shape), _cache_spec(layer, mem_shape)],
        out_specs=_row_spec(D_MODEL),
        out_shape=jax.ShapeDtypeStruct((ns, 1, D_MODEL), F32),
        compiler_params=_cparams(1),
        name="sample_hgrn_gate",
    )(u.reshape(ns, 1, HG_WIDTH), mix.reshape(ns, 1, HG_K), cache_mk, cache_mv)
    return br.reshape(ns, D_MODEL)


def _hgrn_sample_kernel(q_ref, f_ref, iv_ref, lbl_ref, nw_ref, st_ref, mix_ref, so_ref, o_scr, *, layer, ns):
    lb = _lower_bound(lbl_ref[...], layer)
    fg = lb + (1.0 - lb) * jax.nn.sigmoid(f_ref[...])
    kk_t = (1.0 - fg).T
    fg_t = fg.T
    qs = _silu(q_ref[...])
    for r in range(ns):
        state = fg_t[:, r:r + 1] * st_ref[r] + kk_t[:, r:r + 1] * iv_ref[r:r + 1, :]
        so_ref[r] = state
        q8 = jnp.broadcast_to(qs[r:r + 1, :], (V7X_SUBLANES, HG_D))
        o_scr[r:r + 1, :] = _dot(q8, state)[0:1, :]
    o = o_scr[...]
    ms = jnp.mean(o * o, axis=-1, keepdims=True)
    mix_ref[...] = o * lax.rsqrt(ms + RMS_EPS) * nw_ref[...]


def _hgrn_sample_step(u, state, j, lb_logits, norm_w, layer):
    ns = u.shape[0]
    col = lambda part: pl.BlockSpec((ns, HG_D), lambda h: (0, part * HG_HEADS + h))
    kernel = functools.partial(_hgrn_sample_kernel, layer=layer, ns=ns)
    return pl.pallas_call(
        kernel,
        grid=(HG_HEADS,),
        in_specs=[col(0), col(1), col(2),
                  pl.BlockSpec((DEPTH, HG_D), lambda h: (0, h)),
                  pl.BlockSpec((1, HG_D), lambda h: (0, h)),
                  pl.BlockSpec((None, ns, None, HG_D, HG_D), lambda h: (j, 0, h, 0, 0))],
        out_specs=[pl.BlockSpec((ns, HG_D), lambda h: (0, h)),
                   pl.BlockSpec((ns, None, HG_D, HG_D), lambda h: (0, h, 0, 0))],
        out_shape=[jax.ShapeDtypeStruct((ns, HG_K), F32),
                   jax.ShapeDtypeStruct(state.shape[1:], F32)],
        scratch_shapes=[pltpu.VMEM((ns, HG_D), F32)],
        compiler_params=_cparams(1),
        name="hgrn_sample_step",
    )(u, u, u, lb_logits, norm_w.reshape(1, HG_K), state)


def _outproj_ln_kernel(br_ref, h_ref, wout_ref, lnw_ref, lnb_ref, out_ref):
    y = jnp.dot(br_ref[...].astype(BF16), wout_ref[:, :D_MODEL], preferred_element_type=F32)
    out_ref[...] = _layer_norm(DEEPNORM_ALPHA * h_ref[...] + y, lnw_ref[...], lnb_ref[...])


def _outproj_ln(branch, h, w_out, ln_w, ln_b):
    ns, d = h.shape
    full = lambda shape: pl.BlockSpec(shape, lambda i: (0, 0))
    return pl.pallas_call(
        _outproj_ln_kernel,
        grid=(1,),
        in_specs=[full((ns, d)), full((ns, d)), full(w_out.shape), full((1, d)), full((1, d))],
        out_specs=full((ns, d)),
        out_shape=jax.ShapeDtypeStruct((ns, d), F32),
        compiler_params=_cparams(1),
        name="outproj_ln",
    )(branch, h, w_out, ln_w.reshape(1, d), ln_b.reshape(1, d))


def kernel(x_prompt, x_sample, cache_mem_k, cache_mem_v, cache_swa_k, cache_swa_v, state_hgrn, mem_prompt, rel_bias, swa_w_in, swa_sinks, hg_w_in, hg_lb_logits, hg_norm_w, w_mem_k, w_mem_v, w_out, ln_w, ln_b):
    bp, t, d = x_prompt.shape
    ns = x_sample.shape[0]
    assert d == D_MODEL and x_sample.shape[1] == 1 and t % PROMPT_ROWS == 0
    assert cache_swa_k.shape[2] == WINDOW <= PAST_LEN
    assert w_mem_k.shape[0] == DEPTH

    wo = [_pad_cols(w_out[i].astype(BF16)) if i % 2 else w_out[i].astype(BF16) for i in range(DEPTH)]
    mem_k_prompt, mem_v_prompt, mk_bf, mv_bf = _mem_kv_proj(mem_prompt, w_mem_k, w_mem_v)
    cmk, cmv = _head_minor_rows(cache_mem_k), _head_minor_rows(cache_mem_v)
    csk, csv = _rows_minor(cache_swa_k), _rows_minor(cache_swa_v)

    hp = x_prompt
    hs = x_sample.reshape(ns, d)
    swa_kp, swa_vp, swa_ks, swa_vs, hg_sp, hg_ss = [], [], [], [], [], []
    for i in range(DEPTH):
        j = i // 2
        if i % 2 == 0:
            w_in = _pad_cols(swa_w_in[j].astype(BF16))
            hp, kw, vw = _swa_prompt_layer(hp, w_in, wo[i], mk_bf, mv_bf, i, rel_bias, swa_sinks[j],
                                           ln_w[i], ln_b[i], rows=PROMPT_ROWS)
            swa_kp.append(kw.reshape(bp, WINDOW, SWA_KVH, SWA_HD))
            swa_vp.append(vw.reshape(bp, WINDOW, SWA_KVH, SWA_HD))
            u = _proj(hs, w_in, SWA_WIDTH, "sample_swa_proj")
            br, ko, vo = _sample_swa_attend(u, csk, csv, j, cmk, cmv, i, rel_bias, swa_sinks[j])
            swa_ks.append(jnp.moveaxis(ko, -1, -3))
            swa_vs.append(jnp.moveaxis(vo, -1, -3))
        else:
            w_in = _pad_cols(hg_w_in[j].astype(BF16))
            hp, st = _hgrn_prompt_layer(hp, w_in, wo[i], mk_bf, mv_bf, i, hg_lb_logits, hg_norm_w[j],
                                        ln_w[i], ln_b[i])
            hg_sp.append(st)
            u = _proj(hs, w_in, HG_WIDTH, "sample_hgrn_proj")
            mix, so = _hgrn_sample_step(u, state_hgrn, j, hg_lb_logits, hg_norm_w[j], i)
            hg_ss.append(so)
            br = _sample_hgrn_gate(u, mix, cmk, cmv, i)
        hs = _outproj_ln(br, hs, wo[i], ln_w[i], ln_b[i])

    return (hp, hs.reshape(ns, 1, d), mem_k_prompt, mem_v_prompt,
            jnp.stack(swa_kp), jnp.stack(swa_vp), jnp.stack(hg_sp),
            jnp.stack(swa_ks), jnp.stack(swa_vs), jnp.stack(hg_ss))
```

```python
import functools
import math

import numpy as np
import jax
import jax.numpy as jnp
from jax import lax
from jax.experimental import pallas as pl
from jax.experimental.pallas import tpu as pltpu

F32 = jnp.float32
BF16 = jnp.bfloat16

D_MODEL = 2048
DEPTH = 2
PAST_LEN = 8192
N_MEM = 256
MEM_HEADS = 4
MEM_HD = 128
MEM_Q = MEM_HEADS * MEM_HD
WINDOW = 128
SWA_HD = 64
SWA_KVH = 4
SWA_GROUP = 6
SWA_QH = SWA_KVH * SWA_GROUP
SWA_Q = SWA_QH * SWA_HD
SWA_KV = SWA_KVH * SWA_HD
N_BUCKETS = 32
HG_HEADS = 12
HG_D = 128
HG_K = HG_HEADS * HG_D
SWA_WIDTH = SWA_Q + 2 * SWA_KV + MEM_Q + D_MODEL
HG_WIDTH = 3 * HG_K + MEM_Q + D_MODEL
DEEPNORM_ALPHA = (2.0 * DEPTH) ** 0.25
LN_EPS = 1e-5
RMS_EPS = 1e-6
MASKED = -1e30

V7X_SUBLANES = 8
V7X_LANES = 128
V7X_MXU_COLS = 256
V7X_VMEM_LIMIT = 60000 * 1024
PROJ_COLS = 512
GATE_CHUNK = 512
PROMPT_ROWS = 256


def _t5_bucket_ranges():
    n = np.arange(WINDOW)
    max_exact = N_BUCKETS // 2
    nf = np.maximum(n, 1).astype(np.float64)
    large = max_exact + (np.log(nf / max_exact) / math.log(WINDOW / max_exact)
                         * (N_BUCKETS - max_exact)).astype(np.int64)
    bucket = np.where(n < max_exact, n, np.minimum(large, N_BUCKETS - 1))
    ranges = []
    for b in range(N_BUCKETS):
        idx = np.nonzero(bucket == b)[0]
        ranges.append((int(idx[0]), int(idx[-1])) if idx.size else None)
    return ranges


_BUCKET_RANGES = _t5_bucket_ranges()


def _dot(a, b):
    return jnp.dot(a.astype(BF16), b.astype(BF16), preferred_element_type=F32)


def _dot_nt(a, b):
    return lax.dot_general(a.astype(BF16), b.astype(BF16), (((1,), (1,)), ((), ())),
                           preferred_element_type=F32)


def _layer_norm(z, w, b):
    mu = jnp.mean(z, axis=-1, keepdims=True)
    zc = z - mu
    var = jnp.mean(zc * zc, axis=-1, keepdims=True)
    return zc * lax.rsqrt(var + LN_EPS) * w + b


def _silu(x):
    return x * jax.nn.sigmoid(x)


def _cparams(n_axes):
    return pltpu.CompilerParams(dimension_semantics=("arbitrary",) * n_axes,
                                vmem_limit_bytes=V7X_VMEM_LIMIT)


def _resident(shape):
    nd = len(shape)
    return pl.BlockSpec(shape, lambda *_: (0,) * nd, pipeline_mode=pl.Buffered(1))


_SMEM = pl.BlockSpec(memory_space=pltpu.SMEM)


def _mem_kv_kernel(x_ref, wk_ref, wv_ref, ko_ref, vo_ref, kb_ref, vb_ref):
    xb = x_ref[...].astype(BF16)
    for layer in range(wk_ref.shape[0]):
        for w_ref, o_ref, b_ref in ((wk_ref, ko_ref, kb_ref), (wv_ref, vo_ref, vb_ref)):
            y = jnp.dot(xb, w_ref[layer].astype(BF16), preferred_element_type=F32)
            b_ref[layer] = y.astype(BF16)
            for hm in range(MEM_HEADS):
                o_ref[layer, :, hm, :] = y[:, hm * MEM_HD:(hm + 1) * MEM_HD]


def _mem_kv_proj(mem, w_k, w_v):
    bsz, n, d = mem.shape
    nl = w_k.shape[0]
    w_spec = _resident((nl, d, MEM_Q))
    o_spec = pl.BlockSpec((nl, None, n, MEM_HEADS, MEM_HD), lambda b: (0, b, 0, 0, 0))
    b_spec = pl.BlockSpec((nl, n, MEM_Q), lambda b: (0, b, 0))
    o_shape = jax.ShapeDtypeStruct((nl, bsz, n, MEM_HEADS, MEM_HD), F32)
    b_shape = jax.ShapeDtypeStruct((nl, bsz * n, MEM_Q), BF16)
    return pl.pallas_call(
        _mem_kv_kernel,
        grid=(bsz,),
        in_specs=[pl.BlockSpec((None, n, d), lambda b: (b, 0, 0)), w_spec, w_spec],
        out_specs=[o_spec, o_spec, b_spec, b_spec],
        out_shape=[o_shape, o_shape, b_shape, b_shape],
        compiler_params=_cparams(1),
        name="mem_kv_proj",
    )(mem, w_k, w_v)


def _proj_kernel(x_ref, w_ref, o_ref):
    o_ref[...] = jnp.dot(x_ref[...].astype(BF16), w_ref[...], preferred_element_type=F32)


def _pad_cols(w):
    if w.shape[-1] % (4 * V7X_MXU_COLS) != 0:
        return w
    return jnp.concatenate([w, jnp.zeros(w.shape[:-1] + (V7X_MXU_COLS,), w.dtype)], axis=-1)


def _proj(x, w, n, name):
    m, k = x.shape
    return pl.pallas_call(
        _proj_kernel,
        grid=(n // PROJ_COLS,),
        in_specs=[pl.BlockSpec((m, k), lambda j: (0, 0)),
                  pl.BlockSpec((k, PROJ_COLS), lambda j: (0, j))],
        out_specs=pl.BlockSpec((m, PROJ_COLS), lambda j: (0, j)),
        out_shape=jax.ShapeDtypeStruct((m, n), F32),
        compiler_params=_cparams(1),
        name=name,
    )(x, w)


def _mem_scores(mq_all, mk_ref):
    return [_dot_nt(mq_all[:, hm * MEM_HD:(hm + 1) * MEM_HD], mk_ref[:, hm * MEM_HD:(hm + 1) * MEM_HD])
            * (MEM_HD ** -0.5) for hm in range(MEM_HEADS)]


def _put_gated(bb_ref, gs_ref, col0, o):
    cols = slice(col0, col0 + o.shape[-1])
    bb_ref[:, cols] = (o * gs_ref[:, cols]).astype(BF16)


def _mem_values(weights, mv_ref, bb_ref, gs_ref, out_col0):
    for hm, (p, l) in enumerate(weights):
        lo = hm * MEM_HD
        _put_gated(bb_ref, gs_ref, out_col0 + lo, _dot(p, mv_ref[:, lo:lo + MEM_HD]) / l)


def _swa_prompt_kernel(relb_ref, sink_ref, h_ref, win_ref, wout_ref, mk_ref, mv_ref, lnw_ref, lnb_ref,
                       out_ref, kout_ref, vout_ref,
                       bias_scr, k_scr, v_scr, gs_scr, bb_scr, *, rows):
    b = pl.program_id(0)
    t = pl.program_id(1)
    w2 = 2 * WINDOW

    @pl.when(jnp.logical_and(b == 0, t == 0))
    def _build_bias():
        qi = lax.broadcasted_iota(jnp.int32, (WINDOW, w2), 0)
        kj = lax.broadcasted_iota(jnp.int32, (WINDOW, w2), 1)
        dist = qi + WINDOW - kj

        def per_head(hq, carry):
            acc = jnp.full((WINDOW, w2), MASKED, F32)
            for bk, rng in enumerate(_BUCKET_RANGES):
                if rng is not None:
                    hit = jnp.logical_and(dist >= rng[0], dist <= rng[1])
                    acc = jnp.where(hit, relb_ref[bk, hq], acc)
            bias_scr[0, hq] = acc
            bias_scr[1, hq] = jnp.where(kj < WINDOW, MASKED, acc)
            return carry

        lax.fori_loop(0, SWA_QH, per_head, 0)

    @pl.when(t == 0)
    def _reset_window():
        k_scr[0:WINDOW, :] = jnp.zeros((WINDOW, SWA_KV), BF16)
        v_scr[0:WINDOW, :] = jnp.zeros((WINDOW, SWA_KV), BF16)

    qkv_w = SWA_Q + 2 * SWA_KV
    n_blocks = rows // WINDOW
    n_gate = D_MODEL // GATE_CHUNK
    heads = [(g, j) for g in range(SWA_KVH) for j in range(SWA_GROUP)]
    xb, qkv, mq, scores, mem_s = {}, {}, {}, {}, {}

    def project(qb):
        r0 = qb * WINDOW
        xb[qb] = h_ref[0, r0:r0 + WINDOW, :].astype(BF16)
        qkv[qb] = jnp.dot(xb[qb], win_ref[:, :qkv_w], preferred_element_type=F32)
        mq[qb] = jnp.dot(xb[qb], win_ref[:, qkv_w:qkv_w + MEM_Q], preferred_element_type=F32)
        k = qkv[qb][:, SWA_Q:SWA_Q + SWA_KV]
        v = qkv[qb][:, SWA_Q + SWA_KV:]
        if qb == n_blocks - 1:
            kout_ref[0] = k
            vout_ref[0] = v
        k_scr[WINDOW + r0:w2 + r0, :] = k.astype(BF16)
        v_scr[WINDOW + r0:w2 + r0, :] = v.astype(BF16)

    def score(qb):
        r0 = qb * WINDOW
        table = jnp.where(t == 0, 1, 0) if qb == 0 else 0
        scores[qb] = []
        for idx, (g, j) in enumerate(heads):
            hq = g * SWA_GROUP + j
            q = qkv[qb][:, hq * SWA_HD:(hq + 1) * SWA_HD] * (SWA_HD ** -0.5)
            scores[qb].append(_dot_nt(q, k_scr[r0:r0 + w2, g * SWA_HD:(g + 1) * SWA_HD]) + bias_scr[table, hq])
            for chunk in range(idx * n_gate // len(heads), (idx + 1) * n_gate // len(heads)):
                lo = chunk * GATE_CHUNK
                gate = jnp.dot(xb[qb], win_ref[:, qkv_w + MEM_Q + lo:qkv_w + MEM_Q + lo + GATE_CHUNK],
                               preferred_element_type=F32)
                gs_scr[r0:r0 + WINDOW, lo:lo + GATE_CHUNK] = _silu(gate)
        mem_s[qb] = _mem_scores(mq[qb], mk_ref)

    def attend(qb):
        r0 = qb * WINDOW
        bb = bb_scr.at[r0:r0 + WINDOW, :]
        gs = gs_scr.at[r0:r0 + WINDOW, :]
        sinks = [sink_ref[g * SWA_GROUP + j] for g, j in heads]
        maxes = [jnp.maximum(jnp.max(s, axis=-1, keepdims=True), sk) for s, sk in zip(scores[qb], sinks)]
        mem_max = [jnp.max(s, axis=-1, keepdims=True) for s in mem_s[qb]]
        probs = [jnp.exp(s - m) for s, m in zip(scores[qb], maxes)]
        mem_p = [jnp.exp(s - m) for s, m in zip(mem_s[qb], mem_max)]
        weights = [(p, jnp.sum(p, axis=-1, keepdims=True) + jnp.exp(sk - m))
                   for p, m, sk in zip(probs, maxes, sinks)]
        mem_w = [(p, jnp.sum(p, axis=-1, keepdims=True)) for p in mem_p]
        for (g, j), (p, l) in zip(heads, weights):
            o = _dot(p, v_scr[r0:r0 + w2, g * SWA_HD:(g + 1) * SWA_HD]) / l
            _put_gated(bb, gs, (g * SWA_GROUP + j) * SWA_HD, o)
        _mem_values(mem_w, mv_ref, bb, gs, SWA_Q)

    def finish(qb):
        r0 = qb * WINDOW
        y = jnp.dot(bb_scr[r0:r0 + WINDOW, :], wout_ref[:, :D_MODEL], preferred_element_type=F32)
        out_ref[0, r0:r0 + WINDOW, :] = _layer_norm(DEEPNORM_ALPHA * h_ref[0, r0:r0 + WINDOW, :] + y,
                                                    lnw_ref[...], lnb_ref[...])

    project(0)
    score(0)
    for qb in range(n_blocks):
        if qb + 1 < n_blocks:
            project(qb + 1)
            score(qb + 1)
        attend(qb)
        finish(qb)

    k_scr[0:WINDOW, :] = k_scr[rows:rows + WINDOW, :]
    v_scr[0:WINDOW, :] = v_scr[rows:rows + WINDOW, :]


def _swa_prompt_layer(h, w_in, w_out, mk_bf, mv_bf, layer, rel_bias, sinks, ln_w, ln_b, rows):
    bsz, t, d = h.shape
    kernel = functools.partial(_swa_prompt_kernel, rows=rows)
    mem_spec = pl.BlockSpec((None, N_MEM, MEM_Q), lambda b, i: (layer, b, 0))
    return pl.pallas_call(
        kernel,
        grid=(bsz, t // rows),
        in_specs=[_SMEM, _SMEM,
                  pl.BlockSpec((1, rows, d), lambda b, i: (b, i, 0)),
                  _resident(w_in.shape),
                  _resident(w_out.shape),
                  mem_spec, mem_spec,
                  _resident((1, d)), _resident((1, d))],
        out_specs=[pl.BlockSpec((1, rows, d), lambda b, i: (b, i, 0)),
                   pl.BlockSpec((1, WINDOW, SWA_KV), lambda b, i: (b, 0, 0)),
                   pl.BlockSpec((1, WINDOW, SWA_KV), lambda b, i: (b, 0, 0))],
        out_shape=[jax.ShapeDtypeStruct((bsz, t, d), F32),
                   jax.ShapeDtypeStruct((bsz, WINDOW, SWA_KV), F32),
                   jax.ShapeDtypeStruct((bsz, WINDOW, SWA_KV), F32)],
        scratch_shapes=[pltpu.VMEM((2, SWA_QH, WINDOW, 2 * WINDOW), F32),
                        pltpu.VMEM((WINDOW + rows, SWA_KV), BF16),
                        pltpu.VMEM((WINDOW + rows, SWA_KV), BF16),
                        pltpu.VMEM((rows, d), F32),
                        pltpu.VMEM((rows, d), BF16)],
        compiler_params=_cparams(2),
        name="swa_prompt_layer",
    )(rel_bias, sinks, h, w_in, w_out, mk_bf, mv_bf, ln_w.reshape(1, d), ln_b.reshape(1, d))


HG_CHUNK = 128
HG_DIRECT = 8


def _lower_bound(logits, layer):
    e = jnp.exp(logits - jnp.max(logits, axis=0, keepdims=True))
    sm = e / jnp.sum(e, axis=0, keepdims=True)
    return jnp.sum(sm[1:layer + 1, :], axis=0, keepdims=True)


def _cumsum_rows(tril_bf, g):
    w = g.shape[1]
    g1 = g.astype(BF16)
    r1 = g - g1.astype(F32)
    g2 = r1.astype(BF16)
    g3 = (r1 - g2.astype(F32)).astype(BF16)
    d = jnp.dot(tril_bf, jnp.concatenate([g1, g2, g3], axis=1), preferred_element_type=F32)
    return (d[:, :w] + d[:, w:2 * w]) + d[:, 2 * w:]


def _hgrn_chunk_operands(qs, kk, gc):
    c = HG_CHUNK
    nb = c // HG_DIRECT
    q3 = qs.reshape(nb, HG_DIRECT, HG_D)
    g3 = gc.reshape(nb, HG_DIRECT, HG_D)
    k3 = kk.reshape(nb, HG_DIRECT, HG_D)
    lane = lax.broadcasted_iota(jnp.int32, (nb, HG_DIRECT, c), 2)
    blk = lax.broadcasted_iota(jnp.int32, (nb, HG_DIRECT, c), 0)
    trow = lax.broadcasted_iota(jnp.int32, (nb, HG_DIRECT, c), 1)
    a3 = jnp.zeros((nb, HG_DIRECT, c), F32)
    for sl in range(HG_DIRECT):
        e = jnp.exp2(g3 - g3[:, sl:sl + 1, :])
        col = jnp.sum(e * q3 * k3[:, sl:sl + 1, :], axis=-1, keepdims=True)
        hit = jnp.logical_and(lane == blk * HG_DIRECT + sl, trow >= sl)
        a3 = jnp.where(hit, col, a3)
    ri = lax.broadcasted_iota(jnp.int32, (c, 1), 0)
    levels = []
    half = HG_DIRECT
    while half < c:
        span = 2 * half
        gsp = gc.reshape(c // span, span, HG_D)
        gref = jnp.broadcast_to(gsp[:, half - 1:half, :], gsp.shape).reshape(c, HG_D)
        upper = (ri % span) >= half
        eq = jnp.where(upper, jnp.exp2(gc - gref), 0.0)
        ek = jnp.where(upper, 0.0, jnp.exp2(gref - gc))
        levels.append(((qs * eq).astype(BF16), (kk * ek).astype(BF16)))
        half = span
    return a3.reshape(c, c), levels


def _hgrn_chunk_products(direct, levels):
    c = HG_CHUNK
    ti = lax.broadcasted_iota(jnp.int32, (c, c), 0)
    si = lax.broadcasted_iota(jnp.int32, (c, c), 1)
    a = direct
    for lvl, (ql, kl) in enumerate(levels):
        span = 2 * HG_DIRECT << lvl
        a = a + jnp.where((ti // span) == (si // span), _dot_nt(ql, kl), 0.0)
    return a


def _hgrn_prompt_kernel(h_ref, win_ref, wout_ref, lbl_ref, nw_ref, mk_ref, mv_ref, lnw_ref, lnb_ref,
                        out_ref, st_ref,
                        st_scr, br_scr, *, layer):
    t = pl.program_id(1)
    c = HG_CHUNK
    heads = range(HG_HEADS)

    @pl.when(t == 0)
    def _reset_state():
        st_scr[...] = jnp.zeros(st_scr.shape, F32)

    lb_all = _lower_bound(lbl_ref[...], layer)
    ri = lax.broadcasted_iota(jnp.int32, (c, c), 0)
    ci = lax.broadcasted_iota(jnp.int32, (c, c), 1)
    tril_bf = jnp.where(ri >= ci, 1.0, 0.0).astype(BF16)

    x = h_ref[0]
    xb = x.astype(BF16)
    pair_w = 2 * HG_D
    n_pairs = HG_HEADS // 2
    qs, kk, iv, gc, operands = [], [], [], [], []

    def project(pair):
        return [jnp.dot(xb, win_ref[:, part * HG_K + pair * pair_w:part * HG_K + (pair + 1) * pair_w],
                        preferred_element_type=F32) for part in range(3)]

    def gates(pair, uq, uf, ui):
        lb = lb_all[:, pair * pair_w:(pair + 1) * pair_w]
        fg = lb + (1.0 - lb) * jax.nn.sigmoid(uf)
        gc2 = _cumsum_rows(tril_bf, jnp.log2(fg))
        for cols in (slice(0, HG_D), slice(HG_D, pair_w)):
            qs.append(_silu(uq[:, cols]))
            iv.append(ui[:, cols])
            kk.append(1.0 - fg[:, cols])
            gc.append(gc2[:, cols])
            operands.append(_hgrn_chunk_operands(qs[-1], kk[-1], gc[-1]))

    a = {}

    def products(pair):
        for hh in (2 * pair, 2 * pair + 1):
            a[hh] = _hgrn_chunk_products(*operands[hh])

    parts = [project(0)]
    for pair in range(1, n_pairs):
        parts.append(project(pair))
        gates(pair - 1, *parts[pair - 1])
        if pair >= 2:
            products(pair - 2)
    ug = jnp.dot(xb, win_ref[:, 3 * HG_K:HG_WIDTH], preferred_element_type=F32)
    gates(n_pairs - 1, *parts[n_pairs - 1])
    products(n_pairs - 2)
    mem_s = _mem_scores(ug[:, :MEM_Q], mk_ref)
    mem_p = [jnp.exp(s - jnp.max(s, axis=-1, keepdims=True)) for s in mem_s]
    for hm, p in enumerate(mem_p):
        lo = hm * MEM_HD
        br_scr[:, HG_K + lo:HG_K + lo + MEM_HD] = (_dot(p, mv_ref[:, lo:lo + MEM_HD])
                                                   / jnp.sum(p, axis=-1, keepdims=True))

    products(n_pairs - 1)

    iv_t = [iv[hh].T for hh in heads]
    for hh in heads:
        lhs = jnp.concatenate([a[hh], qs[hh] * jnp.exp2(gc[hh])], axis=1)
        rhs = jnp.concatenate([iv_t[hh], st_scr[hh]], axis=1)
        o = _dot_nt(lhs, rhs)
        ms = jnp.mean(o * o, axis=-1, keepdims=True)
        br_scr[:, hh * HG_D:(hh + 1) * HG_D] = o * lax.rsqrt(ms + RMS_EPS) * nw_ref[:, hh * HG_D:(hh + 1) * HG_D]

    for hh in heads:
        g_last = gc[hh][c - 1:c, :]
        kdec = kk[hh] * jnp.exp2(g_last - gc[hh])
        st_scr[hh] = st_scr[hh] * jnp.exp2(g_last) + _dot(iv_t[hh], kdec)

    @pl.when(t == pl.num_programs(1) - 1)
    def _emit_state():
        for hh in heads:
            st_ref[0, hh] = st_scr[hh].T

    branch = br_scr[...] * _silu(ug[:, MEM_Q:])
    y = jnp.dot(branch.astype(BF16), wout_ref[:, :D_MODEL], preferred_element_type=F32)
    out_ref[0] = _layer_norm(DEEPNORM_ALPHA * x + y, lnw_ref[...], lnb_ref[...])


def _hgrn_prompt_layer(h, w_in, w_out, mk_bf, mv_bf, layer, lb_logits, norm_w, ln_w, ln_b):
    bsz, t, d = h.shape
    rows = HG_CHUNK
    kernel = functools.partial(_hgrn_prompt_kernel, layer=layer)
    mem_spec = pl.BlockSpec((None, N_MEM, MEM_Q), lambda b, i: (layer, b, 0))
    return pl.pallas_call(
        kernel,
        grid=(bsz, t // rows),
        in_specs=[pl.BlockSpec((1, rows, d), lambda b, i: (b, i, 0)),
                  _resident(w_in.shape),
                  _resident(w_out.shape),
                  _resident((DEPTH, HG_K)),
                  _resident((1, HG_K)),
                  mem_spec, mem_spec,
                  _resident((1, d)), _resident((1, d))],
        out_specs=[pl.BlockSpec((1, rows, d), lambda b, i: (b, i, 0)),
                   pl.BlockSpec((1, HG_HEADS, HG_D, HG_D), lambda b, i: (b, 0, 0, 0))],
        out_shape=[jax.ShapeDtypeStruct((bsz, t, d), F32),
                   jax.ShapeDtypeStruct((bsz, HG_HEADS, HG_D, HG_D), F32)],
        scratch_shapes=[pltpu.VMEM((HG_HEADS, HG_D, HG_D), F32),
                        pltpu.VMEM((rows, d), F32)],
        compiler_params=_cparams(2),
        name="hgrn_prompt_layer",
    )(h, w_in, w_out, lb_logits, norm_w.reshape(1, HG_K), mk_bf, mv_bf, ln_w.reshape(1, d), ln_b.reshape(1, d))


REQS_PER_STEP = 4


def _heads_to_sublanes(row, col0, n, width):
    return jnp.concatenate([row[:, col0 + i * width:col0 + (i + 1) * width] for i in range(n)], axis=0)


def _sample_mem_attend(urow, col0, cmk_ref, cmv_ref, r):
    pieces = []
    for hm in range(MEM_HEADS):
        lo = col0 + hm * MEM_HD
        rows = pl.ds(hm, N_MEM, stride=MEM_HEADS)
        s = jnp.sum(cmk_ref[r, rows, :] * urow[:, lo:lo + MEM_HD], axis=-1, keepdims=True) * (MEM_HD ** -0.5)
        s = jnp.broadcast_to(s, (N_MEM, MEM_HD))
        p = jnp.exp(s - jnp.max(s, axis=0, keepdims=True))
        pieces.append(jnp.sum(p * cmv_ref[r, rows, :], axis=0, keepdims=True) / jnp.sum(p, axis=0, keepdims=True))
    return jnp.concatenate(pieces, axis=1)


def _sample_swa_kernel(relb_ref, sink_ref, u_ref, ck_ref, cv_ref, cmk_ref, cmv_ref,
                       br_ref, ko_ref, vo_ref, bias_scr, bias0_scr, sink_scr):
    nbuf = WINDOW

    @pl.when(pl.program_id(0) == 0)
    def _build_tables():
        dist = nbuf - lax.broadcasted_iota(jnp.int32, (V7X_SUBLANES, nbuf), 1)
        rid = lax.broadcasted_iota(jnp.int32, (V7X_SUBLANES, nbuf), 0)
        for g in range(SWA_KVH):
            acc = jnp.where(rid < SWA_GROUP, MASKED, 0.0).astype(F32)
            acc0 = jnp.zeros((V7X_SUBLANES, nbuf), F32)
            sk = jnp.zeros((V7X_SUBLANES, nbuf), F32)
            for j in range(SWA_GROUP):
                hq = g * SWA_GROUP + j
                sk = jnp.where(rid == j, sink_ref[hq], sk)
                acc0 = jnp.where(rid == j, relb_ref[0, hq], acc0)
                for bk, rng in enumerate(_BUCKET_RANGES):
                    if rng is not None:
                        hit = jnp.logical_and(rid == j, jnp.logical_and(dist >= rng[0], dist <= rng[1]))
                        acc = jnp.where(hit, relb_ref[bk, hq], acc)
            bias_scr[g] = acc
            bias0_scr[g] = acc0
            sink_scr[g] = sk

    g0 = SWA_Q + 2 * SWA_KV + MEM_Q
    pad = jnp.zeros((V7X_SUBLANES - SWA_GROUP, SWA_HD), F32)
    pairs = [(r, g) for r in range(REQS_PER_STEP) for g in range(SWA_KVH)]
    urows = [u_ref[r] for r in range(REQS_PER_STEP)]

    diag = (lax.broadcasted_iota(jnp.int32, (SWA_HD, SWA_HD), 0)
            == lax.broadcasted_iota(jnp.int32, (SWA_HD, SWA_HD), 1))
    newest = lax.broadcasted_iota(jnp.int32, (SWA_HD, nbuf), 1) == nbuf - 1

    def rolled(cache_t, new_row):
        new_col = jnp.sum(jnp.where(diag, new_row, 0.0), axis=-1, keepdims=True)
        return jnp.where(newest, new_col, pltpu.roll(cache_t, nbuf - 1, 1))

    scores = []
    for r, g in pairs:
        k_t = ck_ref[r, g]
        k_new = urows[r][:, SWA_Q + g * SWA_HD:SWA_Q + (g + 1) * SWA_HD]
        v_new = urows[r][:, SWA_Q + SWA_KV + g * SWA_HD:SWA_Q + SWA_KV + (g + 1) * SWA_HD]
        qg = jnp.concatenate(
            [urows[r][:, (g * SWA_GROUP + j) * SWA_HD:(g * SWA_GROUP + j + 1) * SWA_HD] for j in range(SWA_GROUP)]
            + [pad], axis=0) * (SWA_HD ** -0.5)
        s = _dot(qg, k_t) + bias_scr[g]
        s_new = jnp.sum(qg * k_new, axis=-1, keepdims=True) + bias0_scr[g][:, 0:1]
        scores.append((s, s_new, k_new, v_new))

    weights = []
    for (r, g), (s, s_new, _, v_new) in zip(pairs, scores):
        sink = sink_scr[g][:, 0:1]
        m = jnp.maximum(jnp.max(s, axis=-1, keepdims=True), jnp.maximum(s_new, sink))
        p = jnp.exp(s - m)
        p_new = jnp.exp(s_new - m)
        l = jnp.sum(p, axis=-1, keepdims=True) + p_new + jnp.exp(sink - m)
        weights.append((p, p_new * v_new, l))

    pieces = [[] for _ in range(REQS_PER_STEP)]
    for (r, g), (p, o_new, l) in zip(pairs, weights):
        o = (_dot_nt(p, cv_ref[r, g]) + o_new) / l
        pieces[r].extend(o[j:j + 1, :] for j in range(SWA_GROUP))

    for (r, g), (_, _, k_new, v_new) in zip(pairs, scores):
        ko_ref[r, g] = rolled(ck_ref[r, g], k_new)
        vo_ref[r, g] = rolled(cv_ref[r, g], v_new)

    for r in range(REQS_PER_STEP):
        mem_o = _sample_mem_attend(urows[r], SWA_Q + 2 * SWA_KV, cmk_ref, cmv_ref, r)
        br_ref[r] = jnp.concatenate(pieces[r] + [mem_o], axis=1) * _silu(urows[r][:, g0:g0 + D_MODEL])


def _cache_spec(layer, shape):
    return pl.BlockSpec((None, REQS_PER_STEP) + shape, lambda i: (layer, i) + (0,) * len(shape))


def _row_spec(width):
    return pl.BlockSpec((REQS_PER_STEP, 1, width), lambda i: (i, 0, 0))


def _head_minor_rows(cache):
    nl, ns, rows, heads, hd = cache.shape
    return cache.reshape(nl, ns, rows * heads, hd)


def _rows_minor(cache):
    return jnp.moveaxis(cache, -3, -1)


def _sample_swa_attend(u, cache_k, cache_v, j, cache_mk, cache_mv, layer, rel_bias, sinks):
    ns = u.shape[0]
    nbuf = cache_k.shape[-1]
    swa_shape = (SWA_KVH, SWA_HD, nbuf)
    mem_shape = (N_MEM * MEM_HEADS, MEM_HD)
    new_cache = pl.BlockSpec((REQS_PER_STEP,) + swa_shape, lambda i: (i, 0, 0, 0))
    br, ko, vo = pl.pallas_call(
        _sample_swa_kernel,
        grid=(ns // REQS_PER_STEP,),
        in_specs=[_SMEM, _SMEM, _row_spec(SWA_WIDTH), _cache_spec(j, swa_shape), _cache_spec(j, swa_shape),
                  _cache_spec(layer, mem_shape), _cache_spec(layer, mem_shape)],
        out_specs=[_row_spec(D_MODEL), new_cache, new_cache],
        out_shape=[jax.ShapeDtypeStruct((ns, 1, D_MODEL), F32),
                   jax.ShapeDtypeStruct((ns,) + swa_shape, F32),
                   jax.ShapeDtypeStruct((ns,) + swa_shape, F32)],
        scratch_shapes=[pltpu.VMEM((SWA_KVH, V7X_SUBLANES, nbuf), F32)] * 3,
        compiler_params=_cparams(1),
        name="sample_swa_attend",
    )(rel_bias, sinks, u.reshape(ns, 1, SWA_WIDTH), cache_k, cache_v, cache_mk, cache_mv)
    return br.reshape(ns, D_MODEL), ko, vo


def _sample_hgrn_gate_kernel(u_ref, mix_ref, cmk_ref, cmv_ref, br_ref):
    g0 = 3 * HG_K + MEM_Q
    for r in range(REQS_PER_STEP):
        urow = u_ref[r]
        mem_o = _sample_mem_attend(urow, 3 * HG_K, cmk_ref, cmv_ref, r)
        br_ref[r] = jnp.concatenate([mix_ref[r], mem_o], axis=1) * _silu(urow[:, g0:g0 + D_MODEL])


def _sample_hgrn_gate(u, mix, cache_mk, cache_mv, layer):
    ns = u.shape[0]
    mem_shape = (N_MEM * MEM_HEADS, MEM_HD)
    br = pl.pallas_call(
        _sample_hgrn_gate_kernel,
        grid=(ns // REQS_PER_STEP,),
        in_specs=[_row_spec(HG_WIDTH), _row_spec(HG_K), _cache_spec(layer, mem_shape), _cache_spec(layer, mem_shape)],
        out_specs=_row_spec(D_MODEL),
        out_shape=jax.ShapeDtypeStruct((ns, 1, D_MODEL), F32),
        compiler_params=_cparams(1),
        name="sample_hgrn_gate",
    )(u.reshape(ns, 1, HG_WIDTH), mix.reshape(ns, 1, HG_K), cache_mk, cache_mv)
    return br.reshape(ns, D_MODEL)


def _hgrn_sample_kernel(q_ref, f_ref, iv_ref, lbl_ref, nw_ref, st_ref, mix_ref, so_ref, o_scr, *, layer, ns):
    lb = _lower_bound(lbl_ref[...], layer)
    fg = lb + (1.0 - lb) * jax.nn.sigmoid(f_ref[...])
    kk_t = (1.0 - fg).T
    fg_t = fg.T
    qs = _silu(q_ref[...])
    for r in range(ns):
        state = fg_t[:, r:r + 1] * st_ref[r] + kk_t[:, r:r + 1] * iv_ref[r:r + 1, :]
        so_ref[r] = state
        q8 = jnp.broadcast_to(qs[r:r + 1, :], (V7X_SUBLANES, HG_D))
        o_scr[r:r + 1, :] = _dot(q8, state)[0:1, :]
    o = o_scr[...]
    ms = jnp.mean(o * o, axis=-1, keepdims=True)
    mix_ref[...] = o * lax.rsqrt(ms + RMS_EPS) * nw_ref[...]


def _hgrn_sample_step(u, state, j, lb_logits, norm_w, layer):
    ns = u.shape[0]
    col = lambda part: pl.BlockSpec((ns, HG_D), lambda h: (0, part * HG_HEADS + h))
    kernel = functools.partial(_hgrn_sample_kernel, layer=layer, ns=ns)
    return pl.pallas_call(
        kernel,
        grid=(HG_HEADS,),
        in_specs=[col(0), col(1), col(2),
                  pl.BlockSpec((DEPTH, HG_D), lambda h: (0, h)),
                  pl.BlockSpec((1, HG_D), lambda h: (0, h)),
                  pl.BlockSpec((None, ns, None, HG_D, HG_D), lambda h: (j, 0, h, 0, 0))],
        out_specs=[pl.BlockSpec((ns, HG_D), lambda h: (0, h)),
                   pl.BlockSpec((ns, None, HG_D, HG_D), lambda h: (0, h, 0, 0))],
        out_shape=[jax.ShapeDtypeStruct((ns, HG_K), F32),
                   jax.ShapeDtypeStruct(state.shape[1:], F32)],
        scratch_shapes=[pltpu.VMEM((ns, HG_D), F32)],
        compiler_params=_cparams(1),
        name="hgrn_sample_step",
    )(u, u, u, lb_logits, norm_w.reshape(1, HG_K), state)


def _outproj_ln_kernel(br_ref, h_ref, wout_ref, lnw_ref, lnb_ref, out_ref):
    y = jnp.dot(br_ref[...].astype(BF16), wout_ref[:, :D_MODEL], preferred_element_type=F32)
    out_ref[...] = _layer_norm(DEEPNORM_ALPHA * h_ref[...] + y, lnw_ref[...], lnb_ref[...])


def _outproj_ln(branch, h, w_out, ln_w, ln_b):
    ns, d = h.shape
    full = lambda shape: pl.BlockSpec(shape, lambda i: (0, 0))
    return pl.pallas_call(
        _outproj_ln_kernel,
        grid=(1,),
        in_specs=[full((ns, d)), full((ns, d)), full(w_out.shape), full((1, d)), full((1, d))],
        out_specs=full((ns, d)),
        out_shape=jax.ShapeDtypeStruct((ns, d), F32),
        compiler_params=_cparams(1),
        name="outproj_ln",
    )(branch, h, w_out, ln_w.reshape(1, d), ln_b.reshape(1, d))


def kernel(x_prompt, x_sample, cache_mem_k, cache_mem_v, cache_swa_k, cache_swa_v, state_hgrn, mem_prompt, rel_bias, swa_w_in, swa_sinks, hg_w_in, hg_lb_logits, hg_norm_w, w_mem_k, w_mem_v, w_out, ln_w, ln_b):
    bp, t, d = x_prompt.shape
    ns = x_sample.shape[0]
    assert d == D_MODEL and x_sample.shape[1] == 1 and t % PROMPT_ROWS == 0
    assert cache_swa_k.shape[2] == WINDOW <= PAST_LEN
    assert w_mem_k.shape[0] == DEPTH

    wo = [_pad_cols(w_out[i].astype(BF16)) if i % 2 else w_out[i].astype(BF16) for i in range(DEPTH)]
    mem_k_prompt, mem_v_prompt, mk_bf, mv_bf = _mem_kv_proj(mem_prompt, w_mem_k, w_mem_v)
    cmk, cmv = _head_minor_rows(cache_mem_k), _head_minor_rows(cache_mem_v)
    csk, csv = _rows_minor(cache_swa_k), _rows_minor(cache_swa_v)

    hp = x_prompt
    hs = x_sample.reshape(ns, d)
    swa_kp, swa_vp, swa_ks, swa_vs, hg_sp, hg_ss = [], [], [], [], [], []
    for i in range(DEPTH):
        j = i // 2
        if i % 2 == 0:
            w_in = _pad_cols(swa_w_in[j].astype(BF16))
            hp, kw, vw = _swa_prompt_layer(hp, w_in, wo[i], mk_bf, mv_bf, i, rel_bias, swa_sinks[j],
                                           ln_w[i], ln_b[i], rows=PROMPT_ROWS)
            swa_kp.append(kw.reshape(bp, WINDOW, SWA_KVH, SWA_HD))
            swa_vp.append(vw.reshape(bp, WINDOW, SWA_KVH, SWA_HD))
            u = _proj(hs, w_in, SWA_WIDTH, "sample_swa_proj")
            br, ko, vo = _sample_swa_attend(u, csk, csv, j, cmk, cmv, i, rel_bias, swa_sinks[j])
            swa_ks.append(jnp.moveaxis(ko, -1, -3))
            swa_vs.append(jnp.moveaxis(vo, -1, -3))
        else:
            w_in = _pad_cols(hg_w_in[j].astype(BF16))
            hp, st = _hgrn_prompt_layer(hp, w_in, wo[i], mk_bf, mv_bf, i, hg_lb_logits, hg_norm_w[j],
                                        ln_w[i], ln_b[i])
            hg_sp.append(st)
            u = _proj(hs, w_in, HG_WIDTH, "sample_hgrn_proj")
            mix, so = _hgrn_sample_step(u, state_hgrn, j, hg_lb_logits, hg_norm_w[j], i)
            hg_ss.append(so)
            br = _sample_hgrn_gate(u, mix, cmk, cmv, i)
        hs = _outproj_ln(br, hs, wo[i], ln_w[i], ln_b[i])

    return (hp, hs.reshape(ns, 1, d), mem_k_prompt, mem_v_prompt,
            jnp.stack(swa_kp), jnp.stack(swa_vp), jnp.stack(hg_sp),
            jnp.stack(swa_ks), jnp.stack(swa_vs), jnp.stack(hg_ss))
```
